```python
import jax, jax.numpy as jnp
from jax import lax
import numpy as np

D_MODEL = 2048
BATCH = 16
SEQ = 256
DEPTH = 4
DEC_BATCH = 2
DEC_SEQ = 1024
PAST_LEN = 512

GRID_W = 64
ROPE_THETA = 10000.0
EPS = 1e-6
Q_BLOCK = 128
N_BRANCH = 3
H_A = 4
DK_A = 128
DV_A = 256
GLA_GATE_RANK = 16
GLA_GATE_NORM = 16.0
GLA_CHUNK = 16
H_B = 8
NOPE_B = 128
ROPE_B = 64
V_B = 128
QK_B = NOPE_B + ROPE_B
Q_LORA_B = 512
KV_LORA_B = 256
HQ_C = 8
HKV_C = 2
HD_C = 128
D_FF = 5632
IN_SIZES = (H_A * DK_A, H_A * DK_A, H_A * DV_A, H_A * DV_A, 2 * GLA_GATE_RANK,
            Q_LORA_B, KV_LORA_B + ROPE_B,
            HQ_C * HD_C, HKV_C * HD_C, HKV_C * HD_C,
            N_BRANCH * D_MODEL)
N_IN = (2 * H_A * DK_A + 2 * H_A * DV_A + 2 * GLA_GATE_RANK + Q_LORA_B + KV_LORA_B + ROPE_B
        + (HQ_C + 2 * HKV_C) * HD_C + N_BRANCH * D_MODEL)

kernel_name = 'hybrid_diffusion_prefix_gla_mla_gqa_step'


def rmsnorm(x, g):
    xf = x.astype(jnp.float32)
    y = xf * lax.rsqrt(jnp.mean(xf * xf, axis=-1, keepdims=True) + EPS)
    return (y * g.astype(jnp.float32)).astype(x.dtype)


def split_columns(z, sizes):
    offsets, acc = [], 0
    for s in sizes[:-1]:
        acc += s
        offsets.append(acc)
    return jnp.split(z, offsets, axis=-1)


def adaln(cvec, w, b):
    return jnp.split(jax.nn.silu(cvec) @ w + b, 6, axis=-1)


def axial_rope(n_tok, rot_dim):
    rows = n_tok // GRID_W
    row = jnp.repeat(jnp.arange(rows, dtype=jnp.float32), GRID_W)
    col = jnp.tile(jnp.arange(GRID_W, dtype=jnp.float32), rows)
    half = rot_dim // 2
    inv = jnp.power(ROPE_THETA, -jnp.arange(half // 2, dtype=jnp.float32) * (2.0 / half))
    ang = jnp.concatenate([row[:, None] * inv, col[:, None] * inv], axis=-1)
    return jnp.cos(ang), jnp.sin(ang)


def apply_rope(x, cos, sin):
    c = cos[None, :, None, :].astype(x.dtype)
    s = sin[None, :, None, :].astype(x.dtype)
    x1, x2 = x[..., 0::2], x[..., 1::2]
    return jnp.stack([x1 * c - x2 * s, x1 * s + x2 * c], axis=-1).reshape(x.shape)


def rope_tail(x, rope):
    cos, sin = rope
    r = 2 * cos.shape[-1]
    return jnp.concatenate([x[..., :-r], apply_rope(x[..., -r:], cos, sin)], axis=-1)


def gla_chunked(q, k, v, log_a, s0):
    b, t, h, dk = q.shape
    dv = v.shape[-1]
    n = t // GLA_CHUNK

    def chunks(z):
        return z.reshape(b, n, GLA_CHUNK, h, z.shape[-1])

    q, k, v = chunks(q), chunks(k), chunks(v)
    cum = jnp.cumsum(chunks(log_a).astype(jnp.float32), axis=2)
    causal = jnp.tril(jnp.ones((GLA_CHUNK, GLA_CHUNK), dtype=bool))
    diff = cum[:, :, :, None] - cum[:, :, None]
    decay = jnp.where(causal[:, :, None, None], jnp.exp(jnp.minimum(diff, 0.0)), 0.0).astype(q.dtype)
    scores = jnp.einsum('bnthd,bnshd,bntshd->bnhts', q, k, decay)
    o_intra = jnp.einsum('bnhts,bnshv->bnthv', scores, v)
    last = cum[:, :, -1]
    k_dec = k * jnp.exp(last[:, :, None] - cum).astype(k.dtype)
    q_dec = q * jnp.exp(cum).astype(q.dtype)
    kv_chunk = jnp.einsum('bnshd,bnshv->bnhdv', k_dec, v)
    a_last = jnp.exp(last).astype(s0.dtype)

    def step(s, xs):
        qd, kvc, al = xs
        o = jnp.einsum('bthd,bhdv->bthv', qd, s)
        s = (al[..., None] * s + kvc).astype(s.dtype)
        return s, o

    xs = (jnp.moveaxis(q_dec, 1, 0), jnp.moveaxis(kv_chunk, 1, 0), jnp.moveaxis(a_last, 1, 0))
    s_fin, o_inter = lax.scan(step, s0, xs)
    o = o_intra + jnp.moveaxis(o_inter, 0, 1).astype(o_intra.dtype)
    return o.reshape(b, t, h, dv), s_fin


def gla_branch(qa, ka, va, ga, gka, gk_up, gk_bias, norm_g, s0):
    b, t, _ = qa.shape
    q = qa.reshape(b, t, H_A, DK_A) * DK_A ** -0.5
    k = ka.reshape(b, t, H_A, DK_A)
    v = va.reshape(b, t, H_A, DV_A)
    gk = gka.reshape(b, t, 2, GLA_GATE_RANK)
    z = jnp.einsum('btdr,drk->btdk', gk, gk_up) + gk_bias
    log_a = (jax.nn.log_sigmoid(z.astype(jnp.float32)) / GLA_GATE_NORM).reshape(b, t, 2, H_A, DK_A)
    o_f, s_f = gla_chunked(q, k, v, log_a[:, :, 0], s0[:, 0])
    rev = lambda a: jnp.flip(a, axis=1)
    o_b, s_b = gla_chunked(rev(q), rev(k), rev(v), rev(log_a[:, :, 1]), s0[:, 1])
    o = rmsnorm(o_f + rev(o_b), norm_g) * jax.nn.silu(ga.reshape(b, t, H_A, DV_A))
    return o.reshape(b, t, H_A * DV_A), jnp.stack([s_f, s_b], axis=1)


def mla_queries(q_lora, qa_g, w_qb, q_g, rope):
    b, t, _ = q_lora.shape
    q = (rmsnorm(q_lora, qa_g) @ w_qb).reshape(b, t, H_B, QK_B)
    q = rmsnorm(q, q_g)
    return q if rope is None else rope_tail(q, rope)


def mla_keys_values(ckv, krope, w_kvb, k_g, rope):
    b, t, _ = ckv.shape
    kv = (ckv @ w_kvb).reshape(b, t, H_B, NOPE_B + V_B)
    k_nope, v = kv[..., :NOPE_B], kv[..., NOPE_B:]
    k = jnp.concatenate([k_nope, jnp.broadcast_to(krope[:, :, None], (b, t, H_B, ROPE_B))], axis=-1)
    k = rmsnorm(k, k_g)
    return (k if rope is None else rope_tail(k, rope)), v


def block_attention(q, k, v):
    b, t, hq, d = q.shape
    hkv, dv = k.shape[2], v.shape[-1]
    g = hq // hkv
    nb = t // Q_BLOCK
    qb = q.reshape(b, nb, Q_BLOCK, hkv, g, d).transpose(1, 0, 2, 3, 4, 5)
    scale = d ** -0.5

    def one_block(qblk):
        s = jnp.einsum('bqhgd,bkhd->bhgqk', qblk, k).astype(jnp.float32) * scale
        p = jax.nn.softmax(s, axis=-1).astype(v.dtype)
        return jnp.einsum('bhgqk,bkhv->bqhgv', p, v)

    o = lax.map(one_block, qb)
    return o.transpose(1, 0, 2, 3, 4, 5).reshape(b, t, hq * dv)


def gated_merge(y_a, y_b, y_c, gates, w_out_a, w_out_b, w_out_c, w_o):
    g_a, g_b, g_c = jnp.split(jax.nn.sigmoid(gates), N_BRANCH, axis=-1)
    m = g_a * (y_a @ w_out_a) + g_b * (y_b @ w_out_b) + g_c * (y_c @ w_out_c)
    return m @ w_o


def conv_ffn(h, w_up, conv_w, conv_b, w_down):
    u = h @ w_up
    up = jnp.pad(u, ((0, 0), (1, 1), (0, 0)))
    u = up[:, :-2] * conv_w[0] + up[:, 1:-1] * conv_w[1] + up[:, 2:] * conv_w[2] + conv_b
    a, g = jnp.split(u, 2, axis=-1)
    return (jax.nn.silu(g) * a) @ w_down


def trunk_layer(x, mod, p, ropes, ctx):
    shift1, scale1, gate1, shift2, scale2, gate2 = mod
    b, t, _ = x.shape
    h = rmsnorm(x, p['norm1_g']) * (1 + scale1) + shift1
    qa, ka, va, ga, gka, q_lora, kv_a, qc, kc, vc, gates = split_columns(h @ p['w_in'], IN_SIZES)

    s0 = jnp.zeros((b, 2, H_A, DK_A, DV_A), x.dtype) if ctx is None else ctx[0]
    y_a, gla_state = gla_branch(qa, ka, va, ga, gka, p['gla_gk_up'], p['gla_gk_bias'], p['gla_norm_g'], s0)

    ckv = rmsnorm(kv_a[..., :KV_LORA_B], p['mla_kva_norm_g'])
    krope = kv_a[..., KV_LORA_B:]
    rope_b = None if ropes is None else ropes[0]
    q_b = mla_queries(q_lora, p['mla_qa_norm_g'], p['mla_w_qb'], p['mla_q_norm_g'], rope_b)
    k_b, v_b = mla_keys_values(ckv, krope, p['mla_w_kvb'], p['mla_k_norm_g'], rope_b)
    if ctx is not None:
        k_bc, v_bc = mla_keys_values(ctx[1], ctx[2], p['mla_w_kvb'], p['mla_k_norm_g'], None)
        k_b = jnp.concatenate([k_b, k_bc], axis=1)
        v_b = jnp.concatenate([v_b, v_bc], axis=1)
    y_b = block_attention(q_b, k_b, v_b)

    q_c = rmsnorm(qc.reshape(b, t, HQ_C, HD_C), p['gqa_q_norm_g'])
    k_c = rmsnorm(kc.reshape(b, t, HKV_C, HD_C), p['gqa_k_norm_g'])
    v_c = vc.reshape(b, t, HKV_C, HD_C)
    if ctx is None:
        y_c = block_attention(q_c, k_c, v_c)
    else:
        cos_c, sin_c = ropes[1]
        k_all = jnp.concatenate([apply_rope(k_c, cos_c, sin_c), ctx[3]], axis=1)
        v_all = jnp.concatenate([v_c, ctx[4]], axis=1)
        y_c = block_attention(apply_rope(q_c, cos_c, sin_c), k_all, v_all)

    x = x + gate1 * gated_merge(y_a, y_b, y_c, gates, p['w_out_a'], p['w_out_b'], p['w_out_c'], p['w_o'])
    h2 = rmsnorm(x, p['norm2_g']) * (1 + scale2) + shift2
    x = x + gate2 * conv_ffn(h2, p['ffn_w_up'], p['ffn_conv_w'], p['ffn_conv_b'], p['ffn_w_down'])
    return x, (gla_state, ckv, krope, k_c, v_c)


def setup_inputs(seed: int = 0) -> dict:
    key = jax.random.key(seed)
    keys = jax.random.split(key, 33)

    def nrm(i, shape, scale=1.0):
        return jax.random.normal(keys[i], shape, jnp.float32) * scale

    def gain(i, shape):
        return 1.0 + nrm(i, shape, 0.02)

    L, D = DEPTH, D_MODEL
    return {
        'x_prompt': nrm(0, (BATCH, SEQ, D)),
        'x_sample': nrm(1, (DEC_BATCH, DEC_SEQ, D)),
        'c': nrm(2, (DEC_BATCH, D)),
        'state_gla': nrm(3, (DEC_BATCH, L, 2, H_A, DK_A, DV_A)),
        'cache_mla_ckv': nrm(4, (DEC_BATCH, L, PAST_LEN, KV_LORA_B)),
        'cache_mla_krope': nrm(5, (DEC_BATCH, L, PAST_LEN, ROPE_B)),
        'cache_gqa_k': nrm(6, (DEC_BATCH, L, PAST_LEN, HKV_C, HD_C)),
        'cache_gqa_v': nrm(7, (DEC_BATCH, L, PAST_LEN, HKV_C, HD_C)),
        'c_ctx': nrm(8, (D,)),
        'norm1_g': gain(9, (L, D)),
        'norm2_g': gain(10, (L, D)),
        'w_ada': nrm(11, (L, D, 6 * D), 0.5 * D ** -0.5),
        'b_ada': nrm(12, (L, 6 * D), 0.02),
        'w_in': nrm(13, (L, D, N_IN), D ** -0.5),
        'gla_gk_up': nrm(14, (L, 2, GLA_GATE_RANK, H_A * DK_A), GLA_GATE_RANK ** -0.5),
        'gla_gk_bias': nrm(15, (L, 2, H_A * DK_A), 0.1),
        'gla_norm_g': gain(16, (L, DV_A)),
        'mla_qa_norm_g': gain(17, (L, Q_LORA_B)),
        'mla_w_qb': nrm(18, (L, Q_LORA_B, H_B * QK_B), Q_LORA_B ** -0.5),
        'mla_kva_norm_g': gain(19, (L, KV_LORA_B)),
        'mla_w_kvb': nrm(20, (L, KV_LORA_B, H_B * (NOPE_B + V_B)), KV_LORA_B ** -0.5),
        'mla_q_norm_g': gain(21, (L, QK_B)),
        'mla_k_norm_g': gain(22, (L, QK_B)),
        'gqa_q_norm_g': gain(23, (L, HD_C)),
        'gqa_k_norm_g': gain(24, (L, HD_C)),
        'w_out_a': nrm(25, (L, H_A * DV_A, D), (H_A * DV_A) ** -0.5),
        'w_out_b': nrm(26, (L, H_B * V_B, D), (H_B * V_B) ** -0.5),
        'w_out_c': nrm(27, (L, HQ_C * HD_C, D), (HQ_C * HD_C) ** -0.5),
        'w_o': nrm(28, (L, D, D), D ** -0.5),
        'ffn_w_up': nrm(29, (L, D, 2 * D_FF), D ** -0.5),
        'ffn_conv_w': nrm(30, (L, 3, 2 * D_FF), 3 ** -0.5),
        'ffn_conv_b': nrm(31, (L, 2 * D_FF), 0.02),
        'ffn_w_down': nrm(32, (L, D_FF, D), D_FF ** -0.5),
    }


def reference(x_prompt, x_sample, c, state_gla, cache_mla_ckv, cache_mla_krope, cache_gqa_k, cache_gqa_v,
              c_ctx, norm1_g, norm2_g, w_ada, b_ada, w_in, gla_gk_up, gla_gk_bias, gla_norm_g,
              mla_qa_norm_g, mla_w_qb, mla_kva_norm_g, mla_w_kvb, mla_q_norm_g, mla_k_norm_g,
              gqa_q_norm_g, gqa_k_norm_g, w_out_a, w_out_b, w_out_c, w_o,
              ffn_w_up, ffn_conv_w, ffn_conv_b, ffn_w_down):
    n_lat = x_sample.shape[1]
    ropes = (axial_rope(n_lat, ROPE_B), axial_rope(n_lat, HD_C))
    y_p, y_s = x_prompt, x_sample
    ctx_out = []
    for l in range(DEPTH):
        p = {
            'norm1_g': norm1_g[l], 'norm2_g': norm2_g[l], 'w_in': w_in[l],
            'gla_gk_up': gla_gk_up[l], 'gla_gk_bias': gla_gk_bias[l], 'gla_norm_g': gla_norm_g[l],
            'mla_qa_norm_g': mla_qa_norm_g[l], 'mla_w_qb': mla_w_qb[l],
            'mla_kva_norm_g': mla_kva_norm_g[l], 'mla_w_kvb': mla_w_kvb[l],
            'mla_q_norm_g': mla_q_norm_g[l], 'mla_k_norm_g': mla_k_norm_g[l],
            'gqa_q_norm_g': gqa_q_norm_g[l], 'gqa_k_norm_g': gqa_k_norm_g[l],
            'w_out_a': w_out_a[l], 'w_out_b': w_out_b[l], 'w_out_c': w_out_c[l], 'w_o': w_o[l],
            'ffn_w_up': ffn_w_up[l], 'ffn_conv_w': ffn_conv_w[l], 'ffn_conv_b': ffn_conv_b[l],
            'ffn_w_down': ffn_w_down[l],
        }
        y_p, st = trunk_layer(y_p, adaln(c_ctx, w_ada[l], b_ada[l]), p, None, None)
        ctx_out.append(st)
        cached = (state_gla[:, l], cache_mla_ckv[:, l], cache_mla_krope[:, l], cache_gqa_k[:, l], cache_gqa_v[:, l])
        y_s, _ = trunk_layer(y_s, adaln(c[:, None, :], w_ada[l], b_ada[l]), p, ropes, cached)
    new_state_gla = jnp.stack([s[0] for s in ctx_out], axis=1)
    new_cache_mla_ckv = jnp.stack([s[1] for s in ctx_out], axis=1)
    new_cache_mla_krope = jnp.stack([s[2] for s in ctx_out], axis=1)
    new_cache_gqa_k = jnp.stack([s[3] for s in ctx_out], axis=1)
    new_cache_gqa_v = jnp.stack([s[4] for s in ctx_out], axis=1)
    return (y_p, y_s, new_state_gla, new_cache_mla_ckv, new_cache_mla_krope, new_cache_gqa_k, new_cache_gqa_v)
```

```python
import functools

import numpy as np
import jax
import jax.numpy as jnp
from jax import lax
from jax.experimental import pallas as pl
from jax.experimental.pallas import tpu as pltpu

F32 = jnp.float32
BF16 = jnp.bfloat16

D_MODEL = 2048
BATCH = 16
SEQ = 256
DEPTH = 4
DEC_BATCH = 2
DEC_SEQ = 1024
PAST_LEN = 512
GRID_W = 64
ROPE_THETA = 10000.0
EPS = 1e-6
H_A, DK_A, DV_A = 4, 128, 256
GLA_GATE_RANK = 16
GLA_GATE_NORM = 16.0
H_B, NOPE_B, ROPE_B, V_B = 8, 128, 64, 128
QK_B = NOPE_B + ROPE_B
Q_LORA_B = 512
KV_LORA_B = 256
HQ_C, HKV_C, HD_C = 8, 2, 128
D_FF = 5632

T_P = BATCH * SEQ
T_S = DEC_BATCH * DEC_SEQ
T_ALL = T_P + T_S
N_GROUPS = 1 + DEC_BATCH
MOD_ROWS = 8

_O_QA, _O_KA, _O_VA, _O_GA = 0, 512, 1024, 2048
_O_GKA = 3072
_O_QLORA = 3104
_O_KVA = 3616
_O_QC, _O_KC, _O_VC = 3936, 4960, 5216
_O_GATES = 5472
N_IN = 11616
Z_QA, Z_KA, Z_VA, Z_GA = 0, 512, 1024, 2048
Z_QC = 3072
Z_QLORA = 4096
Z_KC, Z_VC = 4608, 4864
Z_GATES = 5120
N_MAIN = 11264
N_TAIL = 384
PAD_QK_B = 256

GLA_CHUNK = 256
GLA_LEVELS = 8

VMEM_LIMIT = 56 * 1024 * 1024


def _cparams(n_axes):
    return pltpu.CompilerParams(
        dimension_semantics=("arbitrary",) * n_axes, vmem_limit_bytes=VMEM_LIMIT)


def _group_of_block(i, bm):
    n_p = T_P // bm
    per = DEC_SEQ // bm
    return jnp.where(i < n_p, 0, 1 + (i - n_p) // per)


def _adaln_kernel(c_ref, w_ref, b_ref, o_ref):
    c = c_ref[...]
    a = (c * jax.nn.sigmoid(c)).astype(BF16)
    o_ref[...] = jnp.dot(a, w_ref[...].astype(BF16), preferred_element_type=F32) + b_ref[...]


def adaln_all(cvec, w_ada, b_ada):
    bn = 1024
    n = 6 * D_MODEL
    return pl.pallas_call(
        _adaln_kernel,
        out_shape=jax.ShapeDtypeStruct((DEPTH, MOD_ROWS, n), F32),
        grid=(DEPTH, n // bn),
        in_specs=[
            pl.BlockSpec((MOD_ROWS, D_MODEL), lambda l, j: (0, 0)),
            pl.BlockSpec((None, D_MODEL, bn), lambda l, j: (l, 0, j)),
            pl.BlockSpec((None, 1, bn), lambda l, j: (l, 0, j)),
        ],
        out_specs=pl.BlockSpec((None, MOD_ROWS, bn), lambda l, j: (l, 0, j)),
        compiler_params=_cparams(2),
        name="adaln",
    )(cvec, w_ada, b_ada.reshape(DEPTH, 1, n))


def _norm_mod_kernel(x_ref, g_ref, sh_ref, sc_ref, o_ref, *, bm):
    grp = _group_of_block(pl.program_id(0), bm)
    x = x_ref[...]
    y = x * lax.rsqrt(jnp.mean(x * x, axis=-1, keepdims=True) + EPS) * g_ref[...]
    sh = sh_ref[pl.ds(grp, 1), :]
    sc = sc_ref[pl.ds(grp, 1), :]
    o_ref[...] = (y * (1.0 + sc) + sh).astype(o_ref.dtype)


def norm_mod(x, g, mods, l, which_shift, which_scale):
    bm = 512
    nd = D_MODEL
    return pl.pallas_call(
        functools.partial(_norm_mod_kernel, bm=bm),
        out_shape=jax.ShapeDtypeStruct((T_ALL, nd), BF16),
        grid=(T_ALL // bm,),
        in_specs=[
            pl.BlockSpec((bm, nd), lambda i: (i, 0)),
            pl.BlockSpec((None, 1, nd), lambda i: (l, 0, 0)),
            pl.BlockSpec((None, MOD_ROWS, nd), lambda i: (l, 0, which_shift)),
            pl.BlockSpec((None, MOD_ROWS, nd), lambda i: (l, 0, which_scale)),
        ],
        out_specs=pl.BlockSpec((bm, nd), lambda i: (i, 0)),
        compiler_params=_cparams(1),
        name="norm_mod",
    )(x, g.reshape(DEPTH, 1, nd), mods, mods)


def _mm_kernel(x_ref, w_ref, o_ref, wb_ref):
    @pl.when(pl.program_id(1) == 0)
    def _():
        wb_ref[...] = w_ref[...].astype(BF16)

    o_ref[...] = jnp.dot(x_ref[...], wb_ref[...], preferred_element_type=F32).astype(o_ref.dtype)


def matmul(x, w, l, *, bm, bn, out_dtype, name):
    m, k = x.shape
    n = w.shape[-1]
    return pl.pallas_call(
        _mm_kernel,
        out_shape=jax.ShapeDtypeStruct((m, n), out_dtype),
        grid=(n // bn, m // bm),
        in_specs=[
            pl.BlockSpec((bm, k), lambda j, i: (i, 0)),
            pl.BlockSpec((None, k, bn), lambda j, i: (l, 0, j)),
        ],
        out_specs=pl.BlockSpec((bm, bn), lambda j, i: (i, j)),
        scratch_shapes=[pltpu.VMEM((k, bn), BF16)],
        compiler_params=_cparams(2),
        name=name,
    )(x, w)


def _mm_res_kernel(a_ref, w_ref, x_ref, gate_ref, o_ref, wb_ref, *, bm):
    @pl.when(pl.program_id(1) == 0)
    def _():
        wb_ref[...] = w_ref[...].astype(BF16)

    grp = _group_of_block(pl.program_id(1), bm)
    acc = jnp.dot(a_ref[...], wb_ref[...], preferred_element_type=F32)
    o_ref[...] = x_ref[...] + gate_ref[pl.ds(grp, 1), :] * acc


def matmul_residual(a, w, l, x, mods, which_gate, *, bm, bn, name):
    m, k = a.shape
    n = w.shape[-1]
    gate_blk = which_gate * (D_MODEL // bn)
    return pl.pallas_call(
        functools.partial(_mm_res_kernel, bm=bm),
        out_shape=jax.ShapeDtypeStruct((m, n), F32),
        grid=(n // bn, m // bm),
        in_specs=[
            pl.BlockSpec((bm, k), lambda j, i: (i, 0)),
            pl.BlockSpec((None, k, bn), lambda j, i: (l, 0, j)),
            pl.BlockSpec((bm, bn), lambda j, i: (i, j)),
            pl.BlockSpec((None, MOD_ROWS, bn), lambda j, i: (l, 0, gate_blk + j)),
        ],
        out_specs=pl.BlockSpec((bm, bn), lambda j, i: (i, j)),
        scratch_shapes=[pltpu.VMEM((k, bn), BF16)],
        compiler_params=_cparams(2),
        name=name,
    )(a, w, x, mods)


def _merge_kernel(ya_ref, yb_ref, yc_ref, wa_ref, wb_ref, wc_ref, ga_ref, gb_ref, gc_ref,
                  o_ref, wab_ref, wbb_ref, wcb_ref):
    @pl.when(pl.program_id(1) == 0)
    def _():
        wab_ref[...] = wa_ref[...].astype(BF16)
        wbb_ref[...] = wb_ref[...].astype(BF16)
        wcb_ref[...] = wc_ref[...].astype(BF16)

    def branch(y_ref, w_ref, g_ref):
        return jax.nn.sigmoid(g_ref[...].astype(F32)) * jnp.dot(
            y_ref[...], w_ref[...], preferred_element_type=F32)

    m = branch(ya_ref, wab_ref, ga_ref) + branch(yb_ref, wbb_ref, gb_ref) + branch(yc_ref, wcb_ref, gc_ref)
    o_ref[...] = m.astype(o_ref.dtype)


def gated_merge(y_a, y_b, y_c, z_main, w_out_a, w_out_b, w_out_c, l):
    bm, bn = 1024, 512
    k = 1024
    gate0 = Z_GATES // bn
    per = D_MODEL // bn
    y_spec = pl.BlockSpec((bm, k), lambda j, i: (i, 0))
    w_spec = pl.BlockSpec((None, k, bn), lambda j, i: (l, 0, j))

    def g_spec(br):
        return pl.BlockSpec((bm, bn), lambda j, i: (i, gate0 + br * per + j))

    return pl.pallas_call(
        _merge_kernel,
        out_shape=jax.ShapeDtypeStruct((T_ALL, D_MODEL), BF16),
        grid=(D_MODEL // bn, T_ALL // bm),
        in_specs=[y_spec, y_spec, y_spec, w_spec, w_spec, w_spec, g_spec(0), g_spec(1), g_spec(2)],
        out_specs=pl.BlockSpec((bm, bn), lambda j, i: (i, j)),
        scratch_shapes=[pltpu.VMEM((k, bn), BF16)] * 3,
        compiler_params=_cparams(2),
        name="gated_merge",
    )(y_a, y_b, y_c, w_out_a, w_out_b, w_out_c, z_main, z_main, z_main)


def _up_conv_kernel(h_ref, wa_ref, wg_ref, cwa_ref, cwg_ref, cba_ref, cbg_ref, o_ref,
                    wab_ref, wgb_ref, *, bm):
    i = pl.program_id(1)

    @pl.when(i == 0)
    def _():
        wab_ref[...] = wa_ref[...].astype(BF16)
        wgb_ref[...] = wg_ref[...].astype(BF16)

    seq = jnp.where(i < T_P // bm, SEQ, DEC_SEQ)
    pos = lax.broadcasted_iota(jnp.int32, (bm, 1), 0) & (seq - 1)
    first = pos == 0
    last = pos == seq - 1
    h = h_ref[...]

    def conv(w_ref, cw_ref, cb_ref):
        u = jnp.dot(h, w_ref[...], preferred_element_type=F32)
        prev = jnp.where(first, 0.0, pltpu.roll(u, 1, 0))
        nxt = jnp.where(last, 0.0, pltpu.roll(u, bm - 1, 0))
        return prev * cw_ref[0:1, :] + u * cw_ref[1:2, :] + nxt * cw_ref[2:3, :] + cb_ref[...]

    a = conv(wab_ref, cwa_ref, cba_ref)
    g = conv(wgb_ref, cwg_ref, cbg_ref)
    o_ref[...] = (g * jax.nn.sigmoid(g) * a).astype(o_ref.dtype)


def up_conv(h2, w_up, conv_w, conv_b, l):
    bm, bn = 1024, 512
    nb = D_FF // bn
    cb = conv_b.reshape(DEPTH, 1, 2 * D_FF)
    return pl.pallas_call(
        functools.partial(_up_conv_kernel, bm=bm),
        out_shape=jax.ShapeDtypeStruct((T_ALL, D_FF), BF16),
        grid=(nb, T_ALL // bm),
        in_specs=[
            pl.BlockSpec((bm, D_MODEL), lambda j, i: (i, 0)),
            pl.BlockSpec((None, D_MODEL, bn), lambda j, i: (l, 0, j)),
            pl.BlockSpec((None, D_MODEL, bn), lambda j, i: (l, 0, nb + j)),
            pl.BlockSpec((None, 3, bn), lambda j, i: (l, 0, j)),
            pl.BlockSpec((None, 3, bn), lambda j, i: (l, 0, nb + j)),
            pl.BlockSpec((None, 1, bn), lambda j, i: (l, 0, j)),
            pl.BlockSpec((None, 1, bn), lambda j, i: (l, 0, nb + j)),
        ],
        out_specs=pl.BlockSpec((bm, bn), lambda j, i: (i, j)),
        scratch_shapes=[pltpu.VMEM((D_MODEL, bn), BF16)] * 2,
        compiler_params=_cparams(2),
        name="up_conv",
    )(h2, w_up, w_up, conv_w, conv_w, cb, cb)


def _rope_tables(n_tok, rot_dim, width):
    rows = n_tok // GRID_W
    row = jnp.repeat(jnp.arange(rows, dtype=F32), GRID_W)
    col = jnp.tile(jnp.arange(GRID_W, dtype=F32), rows)
    half = rot_dim // 2
    inv = jnp.power(ROPE_THETA, -jnp.arange(half // 2, dtype=F32) * (2.0 / half))
    ang = jnp.concatenate([row[:, None] * inv, col[:, None] * inv], axis=-1)
    cos, sin = jnp.cos(ang), jnp.sin(ang)
    zero = jnp.zeros_like(sin)
    c2 = jnp.stack([cos, cos], axis=-1).reshape(n_tok, rot_dim)
    s_even = jnp.stack([-sin, zero], axis=-1).reshape(n_tok, rot_dim)
    s_odd = jnp.stack([zero, sin], axis=-1).reshape(n_tok, rot_dim)
    pad = ((0, 0), (0, width - rot_dim))
    return jnp.stack([jnp.pad(c2, pad), jnp.pad(s_even, pad), jnp.pad(s_odd, pad)], axis=0)


def _rope(x, tab_ref):
    w = x.shape[-1]
    return (x * tab_ref[0] + pltpu.roll(x, w - 1, 1) * tab_ref[1] + pltpu.roll(x, 1, 1) * tab_ref[2])


def _mla_q_kernel(ql_ref, g1_ref, w_ref, g2_ref, *rest, rope):
    if rope:
        tab_ref, o_ref, wb_ref = rest
    else:
        o_ref, wb_ref = rest

    @pl.when(pl.program_id(0) == 0)
    def _():
        wb_ref[...] = w_ref[...].astype(BF16)

    x = ql_ref[...].astype(F32)
    xn = x * lax.rsqrt(jnp.mean(x * x, axis=-1, keepdims=True) + EPS) * g1_ref[...]
    q = jnp.dot(xn.astype(BF16), wb_ref[...], preferred_element_type=F32)
    scale = QK_B ** -0.5
    for h in range(H_B):
        lo = h * PAD_QK_B
        qn = q[:, lo:lo + NOPE_B]
        qr = q[:, lo + NOPE_B:lo + PAD_QK_B]
        ms = (jnp.sum(qn * qn, axis=-1, keepdims=True) + jnp.sum(qr * qr, axis=-1, keepdims=True)) / QK_B
        r = lax.rsqrt(ms + EPS)
        qn = qn * r * g2_ref[:, 0:NOPE_B]
        qr = qr * r * g2_ref[:, NOPE_B:PAD_QK_B]
        if rope:
            qr = _rope(qr, tab_ref)
        o_ref[:, lo:lo + NOPE_B] = (qn * scale).astype(o_ref.dtype)
        o_ref[:, lo + NOPE_B:lo + PAD_QK_B] = (qr * scale).astype(o_ref.dtype)


def mla_q_proj(z_main, qa_g, w_qb_pad, q_g_pad, l, *, row0, rows, tab):
    bm = 512
    rope = tab is not None
    kq = Q_LORA_B
    n = H_B * PAD_QK_B
    rb0 = row0 // bm
    in_specs = [
        pl.BlockSpec((bm, kq), lambda i: (rb0 + i, Z_QLORA // kq)),
        pl.BlockSpec((None, 1, kq), lambda i: (l, 0, 0)),
        pl.BlockSpec((None, kq, n), lambda i: (l, 0, 0)),
        pl.BlockSpec((None, 1, PAD_QK_B), lambda i: (l, 0, 0)),
    ]
    args = [z_main, qa_g.reshape(DEPTH, 1, kq), w_qb_pad, q_g_pad]
    if rope:
        per = DEC_SEQ // bm
        in_specs.append(pl.BlockSpec((3, bm, 128), lambda i: (0, i % per, 0)))
        args.append(tab)
    return pl.pallas_call(
        functools.partial(_mla_q_kernel, rope=rope),
        out_shape=jax.ShapeDtypeStruct((rows, n), BF16),
        grid=(rows // bm,),
        in_specs=in_specs,
        out_specs=pl.BlockSpec((bm, n), lambda i: (i, 0)),
        scratch_shapes=[pltpu.VMEM((kq, n), BF16)],
        compiler_params=_cparams(1),
        name="mla_q_proj",
    )(*args)


def _mla_kv_kernel(c_ref, kr_ref, g1_ref, w_ref, g2_ref, *rest, norm_in, rope):
    rest = list(rest)
    tab_ref = rest.pop(0) if rope else None
    if norm_in:
        k_ref, v_ref, ckv_ref, wb_ref = rest
    else:
        k_ref, v_ref, wb_ref = rest

    @pl.when(pl.program_id(0) == 0)
    def _():
        wb_ref[...] = w_ref[...].astype(BF16)

    c = c_ref[...].astype(F32)
    if norm_in:
        c = c * lax.rsqrt(jnp.mean(c * c, axis=-1, keepdims=True) + EPS) * g1_ref[...]
        ckv_ref[...] = c
    kv = jnp.dot(c.astype(BF16), wb_ref[...], preferred_element_type=F32)
    kr = kr_ref[...].astype(F32)
    if kr.shape[-1] == 128:
        lane = lax.broadcasted_iota(jnp.int32, kr.shape, 1)
        kr = jnp.where(lane < ROPE_B, kr, 0.0)
    else:
        kr = jnp.concatenate([kr, jnp.zeros_like(kr)], axis=-1)
    kr_ss = jnp.sum(kr * kr, axis=-1, keepdims=True)
    for h in range(H_B):
        kn = kv[:, h * 256:h * 256 + NOPE_B]
        v = kv[:, h * 256 + NOPE_B:(h + 1) * 256]
        r = lax.rsqrt((jnp.sum(kn * kn, axis=-1, keepdims=True) + kr_ss) / QK_B + EPS)
        krh = kr * r * g2_ref[:, NOPE_B:PAD_QK_B]
        if rope:
            krh = _rope(krh, tab_ref)
        k_ref[:, h * PAD_QK_B:h * PAD_QK_B + NOPE_B] = (kn * r * g2_ref[:, 0:NOPE_B]).astype(k_ref.dtype)
        k_ref[:, h * PAD_QK_B + NOPE_B:(h + 1) * PAD_QK_B] = krh.astype(k_ref.dtype)
        v_ref[:, h * V_B:(h + 1) * V_B] = v.astype(v_ref.dtype)


def mla_kv_proj(c_arr, c_spec, kr_arr, kr_spec, kva_g, w_kvb, k_g_pad, l, *, rows, bm, norm_in, tab):
    rope = tab is not None
    kc = KV_LORA_B
    n = H_B * (NOPE_B + V_B)
    in_specs = [
        c_spec, kr_spec,
        pl.BlockSpec((None, 1, kc), lambda i: (l, 0, 0)),
        pl.BlockSpec((None, kc, n), lambda i: (l, 0, 0)),
        pl.BlockSpec((None, 1, PAD_QK_B), lambda i: (l, 0, 0)),
    ]
    args = [c_arr, kr_arr, kva_g.reshape(DEPTH, 1, kc), w_kvb, k_g_pad]
    if rope:
        per = DEC_SEQ // bm
        in_specs.append(pl.BlockSpec((3, bm, 128), lambda i: (0, i % per, 0)))
        args.append(tab)
    out_shape = [jax.ShapeDtypeStruct((rows, H_B * PAD_QK_B), BF16),
                 jax.ShapeDtypeStruct((rows, H_B * V_B), BF16)]
    out_specs = [pl.BlockSpec((bm, H_B * PAD_QK_B), lambda i: (i, 0)),
                 pl.BlockSpec((bm, H_B * V_B), lambda i: (i, 0))]
    if norm_in:
        out_shape.append(jax.ShapeDtypeStruct((rows, kc), F32))
        out_specs.append(pl.BlockSpec((bm, kc), lambda i: (i, 0)))
    return pl.pallas_call(
        functools.partial(_mla_kv_kernel, norm_in=norm_in, rope=rope),
        out_shape=out_shape,
        grid=(rows // bm,),
        in_specs=in_specs,
        out_specs=out_specs,
        scratch_shapes=[pltpu.VMEM((kc, n), BF16)],
        compiler_params=_cparams(1),
        name="mla_kv_proj",
    )(*args)


def _gqa_prep_kernel(q_ref, k_ref, v_ref, gq_ref, gk_ref, *rest, rope, emit_kn):
    rest = list(rest)
    tab_ref = rest.pop(0) if rope else None
    qo_ref, ko_ref, vo_ref = rest[:3]
    kn_ref = rest[3] if emit_kn else None
    scale = HD_C ** -0.5

    def norm(x, g_ref):
        return x * lax.rsqrt(jnp.mean(x * x, axis=-1, keepdims=True) + EPS) * g_ref[...]

    for h in range(HQ_C):
        x = norm(q_ref[:, h * HD_C:(h + 1) * HD_C].astype(F32), gq_ref)
        if rope:
            x = _rope(x, tab_ref)
        qo_ref[:, h * HD_C:(h + 1) * HD_C] = (x * scale).astype(qo_ref.dtype)
    for h in range(HKV_C):
        x = norm(k_ref[:, h * HD_C:(h + 1) * HD_C].astype(F32), gk_ref)
        if emit_kn:
            kn_ref[:, h * HD_C:(h + 1) * HD_C] = x
        if rope:
            x = _rope(x, tab_ref)
        ko_ref[:, h * HD_C:(h + 1) * HD_C] = x.astype(ko_ref.dtype)
    vo_ref[...] = v_ref[...].astype(vo_ref.dtype)


def gqa_prep(z_main, gq, gk, l, *, row0, rows, tab, emit_kn):
    bm = 512
    rope = tab is not None
    rb0 = row0 // bm
    nq, nk = HQ_C * HD_C, HKV_C * HD_C
    in_specs = [
        pl.BlockSpec((bm, nq), lambda i: (rb0 + i, Z_QC // nq)),
        pl.BlockSpec((bm, nk), lambda i: (rb0 + i, Z_KC // nk)),
        pl.BlockSpec((bm, nk), lambda i: (rb0 + i, Z_VC // nk)),
        pl.BlockSpec((None, 1, HD_C), lambda i: (l, 0, 0)),
        pl.BlockSpec((None, 1, HD_C), lambda i: (l, 0, 0)),
    ]
    args = [z_main, z_main, z_main, gq.reshape(DEPTH, 1, HD_C), gk.reshape(DEPTH, 1, HD_C)]
    if rope:
        per = DEC_SEQ // bm
        in_specs.append(pl.BlockSpec((3, bm, 128), lambda i: (0, i % per, 0)))
        args.append(tab)
    out_shape = [jax.ShapeDtypeStruct((rows, nq), BF16),
                 jax.ShapeDtypeStruct((rows, nk), BF16),
                 jax.ShapeDtypeStruct((rows, nk), BF16)]
    out_specs = [pl.BlockSpec((bm, nq), lambda i: (i, 0)),
                 pl.BlockSpec((bm, nk), lambda i: (i, 0)),
                 pl.BlockSpec((bm, nk), lambda i: (i, 0))]
    if emit_kn:
        out_shape.append(jax.ShapeDtypeStruct((rows, nk), F32))
        out_specs.append(pl.BlockSpec((bm, nk), lambda i: (i, 0)))
    return pl.pallas_call(
        functools.partial(_gqa_prep_kernel, rope=rope, emit_kn=emit_kn),
        out_shape=out_shape,
        grid=(rows // bm,),
        in_specs=in_specs,
        out_specs=out_specs,
        compiler_params=_cparams(1),
        name="gqa_prep",
    )(*args)


def _attn_kernel(*refs, n_seg, seg_rows, hq, hkv, dqk, dv, tq):
    q_ref = refs[0]
    kv_refs = refs[1:1 + 2 * n_seg]
    o_ref = refs[1 + 2 * n_seg]
    k_sc, v_sc = refs[2 + 2 * n_seg:]
    g = hq // hkv
    nq = q_ref.shape[0] // tq
    for j in range(hkv):
        r = 0
        for s in range(n_seg):
            k_sc[r:r + seg_rows[s], :] = kv_refs[2 * s][:, j * dqk:(j + 1) * dqk].astype(BF16)
            v_sc[r:r + seg_rows[s], :] = kv_refs[2 * s + 1][:, j * dv:(j + 1) * dv].astype(BF16)
            r += seg_rows[s]
        for hh in range(g):
            h = j * g + hh

            def body(qb, carry, h=h):
                r0 = pl.multiple_of(qb * tq, tq)
                q = q_ref[pl.ds(r0, tq), h * dqk:(h + 1) * dqk]
                s_ = lax.dot_general(q, k_sc[...], (((1,), (1,)), ((), ())), preferred_element_type=F32)
                m = jnp.max(s_, axis=-1, keepdims=True)
                p = jnp.exp(s_ - m)
                den = jnp.sum(p, axis=-1, keepdims=True)
                o = jnp.dot(p.astype(BF16), v_sc[...], preferred_element_type=F32) / den
                o_ref[pl.ds(r0, tq), h * dv:(h + 1) * dv] = o.astype(o_ref.dtype)
                return carry

            if nq == 1:
                body(0, 0)
            else:
                lax.fori_loop(0, nq, body, 0)


def attention(q, segs, *, nb, q_rows, hq, hkv, dqk, dv, name):
    tq = 256
    in_specs = [pl.BlockSpec((q_rows, hq * dqk), lambda b: (b, 0))]
    args = [q]
    seg_rows = []
    for k_arr, k_spec, v_arr, v_spec, rows in segs:
        in_specs += [k_spec, v_spec]
        args += [k_arr, v_arr]
        seg_rows.append(rows)
    s_tot = sum(seg_rows)
    return pl.pallas_call(
        functools.partial(_attn_kernel, n_seg=len(segs), seg_rows=tuple(seg_rows), hq=hq, hkv=hkv,
                          dqk=dqk, dv=dv, tq=tq),
        out_shape=jax.ShapeDtypeStruct((nb * q_rows, hq * dv), BF16),
        grid=(nb,),
        in_specs=in_specs,
        out_specs=pl.BlockSpec((q_rows, hq * dv), lambda b: (b, 0)),
        scratch_shapes=[pltpu.VMEM((s_tot, dqk), BF16), pltpu.VMEM((s_tot, dv), BF16)],
        compiler_params=_cparams(1),
        name=name,
    )(*args)


def _rows_spec(rows, width, col_blk=0):
    return pl.BlockSpec((rows, width), lambda b: (b, col_blk))


def _gla_constants():
    c = GLA_CHUNK
    t = np.arange(c)[:, None]
    u = np.arange(c)[None, :]
    mats = {0: [], 1: []}
    for lvl in range(GLA_LEVELS):
        b = 1 << lvl
        bound = (t // (2 * b)) * (2 * b) + b - 1
        second = (t // b) % 2 == 1
        mats[0].append((second & (u > bound) & (u <= t)) | (~second & (u > t) & (u <= bound)))
        mats[1].append((~second & (u >= t) & (u <= bound)) | (second & (u > bound) & (u < t)))
    mats[0] += [u <= t, u > t]
    mats[1] += [u >= t, u < t]
    mstack = np.stack([np.concatenate(mats[d], axis=0) for d in (0, 1)]).astype(np.float32)
    x = t ^ u
    msb = np.where(x > 0, np.floor(np.log2(np.maximum(x, 1))), GLA_LEVELS).astype(np.int32)
    lev_f = np.where(t >= u, msb, -1).astype(np.int32)
    lev = np.stack([lev_f, lev_f.T])
    return jnp.asarray(mstack, BF16), jnp.asarray(lev)


def _dot_nt(a, b):
    return lax.dot_general(a, b, (((1,), (1,)), ((), ())), preferred_element_type=F32)


def _gla_kernel(q_ref, k_ref, v_ref, ga_ref, gk_ref, up_ref, bias_ref, ng_ref, m_ref, lev_ref, *rest,
                n_chunks, has_s0, emit_state):
    rest = list(rest)
    s0_ref = rest.pop(0) if has_s0 else None
    y_ref = rest.pop(0)
    st_ref = rest.pop(0) if emit_state else None
    of_ref, ob_ref, s_ref = rest
    c = GLA_CHUNK
    nl = GLA_LEVELS

    def chunk(d, r0, o_ref):
        rows = pl.ds(r0, c)
        gk = gk_ref[rows, :].astype(BF16)
        zg = jnp.dot(gk, up_ref[d].astype(BF16), preferred_element_type=F32) + bias_ref[d]
        la = jax.nn.log_sigmoid(zg) / GLA_GATE_NORM
        hi = la.astype(BF16)
        lo = (la - hi.astype(F32)).astype(BF16)
        mm = m_ref[d]
        dall = jnp.dot(mm, hi, preferred_element_type=F32) + jnp.dot(mm, lo, preferred_element_type=F32)
        q = q_ref[rows, :].astype(F32) * (DK_A ** -0.5)
        k = k_ref[rows, :].astype(F32)
        v = v_ref[rows, :].astype(BF16)
        lev = lev_ref[d]
        sc = jnp.where(lev == nl, _dot_nt(q.astype(BF16), k.astype(BF16)), 0.0)
        for lvl in range(nl):
            e = jnp.exp(jnp.minimum(dall[lvl * c:(lvl + 1) * c], 0.0))
            a = _dot_nt((q * e).astype(BF16), (k * e).astype(BF16))
            sc = sc + jnp.where(lev == lvl, a, 0.0)
        o = jnp.dot(sc.astype(BF16), v, preferred_element_type=F32)
        ek = jnp.exp(jnp.minimum(dall[(nl + 1) * c:(nl + 2) * c], 0.0))
        kv = lax.dot_general((k * ek).astype(BF16), v, (((0,), (0,)), ((), ())),
                             preferred_element_type=F32)
        if has_s0:
            dq = dall[nl * c:(nl + 1) * c]
            eq = jnp.exp(jnp.minimum(dq, 0.0))
            s_in = s_ref[d]
            o = o + jnp.dot((q * eq).astype(BF16), s_in.astype(BF16), preferred_element_type=F32)
            dtot = dq[c - 1:c, :] if d == 0 else dq[0:1, :]
            eye = (lax.broadcasted_iota(jnp.int32, (DK_A, DK_A), 0)
                   == lax.broadcasted_iota(jnp.int32, (DK_A, DK_A), 1))
            dcol = jnp.sum(jnp.where(eye, jnp.broadcast_to(dtot, (DK_A, DK_A)), 0.0), axis=1, keepdims=True)
            s_ref[d] = jnp.exp(jnp.minimum(dcol, 0.0)) * s_in + kv
        else:
            s_ref[d] = kv
        o_ref[rows, :] = o

    if has_s0:
        s_ref[...] = s0_ref[...]

    if n_chunks == 1:
        chunk(0, 0, of_ref)
        chunk(1, 0, ob_ref)
    else:
        def body(i, carry):
            chunk(0, pl.multiple_of(i * c, c), of_ref)
            chunk(1, pl.multiple_of((n_chunks - 1 - i) * c, c), ob_ref)
            return carry

        lax.fori_loop(0, n_chunks, body, 0)

    o = of_ref[...] + ob_ref[...]
    o = o * lax.rsqrt(jnp.mean(o * o, axis=-1, keepdims=True) + EPS) * ng_ref[...]
    ga = ga_ref[...].astype(F32)
    y_ref[...] = (o * (ga * jax.nn.sigmoid(ga))).astype(y_ref.dtype)
    if emit_state:
        st_ref[...] = s_ref[...]


def gla(z_main, z_tail, up_pad, bias, norm_g, mstack, lev, l, *, row0, nb, seq, s0, emit_state):
    rb0 = row0 // seq
    n_chunks = seq // GLA_CHUNK
    has_s0 = s0 is not None
    nm = mstack.shape[1]
    in_specs = [
        pl.BlockSpec((seq, DK_A), lambda b, h: (rb0 + b, Z_QA // DK_A + h)),
        pl.BlockSpec((seq, DK_A), lambda b, h: (rb0 + b, Z_KA // DK_A + h)),
        pl.BlockSpec((seq, DV_A), lambda b, h: (rb0 + b, Z_VA // DV_A + h)),
        pl.BlockSpec((seq, DV_A), lambda b, h: (rb0 + b, Z_GA // DV_A + h)),
        pl.BlockSpec((seq, 128), lambda b, h: (rb0 + b, 2)),
        pl.BlockSpec((None, 2, 128, DK_A), lambda b, h: (l, 0, 0, h)),
        pl.BlockSpec((None, 2, 1, DK_A), lambda b, h: (l, 0, 0, h)),
        pl.BlockSpec((None, 1, DV_A), lambda b, h: (l, 0, 0)),
        pl.BlockSpec((2, nm, GLA_CHUNK), lambda b, h: (0, 0, 0)),
        pl.BlockSpec((2, GLA_CHUNK, GLA_CHUNK), lambda b, h: (0, 0, 0)),
    ]
    args = [z_main, z_main, z_main, z_main, z_tail, up_pad, bias.reshape(DEPTH, 2, 1, H_A * DK_A),
            norm_g.reshape(DEPTH, 1, DV_A), mstack, lev]
    if has_s0:
        in_specs.append(pl.BlockSpec((None, None, 2, None, DK_A, DV_A), lambda b, h: (b, l, 0, h, 0, 0)))
        args.append(s0)
    out_shape = [jax.ShapeDtypeStruct((nb * seq, H_A * DV_A), BF16)]
    out_specs = [pl.BlockSpec((seq, DV_A), lambda b, h: (b, h))]
    if emit_state:
        out_shape.append(jax.ShapeDtypeStruct((nb, 2, H_A, DK_A, DV_A), F32))
        out_specs.append(pl.BlockSpec((None, 2, None, DK_A, DV_A), lambda b, h: (b, 0, h, 0, 0)))
    return pl.pallas_call(
        functools.partial(_gla_kernel, n_chunks=n_chunks, has_s0=has_s0, emit_state=emit_state),
        out_shape=out_shape,
        grid=(nb, H_A),
        in_specs=in_specs,
        out_specs=out_specs,
        scratch_shapes=[pltpu.VMEM((seq, DV_A), F32), pltpu.VMEM((seq, DV_A), F32),
                        pltpu.VMEM((2, DK_A, DV_A), F32)],
        compiler_params=_cparams(2),
        name="gla",
    )(*args)


Z_DTYPE = F32


def kernel(x_prompt, x_sample, c, state_gla, cache_mla_ckv, cache_mla_krope, cache_gqa_k, cache_gqa_v,
           c_ctx, norm1_g, norm2_g, w_ada, b_ada, w_in, gla_gk_up, gla_gk_bias, gla_norm_g,
           mla_qa_norm_g, mla_w_qb, mla_kva_norm_g, mla_w_kvb, mla_q_norm_g, mla_k_norm_g,
           gqa_q_norm_g, gqa_k_norm_g, w_out_a, w_out_b, w_out_c, w_o,
           ffn_w_up, ffn_conv_w, ffn_conv_b, ffn_w_down):
    d = D_MODEL
    x = jnp.concatenate([x_prompt.reshape(T_P, d), x_sample.reshape(T_S, d)], axis=0)
    cvec = jnp.concatenate([c_ctx[None, :], c, jnp.zeros((MOD_ROWS - N_GROUPS, d), F32)], axis=0)
    mods = adaln_all(cvec, w_ada, b_ada)

    w_main = jnp.concatenate([
        w_in[..., _O_QA:_O_GKA], w_in[..., _O_QC:_O_QC + 1024], w_in[..., _O_QLORA:_O_QLORA + 512],
        w_in[..., _O_KC:_O_KC + 256], w_in[..., _O_VC:_O_VC + 256], w_in[..., _O_GATES:]], axis=-1).astype(BF16)
    w_tail = jnp.concatenate([
        w_in[..., _O_KVA:_O_KVA + 320], w_in[..., _O_GKA:_O_GKA + 32],
        jnp.zeros((DEPTH, d, 32), F32)], axis=-1).astype(BF16)
    w_qb_pad = jnp.pad(mla_w_qb.reshape(DEPTH, Q_LORA_B, H_B, QK_B),
                       ((0, 0), (0, 0), (0, 0), (0, PAD_QK_B - QK_B))).reshape(DEPTH, Q_LORA_B, H_B * PAD_QK_B)
    q_g_pad = jnp.pad(mla_q_norm_g, ((0, 0), (0, PAD_QK_B - QK_B))).reshape(DEPTH, 1, PAD_QK_B)
    k_g_pad = jnp.pad(mla_k_norm_g, ((0, 0), (0, PAD_QK_B - QK_B))).reshape(DEPTH, 1, PAD_QK_B)
    up_pad = jnp.zeros((DEPTH, 2, 128, H_A * DK_A), F32)
    for dd in range(2):
        r0 = ROPE_B + dd * GLA_GATE_RANK
        up_pad = up_pad.at[:, dd, r0:r0 + GLA_GATE_RANK, :].set(gla_gk_up[:, dd])
    tab_b = _rope_tables(DEC_SEQ, ROPE_B, 128)
    tab_c = _rope_tables(DEC_SEQ, HD_C, 128)
    mstack, lev = _gla_constants()
    ck = cache_gqa_k.reshape(DEC_BATCH, DEPTH, PAST_LEN, HKV_C * HD_C)
    cv = cache_gqa_v.reshape(DEC_BATCH, DEPTH, PAST_LEN, HKV_C * HD_C)

    st_out, ckv_out, krope_out, kc_out, vc_out = [], [], [], [], []
    for l in range(DEPTH):
        h = norm_mod(x, norm1_g, mods, l, 0, 1)
        z_main = matmul(h, w_main, l, bm=1024, bn=1024, out_dtype=Z_DTYPE, name="w_in_main")
        z_tail = matmul(h, w_tail, l, bm=1024, bn=N_TAIL, out_dtype=F32, name="w_in_tail")

        ya_p, st = gla(z_main, z_tail, up_pad, gla_gk_bias, gla_norm_g, mstack, lev, l,
                       row0=0, nb=BATCH, seq=SEQ, s0=None, emit_state=True)
        ya_s, = gla(z_main, z_tail, up_pad, gla_gk_bias, gla_norm_g, mstack, lev, l,
                    row0=T_P, nb=DEC_BATCH, seq=DEC_SEQ, s0=state_gla, emit_state=False)
        y_a = jnp.concatenate([ya_p, ya_s], axis=0)

        bmk = 512
        qb_p = mla_q_proj(z_main, mla_qa_norm_g, w_qb_pad, q_g_pad, l, row0=0, rows=T_P, tab=None)
        qb_s = mla_q_proj(z_main, mla_qa_norm_g, w_qb_pad, q_g_pad, l, row0=T_P, rows=T_S, tab=tab_b)

        def tail_specs(row0):
            rb = row0 // bmk
            return (pl.BlockSpec((bmk, KV_LORA_B), lambda i: (rb + i, 0)),
                    pl.BlockSpec((bmk, 128), lambda i: (rb + i, 2)))

        c_spec, kr_spec = tail_specs(0)
        kb_p, vb_p, ckv_p = mla_kv_proj(z_tail, c_spec, z_tail, kr_spec, mla_kva_norm_g, mla_w_kvb, k_g_pad, l,
                                        rows=T_P, bm=bmk, norm_in=True, tab=None)
        c_spec, kr_spec = tail_specs(T_P)
        kb_s, vb_s, _ = mla_kv_proj(z_tail, c_spec, z_tail, kr_spec, mla_kva_norm_g, mla_w_kvb, k_g_pad, l,
                                    rows=T_S, bm=bmk, norm_in=True, tab=tab_b)
        kb_c, vb_c = mla_kv_proj(
            cache_mla_ckv, pl.BlockSpec((None, None, PAST_LEN, KV_LORA_B), lambda i: (i, l, 0, 0)),
            cache_mla_krope, pl.BlockSpec((None, None, PAST_LEN, ROPE_B), lambda i: (i, l, 0, 0)),
            mla_kva_norm_g, mla_w_kvb, k_g_pad, l, rows=DEC_BATCH * PAST_LEN, bm=PAST_LEN, norm_in=False, tab=None)
        nkb, nvb = H_B * PAD_QK_B, H_B * V_B
        yb_p = attention(qb_p, [(kb_p, _rows_spec(SEQ, nkb), vb_p, _rows_spec(SEQ, nvb), SEQ)],
                         nb=BATCH, q_rows=SEQ, hq=H_B, hkv=H_B, dqk=PAD_QK_B, dv=V_B, name="mla_attn_p")
        yb_s = attention(qb_s, [(kb_s, _rows_spec(DEC_SEQ, nkb), vb_s, _rows_spec(DEC_SEQ, nvb), DEC_SEQ),
                                (kb_c, _rows_spec(PAST_LEN, nkb), vb_c, _rows_spec(PAST_LEN, nvb), PAST_LEN)],
                         nb=DEC_BATCH, q_rows=DEC_SEQ, hq=H_B, hkv=H_B, dqk=PAD_QK_B, dv=V_B, name="mla_attn_s")
        y_b = jnp.concatenate([yb_p, yb_s], axis=0)

        qc_p, kc_p, vc_p, kn_p = gqa_prep(z_main, gqa_q_norm_g, gqa_k_norm_g, l, row0=0, rows=T_P,
                                          tab=None, emit_kn=True)
        qc_s, kc_s, vc_s = gqa_prep(z_main, gqa_q_norm_g, gqa_k_norm_g, l, row0=T_P, rows=T_S,
                                    tab=tab_c, emit_kn=False)
        nkc = HKV_C * HD_C
        yc_p = attention(qc_p, [(kc_p, _rows_spec(SEQ, nkc), vc_p, _rows_spec(SEQ, nkc), SEQ)],
                         nb=BATCH, q_rows=SEQ, hq=HQ_C, hkv=HKV_C, dqk=HD_C, dv=HD_C, name="gqa_attn_p")
        cache_spec = pl.BlockSpec((None, None, PAST_LEN, nkc), lambda b: (b, l, 0, 0))
        yc_s = attention(qc_s, [(kc_s, _rows_spec(DEC_SEQ, nkc), vc_s, _rows_spec(DEC_SEQ, nkc), DEC_SEQ),
                                (ck, cache_spec, cv, cache_spec, PAST_LEN)],
                         nb=DEC_BATCH, q_rows=DEC_SEQ, hq=HQ_C, hkv=HKV_C, dqk=HD_C, dv=HD_C, name="gqa_attn_s")
        y_c = jnp.concatenate([yc_p, yc_s], axis=0)

        m = gated_merge(y_a, y_b, y_c, z_main, w_out_a, w_out_b, w_out_c, l)
        x = matmul_residual(m, w_o, l, x, mods, 2, bm=1024, bn=1024, name="w_o")
        h2 = norm_mod(x, norm2_g, mods, l, 3, 4)
        act = up_conv(h2, ffn_w_up, ffn_conv_w, ffn_conv_b, l)
        x = matmul_residual(act, ffn_w_down, l, x, mods, 5, bm=512, bn=512, name="ffn_down")

        st_out.append(st)
        ckv_out.append(ckv_p.reshape(BATCH, SEQ, KV_LORA_B))
        krope_out.append(z_tail[:T_P, KV_LORA_B:KV_LORA_B + ROPE_B].astype(F32).reshape(BATCH, SEQ, ROPE_B))
        kc_out.append(kn_p.reshape(BATCH, SEQ, HKV_C, HD_C))
        vc_out.append(z_main[:T_P, Z_VC:Z_VC + nkc].astype(F32).reshape(BATCH, SEQ, HKV_C, HD_C))

    y_p = x[:T_P].reshape(BATCH, SEQ, d)
    y_s = x[T_P:].reshape(DEC_BATCH, DEC_SEQ, d)
    return (y_p, y_s, jnp.stack(st_out, axis=1), jnp.stack(ckv_out, axis=1), jnp.stack(krope_out, axis=1),
            jnp.stack(kc_out, axis=1), jnp.stack(vc_out, axis=1))
```

```python
import functools

import numpy as np
import jax
import jax.numpy as jnp
from jax import lax
from jax.experimental import pallas as pl
from jax.experimental.pallas import tpu as pltpu

F32 = jnp.float32
BF16 = jnp.bfloat16

D_MODEL = 2048
BATCH = 16
SEQ = 256
DEPTH = 4
DEC_BATCH = 2
DEC_SEQ = 1024
PAST_LEN = 512
GRID_W = 64
ROPE_THETA = 10000.0
EPS = 1e-6
H_A, DK_A, DV_A = 4, 128, 256
GLA_GATE_RANK = 16
GLA_GATE_NORM = 16.0
H_B, NOPE_B, ROPE_B, V_B = 8, 128, 64, 128
QK_B = NOPE_B + ROPE_B
Q_LORA_B = 512
KV_LORA_B = 256
HQ_C, HKV_C, HD_C = 8, 2, 128
D_FF = 5632

T_P = BATCH * SEQ
T_S = DEC_BATCH * DEC_SEQ
T_ALL = T_P + T_S
N_GROUPS = 1 + DEC_BATCH
MOD_ROWS = 8

_O_QA, _O_KA, _O_VA, _O_GA = 0, 512, 1024, 2048
_O_GKA = 3072
_O_QLORA = 3104
_O_KVA = 3616
_O_QC, _O_KC, _O_VC = 3936, 4960, 5216
_O_GATES = 5472
N_IN = 11616
ZA_Q, ZA_K, ZA_V, ZA_G = 0, 512, 1024, 2048
ZB_QLORA, ZB_CKV, ZB_KROPE = 0, 512, 768
ZC_Q, ZC_K, ZC_V, ZC_GATES = 0, 1024, 1280, 1536
PAD_QK_B = 256

GLA_CHUNK = 256
GLA_LEVELS = 8

VMEM_LIMIT = 56 * 1024 * 1024


def _cparams(n_axes):
    return pltpu.CompilerParams(
        dimension_semantics=("arbitrary",) * n_axes, vmem_limit_bytes=VMEM_LIMIT)


def _group_of_block(i, bm):
    n_p = T_P // bm
    per = DEC_SEQ // bm
    return jnp.where(i < n_p, 0, 1 + (i - n_p) // per)


def _adaln_kernel(c_ref, w_ref, b_ref, o_ref):
    c = c_ref[...]
    a = (c * jax.nn.sigmoid(c)).astype(BF16)
    o_ref[...] = jnp.dot(a, w_ref[...].astype(BF16), preferred_element_type=F32) + b_ref[...]


def adaln_all(cvec, w_ada, b_ada):
    bn = 1024
    n = 6 * D_MODEL
    return pl.pallas_call(
        _adaln_kernel,
        out_shape=jax.ShapeDtypeStruct((DEPTH, MOD_ROWS, n), F32),
        grid=(DEPTH, n // bn),
        in_specs=[
            pl.BlockSpec((MOD_ROWS, D_MODEL), lambda l, j: (0, 0)),
            pl.BlockSpec((None, D_MODEL, bn), lambda l, j: (l, 0, j)),
            pl.BlockSpec((None, 1, bn), lambda l, j: (l, 0, j)),
        ],
        out_specs=pl.BlockSpec((None, MOD_ROWS, bn), lambda l, j: (l, 0, j)),
        compiler_params=_cparams(2),
        name="adaln",
    )(cvec, w_ada, b_ada.reshape(DEPTH, 1, n))


def _norm_mod_kernel(x_ref, g_ref, sh_ref, sc_ref, o_ref, *, bm):
    grp = _group_of_block(pl.program_id(0), bm)
    x = x_ref[...]
    y = x * lax.rsqrt(jnp.mean(x * x, axis=-1, keepdims=True) + EPS) * g_ref[...]
    sh = sh_ref[pl.ds(grp, 1), :]
    sc = sc_ref[pl.ds(grp, 1), :]
    o_ref[...] = (y * (1.0 + sc) + sh).astype(o_ref.dtype)


def norm_mod(x, g, mods, l, which_shift, which_scale):
    bm = 512
    nd = D_MODEL
    return pl.pallas_call(
        functools.partial(_norm_mod_kernel, bm=bm),
        out_shape=jax.ShapeDtypeStruct((T_ALL, nd), BF16),
        grid=(T_ALL // bm,),
        in_specs=[
            pl.BlockSpec((bm, nd), lambda i: (i, 0)),
            pl.BlockSpec((None, 1, nd), lambda i: (l, 0, 0)),
            pl.BlockSpec((None, MOD_ROWS, nd), lambda i: (l, 0, which_shift)),
            pl.BlockSpec((None, MOD_ROWS, nd), lambda i: (l, 0, which_scale)),
        ],
        out_specs=pl.BlockSpec((bm, nd), lambda i: (i, 0)),
        compiler_params=_cparams(1),
        name="norm_mod",
    )(x, g.reshape(DEPTH, 1, nd), mods, mods)


def _mm_kernel(x_ref, w_ref, o_ref, wb_ref):
    @pl.when(pl.program_id(1) == 0)
    def _():
        wb_ref[...] = w_ref[...].astype(BF16)

    o_ref[...] = jnp.dot(x_ref[...], wb_ref[...], preferred_element_type=F32).astype(o_ref.dtype)


def _mm_shift_kernel(x_ref, wa_ref, wn_ref, o_ref, wb_ref, *, shift):
    @pl.when(pl.program_id(1) == 0)
    def _():
        wb_ref[...] = jnp.concatenate([wa_ref[:, shift:], wn_ref[:, :shift]], axis=1).astype(BF16)

    o_ref[...] = jnp.dot(x_ref[...], wb_ref[...], preferred_element_type=F32).astype(o_ref.dtype)


def matmul(x, w, l, *, bm, bn, col0, n_blk, out_dtype, name):
    m, k = x.shape
    shift = col0 % 128
    a0 = col0 - shift
    assert a0 % bn == 0
    blk0 = a0 // bn
    in_specs = [
        pl.BlockSpec((bm, k), lambda j, i: (i, 0)),
        pl.BlockSpec((None, k, bn), lambda j, i: (l, 0, blk0 + j)),
    ]
    args = [x, w]
    if shift:
        nxt0 = (a0 + bn) // 128
        per = bn // 128
        in_specs.append(pl.BlockSpec((None, k, 128), lambda j, i: (l, 0, nxt0 + j * per)))
        args.append(w)
        body = functools.partial(_mm_shift_kernel, shift=shift)
    else:
        body = _mm_kernel
    return pl.pallas_call(
        body,
        out_shape=jax.ShapeDtypeStruct((m, n_blk * bn), out_dtype),
        grid=(n_blk, m // bm),
        in_specs=in_specs,
        out_specs=pl.BlockSpec((bm, bn), lambda j, i: (i, j)),
        scratch_shapes=[pltpu.VMEM((k, bn), BF16)],
        compiler_params=_cparams(2),
        name=name,
    )(*args)


def _mm_res_kernel(a_ref, w_ref, x_ref, gate_ref, o_ref, wb_ref, *, bm):
    @pl.when(pl.program_id(1) == 0)
    def _():
        wb_ref[...] = w_ref[...].astype(BF16)

    grp = _group_of_block(pl.program_id(1), bm)
    acc = jnp.dot(a_ref[...], wb_ref[...], preferred_element_type=F32)
    o_ref[...] = x_ref[...] + gate_ref[pl.ds(grp, 1), :] * acc


def matmul_residual(a, w, l, x, mods, which_gate, *, bm, bn, name):
    m, k = a.shape
    n = w.shape[-1]
    gate_blk = which_gate * (D_MODEL // bn)
    return pl.pallas_call(
        functools.partial(_mm_res_kernel, bm=bm),
        out_shape=jax.ShapeDtypeStruct((m, n), F32),
        grid=(n // bn, m // bm),
        in_specs=[
            pl.BlockSpec((bm, k), lambda j, i: (i, 0)),
            pl.BlockSpec((None, k, bn), lambda j, i: (l, 0, j)),
            pl.BlockSpec((bm, bn), lambda j, i: (i, j)),
            pl.BlockSpec((None, MOD_ROWS, bn), lambda j, i: (l, 0, gate_blk + j)),
        ],
        out_specs=pl.BlockSpec((bm, bn), lambda j, i: (i, j)),
        scratch_shapes=[pltpu.VMEM((k, bn), BF16)],
        compiler_params=_cparams(2),
        name=name,
    )(a, w, x, mods)


def _merge_kernel(yap_ref, yas_ref, ybp_ref, ybs_ref, ycp_ref, ycs_ref, wa_ref, wb_ref, wc_ref,
                  ga_ref, gb_ref, gc_ref, o_ref, wab_ref, wbb_ref, wcb_ref, *, bm):
    i = pl.program_id(1)

    @pl.when(i == 0)
    def _():
        wab_ref[...] = wa_ref[...].astype(BF16)
        wbb_ref[...] = wb_ref[...].astype(BF16)
        wcb_ref[...] = wc_ref[...].astype(BF16)

    is_ctx = i < T_P // bm

    def branch(yp_ref, ys_ref, w_ref, g_ref):
        y = jnp.where(is_ctx, yp_ref[...], ys_ref[...])
        return jax.nn.sigmoid(g_ref[...].astype(F32)) * jnp.dot(y, w_ref[...], preferred_element_type=F32)

    m = (branch(yap_ref, yas_ref, wab_ref, ga_ref) + branch(ybp_ref, ybs_ref, wbb_ref, gb_ref)
         + branch(ycp_ref, ycs_ref, wcb_ref, gc_ref))
    o_ref[...] = m.astype(o_ref.dtype)


def gated_merge(ys, zc, w_out_a, w_out_b, w_out_c, l):
    bm, bn = 512, 512
    k = 1024
    gate0 = ZC_GATES // bn
    per = D_MODEL // bn
    n_p = T_P // bm
    yp_spec = pl.BlockSpec((bm, k), lambda j, i: (jnp.minimum(i, n_p - 1), 0))
    ys_spec = pl.BlockSpec((bm, k), lambda j, i: (jnp.maximum(i - n_p, 0), 0))
    w_spec = pl.BlockSpec((None, k, bn), lambda j, i: (l, 0, j))

    def g_spec(br):
        return pl.BlockSpec((bm, bn), lambda j, i: (i, gate0 + br * per + j))

    return pl.pallas_call(
        functools.partial(_merge_kernel, bm=bm),
        out_shape=jax.ShapeDtypeStruct((T_ALL, D_MODEL), BF16),
        grid=(D_MODEL // bn, T_ALL // bm),
        in_specs=[yp_spec, ys_spec] * 3 + [w_spec] * 3 + [g_spec(0), g_spec(1), g_spec(2)],
        out_specs=pl.BlockSpec((bm, bn), lambda j, i: (i, j)),
        scratch_shapes=[pltpu.VMEM((k, bn), BF16)] * 3,
        compiler_params=_cparams(2),
        name="gated_merge",
    )(*ys, w_out_a, w_out_b, w_out_c, zc, zc, zc)


def _up_conv_kernel(h_ref, wa_ref, wg_ref, cwa_ref, cwg_ref, cba_ref, cbg_ref, o_ref,
                    wab_ref, wgb_ref, *, bm):
    i = pl.program_id(1)

    @pl.when(i == 0)
    def _():
        wab_ref[...] = wa_ref[...].astype(BF16)
        wgb_ref[...] = wg_ref[...].astype(BF16)

    seq = jnp.where(i < T_P // bm, SEQ, DEC_SEQ)
    pos = lax.broadcasted_iota(jnp.int32, (bm, 1), 0) & (seq - 1)
    first = pos == 0
    last = pos == seq - 1
    h = h_ref[...]

    def conv(w_ref, cw_ref, cb_ref):
        u = jnp.dot(h, w_ref[...], preferred_element_type=F32)
        prev = jnp.where(first, 0.0, pltpu.roll(u, 1, 0))
        nxt = jnp.where(last, 0.0, pltpu.roll(u, bm - 1, 0))
        return prev * cw_ref[0:1, :] + u * cw_ref[1:2, :] + nxt * cw_ref[2:3, :] + cb_ref[...]

    a = conv(wab_ref, cwa_ref, cba_ref)
    g = conv(wgb_ref, cwg_ref, cbg_ref)
    o_ref[...] = (g * jax.nn.sigmoid(g) * a).astype(o_ref.dtype)


def up_conv(h2, w_up, conv_w, conv_b, l):
    bm, bn = 1024, 512
    nb = D_FF // bn
    cb = conv_b.reshape(DEPTH, 1, 2 * D_FF)
    return pl.pallas_call(
        functools.partial(_up_conv_kernel, bm=bm),
        out_shape=jax.ShapeDtypeStruct((T_ALL, D_FF), BF16),
        grid=(nb, T_ALL // bm),
        in_specs=[
            pl.BlockSpec((bm, D_MODEL), lambda j, i: (i, 0)),
            pl.BlockSpec((None, D_MODEL, bn), lambda j, i: (l, 0, j)),
            pl.BlockSpec((None, D_MODEL, bn), lambda j, i: (l, 0, nb + j)),
            pl.BlockSpec((None, 3, bn), lambda j, i: (l, 0, j)),
            pl.BlockSpec((None, 3, bn), lambda j, i: (l, 0, nb + j)),
            pl.BlockSpec((None, 1, bn), lambda j, i: (l, 0, j)),
            pl.BlockSpec((None, 1, bn), lambda j, i: (l, 0, nb + j)),
        ],
        out_specs=pl.BlockSpec((bm, bn), lambda j, i: (i, j)),
        scratch_shapes=[pltpu.VMEM((D_MODEL, bn), BF16)] * 2,
        compiler_params=_cparams(2),
        name="up_conv",
    )(h2, w_up, w_up, conv_w, conv_w, cb, cb)


def _rope_tables(n_tok, rot_dim, width):
    rows = n_tok // GRID_W
    row = jnp.repeat(jnp.arange(rows, dtype=F32), GRID_W)
    col = jnp.tile(jnp.arange(GRID_W, dtype=F32), rows)
    half = rot_dim // 2
    inv = jnp.power(ROPE_THETA, -jnp.arange(half // 2, dtype=F32) * (2.0 / half))
    ang = jnp.concatenate([row[:, None] * inv, col[:, None] * inv], axis=-1)
    cos, sin = jnp.cos(ang), jnp.sin(ang)
    zero = jnp.zeros_like(sin)
    c2 = jnp.stack([cos, cos], axis=-1).reshape(n_tok, rot_dim)
    s_even = jnp.stack([-sin, zero], axis=-1).reshape(n_tok, rot_dim)
    s_odd = jnp.stack([zero, sin], axis=-1).reshape(n_tok, rot_dim)
    pad = ((0, 0), (0, width - rot_dim))
    return jnp.stack([jnp.pad(c2, pad), jnp.pad(s_even, pad), jnp.pad(s_odd, pad)], axis=0)


def _rope(x, tab_ref):
    w = x.shape[-1]
    return (x * tab_ref[0] + pltpu.roll(x, w - 1, 1) * tab_ref[1] + pltpu.roll(x, 1, 1) * tab_ref[2])


def _mla_q_kernel(ql_ref, g1_ref, w_ref, g2_ref, *rest, rope):
    if rope:
        tab_ref, o_ref, wb_ref = rest
    else:
        o_ref, wb_ref = rest

    @pl.when(pl.program_id(0) == 0)
    def _():
        wb_ref[...] = w_ref[...].astype(BF16)

    x = ql_ref[...].astype(F32)
    xn = x * lax.rsqrt(jnp.mean(x * x, axis=-1, keepdims=True) + EPS) * g1_ref[...]
    q = jnp.dot(xn.astype(BF16), wb_ref[...], preferred_element_type=F32)
    scale = QK_B ** -0.5
    for h in range(H_B):
        lo = h * PAD_QK_B
        qn = q[:, lo:lo + NOPE_B]
        qr = q[:, lo + NOPE_B:lo + PAD_QK_B]
        ms = (jnp.sum(qn * qn, axis=-1, keepdims=True) + jnp.sum(qr * qr, axis=-1, keepdims=True)) / QK_B
        r = lax.rsqrt(ms + EPS)
        qn = qn * r * g2_ref[:, 0:NOPE_B]
        qr = qr * r * g2_ref[:, NOPE_B:PAD_QK_B]
        if rope:
            qr = _rope(qr, tab_ref)
        o_ref[:, lo:lo + NOPE_B] = (qn * scale).astype(o_ref.dtype)
        o_ref[:, lo + NOPE_B:lo + PAD_QK_B] = (qr * scale).astype(o_ref.dtype)


def mla_q_proj(zb, qa_g, w_qb_pad, q_g_pad, l, *, row0, rows, tab):
    bm = 512
    rope = tab is not None
    kq = Q_LORA_B
    n = H_B * PAD_QK_B
    rb0 = row0 // bm
    in_specs = [
        pl.BlockSpec((bm, kq), lambda i: (rb0 + i, ZB_QLORA // kq)),
        pl.BlockSpec((None, 1, kq), lambda i: (l, 0, 0)),
        pl.BlockSpec((None, kq, n), lambda i: (l, 0, 0)),
        pl.BlockSpec((None, 1, PAD_QK_B), lambda i: (l, 0, 0)),
    ]
    args = [zb, qa_g.reshape(DEPTH, 1, kq), w_qb_pad, q_g_pad]
    if rope:
        per = DEC_SEQ // bm
        in_specs.append(pl.BlockSpec((3, bm, 128), lambda i: (0, i % per, 0)))
        args.append(tab)
    return pl.pallas_call(
        functools.partial(_mla_q_kernel, rope=rope),
        out_shape=jax.ShapeDtypeStruct((rows, n), BF16),
        grid=(rows // bm,),
        in_specs=in_specs,
        out_specs=pl.BlockSpec((bm, n), lambda i: (i, 0)),
        scratch_shapes=[pltpu.VMEM((kq, n), BF16)],
        compiler_params=_cparams(1),
        name="mla_q_proj",
    )(*args)


def _mla_kv_kernel(c_ref, kr_ref, g1_ref, w_ref, g2_ref, *rest, norm_in, rope):
    rest = list(rest)
    tab_ref = rest.pop(0) if rope else None
    if norm_in:
        k_ref, v_ref, ckv_ref, wb_ref = rest
    else:
        k_ref, v_ref, wb_ref = rest

    @pl.when(pl.program_id(0) == 0)
    def _():
        wb_ref[...] = w_ref[...].astype(BF16)

    c = c_ref[...].astype(F32)
    if norm_in:
        c = c * lax.rsqrt(jnp.mean(c * c, axis=-1, keepdims=True) + EPS) * g1_ref[...]
        ckv_ref[...] = c
    kv = jnp.dot(c.astype(BF16), wb_ref[...], preferred_element_type=F32)
    kr = kr_ref[...].astype(F32)
    if kr.shape[-1] == 128:
        lane = lax.broadcasted_iota(jnp.int32, kr.shape, 1)
        kr = jnp.where(lane < ROPE_B, kr, 0.0)
    else:
        kr = jnp.concatenate([kr, jnp.zeros_like(kr)], axis=-1)
    kr_ss = jnp.sum(kr * kr, axis=-1, keepdims=True)
    for h in range(H_B):
        kn = kv[:, h * 256:h * 256 + NOPE_B]
        v = kv[:, h * 256 + NOPE_B:(h + 1) * 256]
        r = lax.rsqrt((jnp.sum(kn * kn, axis=-1, keepdims=True) + kr_ss) / QK_B + EPS)
        krh = kr * r * g2_ref[:, NOPE_B:PAD_QK_B]
        if rope:
            krh = _rope(krh, tab_ref)
        k_ref[:, h * PAD_QK_B:h * PAD_QK_B + NOPE_B] = (kn * r * g2_ref[:, 0:NOPE_B]).astype(k_ref.dtype)
        k_ref[:, h * PAD_QK_B + NOPE_B:(h + 1) * PAD_QK_B] = krh.astype(k_ref.dtype)
        v_ref[:, h * V_B:(h + 1) * V_B] = v.astype(v_ref.dtype)


def mla_kv_proj(c_arr, c_spec, kr_arr, kr_spec, kva_g, w_kvb, k_g_pad, l, *, rows, bm, norm_in, tab):
    rope = tab is not None
    kc = KV_LORA_B
    n = H_B * (NOPE_B + V_B)
    in_specs = [
        c_spec, kr_spec,
        pl.BlockSpec((None, 1, kc), lambda i: (l, 0, 0)),
        pl.BlockSpec((None, kc, n), lambda i: (l, 0, 0)),
        pl.BlockSpec((None, 1, PAD_QK_B), lambda i: (l, 0, 0)),
    ]
    args = [c_arr, kr_arr, kva_g.reshape(DEPTH, 1, kc), w_kvb, k_g_pad]
    if rope:
        per = DEC_SEQ // bm
        in_specs.append(pl.BlockSpec((3, bm, 128), lambda i: (0, i % per, 0)))
        args.append(tab)
    out_shape = [jax.ShapeDtypeStruct((rows, H_B * PAD_QK_B), BF16),
                 jax.ShapeDtypeStruct((rows, H_B * V_B), BF16)]
    out_specs = [pl.BlockSpec((bm, H_B * PAD_QK_B), lambda i: (i, 0)),
                 pl.BlockSpec((bm, H_B * V_B), lambda i: (i, 0))]
    if norm_in:
        out_shape.append(jax.ShapeDtypeStruct((rows, kc), F32))
        out_specs.append(pl.BlockSpec((bm, kc), lambda i: (i, 0)))
    return pl.pallas_call(
        functools.partial(_mla_kv_kernel, norm_in=norm_in, rope=rope),
        out_shape=out_shape,
        grid=(rows // bm,),
        in_specs=in_specs,
        out_specs=out_specs,
        scratch_shapes=[pltpu.VMEM((kc, n), BF16)],
        compiler_params=_cparams(1),
        name="mla_kv_proj",
    )(*args)


def _gqa_prep_kernel(q_ref, k_ref, v_ref, gq_ref, gk_ref, *rest, rope, emit_kn):
    rest = list(rest)
    tab_ref = rest.pop(0) if rope else None
    qo_ref, ko_ref, vo_ref = rest[:3]
    kn_ref = rest[3] if emit_kn else None
    scale = HD_C ** -0.5

    def norm(x, g_ref):
        return x * lax.rsqrt(jnp.mean(x * x, axis=-1, keepdims=True) + EPS) * g_ref[...]

    for h in range(HQ_C):
        x = norm(q_ref[:, h * HD_C:(h + 1) * HD_C].astype(F32), gq_ref)
        if rope:
            x = _rope(x, tab_ref)
        qo_ref[:, h * HD_C:(h + 1) * HD_C] = (x * scale).astype(qo_ref.dtype)
    for h in range(HKV_C):
        x = norm(k_ref[:, h * HD_C:(h + 1) * HD_C].astype(F32), gk_ref)
        if emit_kn:
            kn_ref[:, h * HD_C:(h + 1) * HD_C] = x
        if rope:
            x = _rope(x, tab_ref)
        ko_ref[:, h * HD_C:(h + 1) * HD_C] = x.astype(ko_ref.dtype)
    vo_ref[...] = v_ref[...].astype(vo_ref.dtype)


def gqa_prep(zc, gq, gk, l, *, row0, rows, tab, emit_kn):
    bm = 512
    rope = tab is not None
    rb0 = row0 // bm
    nq, nk = HQ_C * HD_C, HKV_C * HD_C
    in_specs = [
        pl.BlockSpec((bm, nq), lambda i: (rb0 + i, ZC_Q // nq)),
        pl.BlockSpec((bm, nk), lambda i: (rb0 + i, ZC_K // nk)),
        pl.BlockSpec((bm, nk), lambda i: (rb0 + i, ZC_V // nk)),
        pl.BlockSpec((None, 1, HD_C), lambda i: (l, 0, 0)),
        pl.BlockSpec((None, 1, HD_C), lambda i: (l, 0, 0)),
    ]
    args = [zc, zc, zc, gq.reshape(DEPTH, 1, HD_C), gk.reshape(DEPTH, 1, HD_C)]
    if rope:
        per = DEC_SEQ // bm
        in_specs.append(pl.BlockSpec((3, bm, 128), lambda i: (0, i % per, 0)))
        args.append(tab)
    out_shape = [jax.ShapeDtypeStruct((rows, nq), BF16),
                 jax.ShapeDtypeStruct((rows, nk), BF16),
                 jax.ShapeDtypeStruct((rows, nk), BF16)]
    out_specs = [pl.BlockSpec((bm, nq), lambda i: (i, 0)),
                 pl.BlockSpec((bm, nk), lambda i: (i, 0)),
                 pl.BlockSpec((bm, nk), lambda i: (i, 0))]
    if emit_kn:
        out_shape.append(jax.ShapeDtypeStruct((rows, nk), F32))
        out_specs.append(pl.BlockSpec((bm, nk), lambda i: (i, 0)))
    return pl.pallas_call(
        functools.partial(_gqa_prep_kernel, rope=rope, emit_kn=emit_kn),
        out_shape=out_shape,
        grid=(rows // bm,),
        in_specs=in_specs,
        out_specs=out_specs,
        compiler_params=_cparams(1),
        name="gqa_prep",
    )(*args)


def _attn_kernel(*refs, n_seg, seg_rows, hq, hkv, dqk, dv, tq):
    q_ref = refs[0]
    kv_refs = refs[1:1 + 2 * n_seg]
    o_ref = refs[1 + 2 * n_seg]
    k_sc, v_sc = refs[2 + 2 * n_seg:]
    g = hq // hkv
    nq = q_ref.shape[0] // tq
    for j in range(hkv):
        r = 0
        for s in range(n_seg):
            k_sc[r:r + seg_rows[s], :] = kv_refs[2 * s][:, j * dqk:(j + 1) * dqk].astype(BF16)
            v_sc[r:r + seg_rows[s], :] = kv_refs[2 * s + 1][:, j * dv:(j + 1) * dv].astype(BF16)
            r += seg_rows[s]
        for hh in range(g):
            h = j * g + hh

            def body(qb, carry, h=h):
                r0 = pl.multiple_of(qb * tq, tq)
                q = q_ref[pl.ds(r0, tq), h * dqk:(h + 1) * dqk]
                s_ = lax.dot_general(q, k_sc[...], (((1,), (1,)), ((), ())), preferred_element_type=F32)
                m = jnp.max(s_, axis=-1, keepdims=True)
                p = jnp.exp(s_ - m)
                den = jnp.sum(p, axis=-1, keepdims=True)
                o = jnp.dot(p.astype(BF16), v_sc[...], preferred_element_type=F32) / den
                o_ref[pl.ds(r0, tq), h * dv:(h + 1) * dv] = o.astype(o_ref.dtype)
                return carry

            if nq == 1:
                body(0, 0)
            else:
                lax.fori_loop(0, nq, body, 0)


def attention(q, segs, *, nb, q_rows, hq, hkv, dqk, dv, name):
    tq = 256
    in_specs = [pl.BlockSpec((q_rows, hq * dqk), lambda b: (b, 0))]
    args = [q]
    seg_rows = []
    for k_arr, k_spec, v_arr, v_spec, rows in segs:
        in_specs += [k_spec, v_spec]
        args += [k_arr, v_arr]
        seg_rows.append(rows)
    s_tot = sum(seg_rows)
    return pl.pallas_call(
        functools.partial(_attn_kernel, n_seg=len(segs), seg_rows=tuple(seg_rows), hq=hq, hkv=hkv,
                          dqk=dqk, dv=dv, tq=tq),
        out_shape=jax.ShapeDtypeStruct((nb * q_rows, hq * dv), BF16),
        grid=(nb,),
        in_specs=in_specs,
        out_specs=pl.BlockSpec((q_rows, hq * dv), lambda b: (b, 0)),
        scratch_shapes=[pltpu.VMEM((s_tot, dqk), BF16), pltpu.VMEM((s_tot, dv), BF16)],
        compiler_params=_cparams(1),
        name=name,
    )(*args)


def _rows_spec(rows, width, col_blk=0):
    return pl.BlockSpec((rows, width), lambda b: (b, col_blk))


def _gla_constants():
    c = GLA_CHUNK
    t = np.arange(c)[:, None]
    u = np.arange(c)[None, :]
    tril = (u <= t).astype(np.float32)
    hb = c // 2
    x = (t ^ u)[:hb, :hb]
    msb = np.where(x > 0, np.floor(np.log2(np.maximum(x, 1))), GLA_LEVELS - 1).astype(np.int32)
    return jnp.asarray(tril, BF16), jnp.asarray(msb)


def _dot_nt(a, b):
    return lax.dot_general(a, b, (((1,), (1,)), ((), ())), preferred_element_type=F32)


def _split3(x):
    hi = x.astype(BF16)
    r = x - hi.astype(F32)
    mid = r.astype(BF16)
    lo = (r - mid.astype(F32)).astype(BF16)
    return [hi, mid, lo]


def _exp_neg_abs(x):
    return jnp.exp(-jnp.abs(x))


def _gla_kernel(q_ref, k_ref, v_ref, ga_ref, gk_ref, up_ref, bias_ref, ng_ref, tril_ref, msb_ref, *rest,
                n_chunks, has_s0):
    rest = list(rest)
    s0_ref = rest.pop(0) if has_s0 else None
    y_ref = rest.pop(0)
    st_ref = None if has_s0 else rest.pop(0)
    la_sc, cum_sc, kv_sc, tot_sc, sin_sc = rest
    c = GLA_CHUNK
    hb = c // 2
    nl = GLA_LEVELS
    dk = DK_A
    row = lax.broadcasted_iota(jnp.int32, (c, 1), 0)

    def pass1(ci, r0):
        rows = pl.ds(r0, c)
        gk = gk_ref[rows, :].astype(BF16)
        las = []
        for d in range(2):
            zg = jnp.dot(gk, up_ref[d].astype(BF16), preferred_element_type=F32) + bias_ref[d]
            las.append(jax.nn.log_sigmoid(zg) / GLA_GATE_NORM)
        parts = jnp.concatenate(_split3(las[0]) + _split3(las[1]), axis=1)
        cs = jnp.dot(tril_ref[...], parts, preferred_element_type=F32)
        cums = [cs[:, 0:dk] + cs[:, dk:2 * dk] + cs[:, 2 * dk:3 * dk],
                cs[:, 3 * dk:4 * dk] + cs[:, 4 * dk:5 * dk] + cs[:, 5 * dk:6 * dk]]
        for d in range(2):
            la_sc[d, rows, :] = las[d]
            cum_sc[d, rows, :] = cums[d]
        tot_f = cums[0][c - 1:c, :]
        tot_sc[ci] = jnp.concatenate([tot_f, cums[1][c - 1:c, :]], axis=1)
        k = k_ref[rows, :].astype(F32)
        kd = jnp.concatenate([k * _exp_neg_abs(tot_f - cums[0]), k * _exp_neg_abs(cums[1] - las[1])], axis=1)
        kv_sc[ci] = lax.dot_general(kd.astype(BF16), v_ref[rows, :].astype(BF16), (((0,), (0,)), ((), ())),
                                    preferred_element_type=F32)

    def pass2(ci, r0):
        rows = pl.ds(r0, c)
        laf, lab = la_sc[0, rows, :], la_sc[1, rows, :]
        cumf, cumb = cum_sc[0, rows, :], cum_sc[1, rows, :]
        cumbx = cumb - lab
        q = q_ref[rows, :].astype(F32) * (DK_A ** -0.5)
        k = k_ref[rows, :].astype(F32)
        v = v_ref[rows, :].astype(BF16)
        msb = msb_ref[...]
        q2 = (2.0 * q).astype(BF16)
        kb = k.astype(BF16)
        scd = [_dot_nt(q2[0:hb], kb[0:hb]), _dot_nt(q2[hb:], kb[hb:])]
        a_lo = a_up = None
        for lvl in range(nl):
            b = 1 << lvl
            second = (row & b) != 0
            if lvl == 0:
                qq = q * jnp.exp(jnp.where(second, laf, lab))
                kk = k
            elif lvl == 1:
                p = row & 3
                dq = jnp.where(p == 0, lab + pltpu.roll(lab, c - 1, 0),
                               jnp.where(p == 1, lab, jnp.where(p == 2, laf, laf + pltpu.roll(laf, 1, 0))))
                dkk = jnp.where(p == 0, pltpu.roll(laf, c - 1, 0), jnp.where(p == 3, pltpu.roll(lab, 1, 0), 0.0))
                qq = q * jnp.exp(dq)
                kk = k * jnp.exp(dkk)
            else:
                n = c // (2 * b)
                shp = (n, 2 * b, dk)
                cf3 = cumf.reshape(shp)
                cb3 = cumb.reshape(shp)
                ef = _exp_neg_abs(cf3 - cf3[:, b - 1:b, :]).reshape(c, dk)
                eb = _exp_neg_abs(cumbx.reshape(shp) - cb3[:, b - 1:b, :]).reshape(c, dk)
                qq = q * jnp.where(second, ef, eb)
                kk = k * jnp.where(second, eb, ef)
            qq = qq.astype(BF16)
            kk = kk.astype(BF16)
            if lvl < nl - 1:
                for blk in range(2):
                    a = _dot_nt(qq[blk * hb:(blk + 1) * hb], kk[blk * hb:(blk + 1) * hb])
                    scd[blk] = jnp.where(msb == lvl, a, scd[blk])
            else:
                a_lo = _dot_nt(qq[hb:], kk[:hb])
                a_up = _dot_nt(qq[:hb], kk[hb:])
        sc = jnp.concatenate([jnp.concatenate([scd[0], a_up], axis=1),
                              jnp.concatenate([a_lo, scd[1]], axis=1)], axis=0)
        o = jnp.dot(sc.astype(BF16), v, preferred_element_type=F32)
        if has_s0:
            tot_b = tot_sc[ci][:, dk:]
            qd = jnp.concatenate([q * _exp_neg_abs(cumf), q * _exp_neg_abs(tot_b - cumbx)], axis=1)
            o = o + jnp.dot(qd.astype(BF16), sin_sc[ci].astype(BF16), preferred_element_type=F32)
        o = o * lax.rsqrt(jnp.mean(o * o, axis=-1, keepdims=True) + EPS) * ng_ref[...]
        ga = ga_ref[rows, :].astype(F32)
        y_ref[rows, :] = (o * (ga * jax.nn.sigmoid(ga))).astype(y_ref.dtype)

    if n_chunks == 1:
        pass1(0, 0)
    else:
        lax.fori_loop(0, n_chunks, lambda i, carry: (pass1(i, pl.multiple_of(i * c, c)), carry)[1], 0)

    if has_s0:
        eye = lax.broadcasted_iota(jnp.int32, (2 * dk, 2 * dk), 0) == lax.broadcasted_iota(
            jnp.int32, (2 * dk, 2 * dk), 1)

        def decay_col(ci):
            tot = jnp.broadcast_to(tot_sc[ci], (2 * dk, 2 * dk))
            return jnp.exp(jnp.sum(jnp.where(eye, tot, 0.0), axis=1, keepdims=True))

        s = s0_ref[0]
        for ci in range(n_chunks):
            sin_sc[ci, 0:dk, :] = s
            if ci < n_chunks - 1:
                s = decay_col(ci)[0:dk] * s + kv_sc[ci, 0:dk, :]
        s = s0_ref[1]
        for ci in reversed(range(n_chunks)):
            sin_sc[ci, dk:2 * dk, :] = s
            if ci > 0:
                s = decay_col(ci)[dk:2 * dk] * s + kv_sc[ci, dk:2 * dk, :]
    else:
        st_ref[0] = kv_sc[0, 0:dk, :]
        st_ref[1] = kv_sc[0, dk:2 * dk, :]

    if n_chunks == 1:
        pass2(0, 0)
    else:
        lax.fori_loop(0, n_chunks, lambda i, carry: (pass2(i, pl.multiple_of(i * c, c)), carry)[1], 0)


def gla(za, zg, up_pad, bias, norm_g, tril, msb, l, *, row0, nb, seq, s0):
    rb0 = row0 // seq
    n_chunks = seq // GLA_CHUNK
    has_s0 = s0 is not None
    assert has_s0 or n_chunks == 1
    hb = GLA_CHUNK // 2
    in_specs = [
        pl.BlockSpec((seq, DK_A), lambda b, h: (rb0 + b, ZA_Q // DK_A + h)),
        pl.BlockSpec((seq, DK_A), lambda b, h: (rb0 + b, ZA_K // DK_A + h)),
        pl.BlockSpec((seq, DV_A), lambda b, h: (rb0 + b, ZA_V // DV_A + h)),
        pl.BlockSpec((seq, DV_A), lambda b, h: (rb0 + b, ZA_G // DV_A + h)),
        pl.BlockSpec((seq, 128), lambda b, h: (rb0 + b, 0)),
        pl.BlockSpec((None, 2, 128, DK_A), lambda b, h: (l, 0, 0, h)),
        pl.BlockSpec((None, 2, 1, DK_A), lambda b, h: (l, 0, 0, h)),
        pl.BlockSpec((None, 1, DV_A), lambda b, h: (l, 0, 0)),
        pl.BlockSpec((GLA_CHUNK, GLA_CHUNK), lambda b, h: (0, 0)),
        pl.BlockSpec((hb, hb), lambda b, h: (0, 0)),
    ]
    args = [za, za, za, za, zg, up_pad, bias.reshape(DEPTH, 2, 1, H_A * DK_A),
            norm_g.reshape(DEPTH, 1, DV_A), tril, msb]
    out_shape = [jax.ShapeDtypeStruct((nb * seq, H_A * DV_A), BF16)]
    out_specs = [pl.BlockSpec((seq, DV_A), lambda b, h: (b, h))]
    if has_s0:
        in_specs.append(pl.BlockSpec((None, None, 2, None, DK_A, DV_A), lambda b, h: (b, l, 0, h, 0, 0)))
        args.append(s0)
    else:
        out_shape.append(jax.ShapeDtypeStruct((nb, 2, H_A, DK_A, DV_A), F32))
        out_specs.append(pl.BlockSpec((None, 2, None, DK_A, DV_A), lambda b, h: (b, 0, h, 0, 0)))
    return pl.pallas_call(
        functools.partial(_gla_kernel, n_chunks=n_chunks, has_s0=has_s0),
        out_shape=out_shape,
        grid=(nb, H_A),
        in_specs=in_specs,
        out_specs=out_specs,
        scratch_shapes=[pltpu.VMEM((2, seq, DK_A), F32), pltpu.VMEM((2, seq, DK_A), F32),
                        pltpu.VMEM((n_chunks, 2 * DK_A, DV_A), F32),
                        pltpu.VMEM((n_chunks, 1, 2 * DK_A), F32),
                        pltpu.VMEM((n_chunks, 2 * DK_A, DV_A), F32)],
        compiler_params=_cparams(2),
        name="gla",
    )(*args)


Z_DTYPE = BF16


def kernel(x_prompt, x_sample, c, state_gla, cache_mla_ckv, cache_mla_krope, cache_gqa_k, cache_gqa_v,
           c_ctx, norm1_g, norm2_g, w_ada, b_ada, w_in, gla_gk_up, gla_gk_bias, gla_norm_g,
           mla_qa_norm_g, mla_w_qb, mla_kva_norm_g, mla_w_kvb, mla_q_norm_g, mla_k_norm_g,
           gqa_q_norm_g, gqa_k_norm_g, w_out_a, w_out_b, w_out_c, w_o,
           ffn_w_up, ffn_conv_w, ffn_conv_b, ffn_w_down):
    d = D_MODEL
    x = jnp.concatenate([x_prompt.reshape(T_P, d), x_sample.reshape(T_S, d)], axis=0)
    cvec = jnp.concatenate([c_ctx[None, :], c, jnp.zeros((MOD_ROWS - N_GROUPS, d), F32)], axis=0)
    mods = adaln_all(cvec, w_ada, b_ada)

    w_qb_pad = jnp.pad(mla_w_qb.reshape(DEPTH, Q_LORA_B, H_B, QK_B),
                       ((0, 0), (0, 0), (0, 0), (0, PAD_QK_B - QK_B))).reshape(DEPTH, Q_LORA_B, H_B * PAD_QK_B)
    q_g_pad = jnp.pad(mla_q_norm_g, ((0, 0), (0, PAD_QK_B - QK_B))).reshape(DEPTH, 1, PAD_QK_B)
    k_g_pad = jnp.pad(mla_k_norm_g, ((0, 0), (0, PAD_QK_B - QK_B))).reshape(DEPTH, 1, PAD_QK_B)
    up_pad = jnp.zeros((DEPTH, 2, 128, H_A * DK_A), F32)
    for dd in range(2):
        r0 = dd * GLA_GATE_RANK
        up_pad = up_pad.at[:, dd, r0:r0 + GLA_GATE_RANK, :].set(gla_gk_up[:, dd])
    tab_b = _rope_tables(DEC_SEQ, ROPE_B, 128)
    tab_c = _rope_tables(DEC_SEQ, HD_C, 128)
    tril, msb = _gla_constants()
    ck = cache_gqa_k.reshape(DEC_BATCH, DEPTH, PAST_LEN, HKV_C * HD_C)
    cv = cache_gqa_v.reshape(DEC_BATCH, DEPTH, PAST_LEN, HKV_C * HD_C)

    st_out, ckv_out, krope_out, kc_out, vc_out = [], [], [], [], []
    for l in range(DEPTH):
        h = norm_mod(x, norm1_g, mods, l, 0, 1)
        za = matmul(h, w_in, l, bm=1024, bn=1024, col0=_O_QA, n_blk=3, out_dtype=Z_DTYPE, name="w_in_a")
        zg = matmul(h, w_in, l, bm=1024, bn=128, col0=_O_GKA, n_blk=1, out_dtype=F32, name="w_in_g")
        zb = matmul(h, w_in, l, bm=1024, bn=1024, col0=_O_QLORA, n_blk=1, out_dtype=F32, name="w_in_b")
        zc = matmul(h, w_in, l, bm=1024, bn=1280, col0=_O_QC, n_blk=6, out_dtype=Z_DTYPE, name="w_in_c")

        ya_p, st = gla(za, zg, up_pad, gla_gk_bias, gla_norm_g, tril, msb, l,
                       row0=0, nb=BATCH, seq=SEQ, s0=None)
        ya_s, = gla(za, zg, up_pad, gla_gk_bias, gla_norm_g, tril, msb, l,
                    row0=T_P, nb=DEC_BATCH, seq=DEC_SEQ, s0=state_gla)

        bmk = 512
        qb_p = mla_q_proj(zb, mla_qa_norm_g, w_qb_pad, q_g_pad, l, row0=0, rows=T_P, tab=None)
        qb_s = mla_q_proj(zb, mla_qa_norm_g, w_qb_pad, q_g_pad, l, row0=T_P, rows=T_S, tab=tab_b)

        def tail_specs(row0):
            rb = row0 // bmk
            return (pl.BlockSpec((bmk, KV_LORA_B), lambda i: (rb + i, ZB_CKV // KV_LORA_B)),
                    pl.BlockSpec((bmk, 128), lambda i: (rb + i, ZB_KROPE // 128)))

        c_spec, kr_spec = tail_specs(0)
        kb_p, vb_p, ckv_p = mla_kv_proj(zb, c_spec, zb, kr_spec, mla_kva_norm_g, mla_w_kvb, k_g_pad, l,
                                        rows=T_P, bm=bmk, norm_in=True, tab=None)
        c_spec, kr_spec = tail_specs(T_P)
        kb_s, vb_s, _ = mla_kv_proj(zb, c_spec, zb, kr_spec, mla_kva_norm_g, mla_w_kvb, k_g_pad, l,
                                    rows=T_S, bm=bmk, norm_in=True, tab=tab_b)
        kb_c, vb_c = mla_kv_proj(
            cache_mla_ckv, pl.BlockSpec((None, None, PAST_LEN, KV_LORA_B), lambda i: (i, l, 0, 0)),
            cache_mla_krope, pl.BlockSpec((None, None, PAST_LEN, ROPE_B), lambda i: (i, l, 0, 0)),
            mla_kva_norm_g, mla_w_kvb, k_g_pad, l, rows=DEC_BATCH * PAST_LEN, bm=PAST_LEN, norm_in=False, tab=None)
        nkb, nvb = H_B * PAD_QK_B, H_B * V_B
        yb_p = attention(qb_p, [(kb_p, _rows_spec(SEQ, nkb), vb_p, _rows_spec(SEQ, nvb), SEQ)],
                         nb=BATCH, q_rows=SEQ, hq=H_B, hkv=H_B, dqk=PAD_QK_B, dv=V_B, name="mla_attn_p")
        yb_s = attention(qb_s, [(kb_s, _rows_spec(DEC_SEQ, nkb), vb_s, _rows_spec(DEC_SEQ, nvb), DEC_SEQ),
                                (kb_c, _rows_spec(PAST_LEN, nkb), vb_c, _rows_spec(PAST_LEN, nvb), PAST_LEN)],
                         nb=DEC_BATCH, q_rows=DEC_SEQ, hq=H_B, hkv=H_B, dqk=PAD_QK_B, dv=V_B, name="mla_attn_s")

        qc_p, kc_p, vc_p, kn_p = gqa_prep(zc, gqa_q_norm_g, gqa_k_norm_g, l, row0=0, rows=T_P,
                                          tab=None, emit_kn=True)
        qc_s, kc_s, vc_s = gqa_prep(zc, gqa_q_norm_g, gqa_k_norm_g, l, row0=T_P, rows=T_S,
                                    tab=tab_c, emit_kn=False)
        nkc = HKV_C * HD_C
        yc_p = attention(qc_p, [(kc_p, _rows_spec(SEQ, nkc), vc_p, _rows_spec(SEQ, nkc), SEQ)],
                         nb=BATCH, q_rows=SEQ, hq=HQ_C, hkv=HKV_C, dqk=HD_C, dv=HD_C, name="gqa_attn_p")
        cache_spec = pl.BlockSpec((None, None, PAST_LEN, nkc), lambda b: (b, l, 0, 0))
        yc_s = attention(qc_s, [(kc_s, _rows_spec(DEC_SEQ, nkc), vc_s, _rows_spec(DEC_SEQ, nkc), DEC_SEQ),
                                (ck, cache_spec, cv, cache_spec, PAST_LEN)],
                         nb=DEC_BATCH, q_rows=DEC_SEQ, hq=HQ_C, hkv=HKV_C, dqk=HD_C, dv=HD_C, name="gqa_attn_s")

        m = gated_merge((ya_p, ya_s, yb_p, yb_s, yc_p, yc_s), zc, w_out_a, w_out_b, w_out_c, l)
        x = matmul_residual(m, w_o, l, x, mods, 2, bm=1024, bn=1024, name="w_o")
        h2 = norm_mod(x, norm2_g, mods, l, 3, 4)
        act = up_conv(h2, ffn_w_up, ffn_conv_w, ffn_conv_b, l)
        x = matmul_residual(act, ffn_w_down, l, x, mods, 5, bm=512, bn=512, name="ffn_down")

        st_out.append(st)
        ckv_out.append(ckv_p.reshape(BATCH, SEQ, KV_LORA_B))
        krope_out.append(zb[:T_P, ZB_KROPE:ZB_KROPE + ROPE_B].reshape(BATCH, SEQ, ROPE_B))
        kc_out.append(kn_p.reshape(BATCH, SEQ, HKV_C, HD_C))
        vc_out.append(zc[:T_P, ZC_V:ZC_V + nkc].astype(F32).reshape(BATCH, SEQ, HKV_C, HD_C))

    y_p = x[:T_P].reshape(BATCH, SEQ, d)
    y_s = x[T_P:].reshape(DEC_BATCH, DEC_SEQ, d)
    return (y_p, y_s, jnp.stack(st_out, axis=1), jnp.stack(ckv_out, axis=1), jnp.stack(krope_out, axis=1),
            jnp.stack(kc_out, axis=1), jnp.stack(vc_out, axis=1))
```

```python
import functools

import numpy as np
import jax
import jax.numpy as jnp
from jax import lax
from jax.experimental import pallas as pl
from jax.experimental.pallas import tpu as pltpu

F32 = jnp.float32
BF16 = jnp.bfloat16

D_MODEL = 2048
BATCH = 16
SEQ = 256
DEPTH = 4
DEC_BATCH = 2
DEC_SEQ = 1024
PAST_LEN = 512
GRID_W = 64
ROPE_THETA = 10000.0
EPS = 1e-6
H_A, DK_A, DV_A = 4, 128, 256
GLA_GATE_RANK = 16
GLA_GATE_NORM = 16.0
H_B, NOPE_B, ROPE_B, V_B = 8, 128, 64, 128
QK_B = NOPE_B + ROPE_B
Q_LORA_B = 512
KV_LORA_B = 256
HQ_C, HKV_C, HD_C = 8, 2, 128
D_FF = 5632

T_P = BATCH * SEQ
T_S = DEC_BATCH * DEC_SEQ
T_ALL = T_P + T_S
N_GROUPS = 1 + DEC_BATCH
MOD_ROWS = 8

_O_QA, _O_KA, _O_VA, _O_GA = 0, 512, 1024, 2048
_O_GKA = 3072
_O_QLORA = 3104
_O_KVA = 3616
_O_QC, _O_KC, _O_VC = 3936, 4960, 5216
_O_GATES = 5472
N_IN = 11616
W_IN_BN = 512
W_IN_RANGES = ((_O_QC, 15), (_O_QA, 6), (_O_QLORA, 2), (_O_GKA, 1))
N_Z = W_IN_BN * sum(nb for _, nb in W_IN_RANGES)
Z_QC, Z_KC, Z_VC, Z_GATES = 0, 1024, 1280, 1536
Z_QA, Z_KA, Z_VA, Z_GA = 7680, 8192, 8704, 9728
Z_QLORA, Z_CKV, Z_KROPE = 10752, 11264, 11520
Z_GK = 11776
PAD_QK_B = 256

GLA_CHUNK = 256
GLA_LEVELS = 8

VMEM_LIMIT = 56 * 1024 * 1024


def _cparams(n_axes):
    return pltpu.CompilerParams(
        dimension_semantics=("arbitrary",) * n_axes, vmem_limit_bytes=VMEM_LIMIT)


def _group_of_block(i, bm):
    n_p = T_P // bm
    per = DEC_SEQ // bm
    return jnp.where(i < n_p, 0, 1 + (i - n_p) // per)


def _adaln_kernel(c_ref, w_ref, b_ref, o_ref):
    c = c_ref[...]
    a = (c * jax.nn.sigmoid(c)).astype(BF16)
    o_ref[...] = jnp.dot(a, w_ref[...].astype(BF16), preferred_element_type=F32) + b_ref[...]


def adaln_all(cvec, w_ada, b_ada):
    bn = 1024
    n = 6 * D_MODEL
    return pl.pallas_call(
        _adaln_kernel,
        out_shape=jax.ShapeDtypeStruct((DEPTH, MOD_ROWS, n), F32),
        grid=(DEPTH, n // bn),
        in_specs=[
            pl.BlockSpec((MOD_ROWS, D_MODEL), lambda l, j: (0, 0)),
            pl.BlockSpec((None, D_MODEL, bn), lambda l, j: (l, 0, j)),
            pl.BlockSpec((None, 1, bn), lambda l, j: (l, 0, j)),
        ],
        out_specs=pl.BlockSpec((None, MOD_ROWS, bn), lambda l, j: (l, 0, j)),
        compiler_params=_cparams(2),
        name="adaln",
    )(cvec, w_ada, b_ada.reshape(DEPTH, 1, n))


def _norm_mod_kernel(x_ref, g_ref, sh_ref, sc_ref, o_ref, *, bm):
    grp = _group_of_block(pl.program_id(0), bm)
    x = x_ref[...]
    y = x * lax.rsqrt(jnp.mean(x * x, axis=-1, keepdims=True) + EPS) * g_ref[...]
    sh = sh_ref[pl.ds(grp, 1), :]
    sc = sc_ref[pl.ds(grp, 1), :]
    o_ref[...] = (y * (1.0 + sc) + sh).astype(o_ref.dtype)


def norm_mod(x, g, mods, l, which_shift, which_scale):
    bm = 512
    nd = D_MODEL
    return pl.pallas_call(
        functools.partial(_norm_mod_kernel, bm=bm),
        out_shape=jax.ShapeDtypeStruct((T_ALL, nd), BF16),
        grid=(T_ALL // bm,),
        in_specs=[
            pl.BlockSpec((bm, nd), lambda i: (i, 0)),
            pl.BlockSpec((None, 1, nd), lambda i: (l, 0, 0)),
            pl.BlockSpec((None, MOD_ROWS, nd), lambda i: (l, 0, which_shift)),
            pl.BlockSpec((None, MOD_ROWS, nd), lambda i: (l, 0, which_scale)),
        ],
        out_specs=pl.BlockSpec((bm, nd), lambda i: (i, 0)),
        compiler_params=_cparams(1),
        name="norm_mod",
    )(x, g.reshape(DEPTH, 1, nd), mods, mods)


def _dot_nt(a, b):
    return lax.dot_general(a, b, (((1,), (1,)), ((), ())), preferred_element_type=F32)


ROW_BLK = 1024


def _w_in_kernel(h_ref, wt_ref, o_ref, wb_ref):
    wb_ref[...] = wt_ref[0].astype(BF16)

    def body(r, carry):
        rows = pl.ds(pl.multiple_of(r * ROW_BLK, ROW_BLK), ROW_BLK)
        o_ref[rows, :] = _dot_nt(h_ref[rows, :], wb_ref[...]).astype(o_ref.dtype)
        return carry

    lax.fori_loop(0, T_ALL // ROW_BLK, body, 0)


def _w_in_first_col(j):
    col8 = None
    start = sum(nb for _, nb in W_IN_RANGES)
    for c0, nb in reversed(W_IN_RANGES):
        assert c0 % 8 == 0
        start -= nb
        here = c0 // 8 + (j - start) * (W_IN_BN // 8)
        col8 = here if col8 is None else jnp.where(j < start + nb, here, col8)
    return col8 * 8


def w_in_proj(h, w_in_t, l):
    bn = W_IN_BN
    return pl.pallas_call(
        _w_in_kernel,
        out_shape=jax.ShapeDtypeStruct((T_ALL, N_Z), BF16),
        grid=(N_Z // bn,),
        in_specs=[
            pl.BlockSpec((T_ALL, D_MODEL), lambda j: (0, 0), pipeline_mode=pl.Buffered(1)),
            pl.BlockSpec((pl.Element(1), pl.Element(bn), pl.Element(D_MODEL)),
                         lambda j: (l, _w_in_first_col(j), 0)),
        ],
        out_specs=pl.BlockSpec((T_ALL, bn), lambda j: (0, j)),
        scratch_shapes=[pltpu.VMEM((bn, D_MODEL), BF16)],
        compiler_params=_cparams(1),
        name="w_in",
    )(h, w_in_t)


def _mm_res_kernel(a_ref, w_ref, x_ref, gate_ref, o_ref, wb_ref, *, bm):
    @pl.when(pl.program_id(1) == 0)
    def _():
        wb_ref[...] = w_ref[...].astype(BF16)

    grp = _group_of_block(pl.program_id(1), bm)
    acc = jnp.dot(a_ref[...], wb_ref[...], preferred_element_type=F32)
    o_ref[...] = x_ref[...] + gate_ref[pl.ds(grp, 1), :] * acc


def matmul_residual(a, w, l, x, mods, which_gate, *, bm, bn, name):
    m, k = a.shape
    n = w.shape[-1]
    gate_blk = which_gate * (D_MODEL // bn)
    return pl.pallas_call(
        functools.partial(_mm_res_kernel, bm=bm),
        out_shape=jax.ShapeDtypeStruct((m, n), F32),
        grid=(n // bn, m // bm),
        in_specs=[
            pl.BlockSpec((bm, k), lambda j, i: (i, 0)),
            pl.BlockSpec((None, k, bn), lambda j, i: (l, 0, j)),
            pl.BlockSpec((bm, bn), lambda j, i: (i, j)),
            pl.BlockSpec((None, MOD_ROWS, bn), lambda j, i: (l, 0, gate_blk + j)),
        ],
        out_specs=pl.BlockSpec((bm, bn), lambda j, i: (i, j)),
        scratch_shapes=[pltpu.VMEM((k, bn), BF16)],
        compiler_params=_cparams(2),
        name=name,
    )(a, w, x, mods)


def _merge_kernel(yap_ref, yas_ref, ybp_ref, ybs_ref, ycp_ref, ycs_ref, wa_ref, wb_ref, wc_ref,
                  ga_ref, gb_ref, gc_ref, o_ref, wab_ref, wbb_ref, wcb_ref, *, bm):
    i = pl.program_id(1)

    @pl.when(i == 0)
    def _():
        wab_ref[...] = wa_ref[...].astype(BF16)
        wbb_ref[...] = wb_ref[...].astype(BF16)
        wcb_ref[...] = wc_ref[...].astype(BF16)

    is_ctx = i < T_P // bm

    def branch(yp_ref, ys_ref, w_ref, g_ref):
        y = jnp.where(is_ctx, yp_ref[...], ys_ref[...])
        return jax.nn.sigmoid(g_ref[...].astype(F32)) * jnp.dot(y, w_ref[...], preferred_element_type=F32)

    m = (branch(yap_ref, yas_ref, wab_ref, ga_ref) + branch(ybp_ref, ybs_ref, wbb_ref, gb_ref)
         + branch(ycp_ref, ycs_ref, wcb_ref, gc_ref))
    o_ref[...] = m.astype(o_ref.dtype)


def gated_merge(ys, z, w_out_a, w_out_b, w_out_c, l):
    bm, bn = 512, 512
    k = 1024
    gate0 = Z_GATES // bn
    per = D_MODEL // bn
    n_p = T_P // bm
    yp_spec = pl.BlockSpec((bm, k), lambda j, i: (jnp.minimum(i, n_p - 1), 0))
    ys_spec = pl.BlockSpec((bm, k), lambda j, i: (jnp.maximum(i - n_p, 0), 0))
    w_spec = pl.BlockSpec((None, k, bn), lambda j, i: (l, 0, j))

    def g_spec(br):
        return pl.BlockSpec((bm, bn), lambda j, i: (i, gate0 + br * per + j))

    return pl.pallas_call(
        functools.partial(_merge_kernel, bm=bm),
        out_shape=jax.ShapeDtypeStruct((T_ALL, D_MODEL), BF16),
        grid=(D_MODEL // bn, T_ALL // bm),
        in_specs=[yp_spec, ys_spec] * 3 + [w_spec] * 3 + [g_spec(0), g_spec(1), g_spec(2)],
        out_specs=pl.BlockSpec((bm, bn), lambda j, i: (i, j)),
        scratch_shapes=[pltpu.VMEM((k, bn), BF16)] * 3,
        compiler_params=_cparams(2),
        name="gated_merge",
    )(*ys, w_out_a, w_out_b, w_out_c, z, z, z)


def _up_conv_kernel(h_ref, wa_ref, wg_ref, cwa_ref, cwg_ref, cba_ref, cbg_ref, o_ref, wab_ref, wgb_ref):
    wab_ref[...] = wa_ref[...].astype(BF16)
    wgb_ref[...] = wg_ref[...].astype(BF16)
    bm = ROW_BLK
    row = lax.broadcasted_iota(jnp.int32, (bm, 1), 0)

    for r in range(T_ALL // bm):
        seq = SEQ if r < T_P // bm else DEC_SEQ
        pos = row & (seq - 1)
        first = pos == 0
        last = pos == seq - 1
        h = h_ref[r * bm:(r + 1) * bm, :]

        def conv(w_ref, cw_ref, cb_ref):
            u = jnp.dot(h, w_ref[...], preferred_element_type=F32)
            prev = jnp.where(first, 0.0, pltpu.roll(u, 1, 0))
            nxt = jnp.where(last, 0.0, pltpu.roll(u, bm - 1, 0))
            return prev * cw_ref[0:1, :] + u * cw_ref[1:2, :] + nxt * cw_ref[2:3, :] + cb_ref[...]

        a = conv(wab_ref, cwa_ref, cba_ref)
        g = conv(wgb_ref, cwg_ref, cbg_ref)
        o_ref[r * bm:(r + 1) * bm, :] = (g * jax.nn.sigmoid(g) * a).astype(o_ref.dtype)


def up_conv(h2, w_up, conv_w, conv_b, l):
    bn = 256
    nb = D_FF // bn
    cb = conv_b.reshape(DEPTH, 1, 2 * D_FF)
    return pl.pallas_call(
        _up_conv_kernel,
        out_shape=jax.ShapeDtypeStruct((T_ALL, D_FF), BF16),
        grid=(nb,),
        in_specs=[
            pl.BlockSpec((T_ALL, D_MODEL), lambda j: (0, 0), pipeline_mode=pl.Buffered(1)),
            pl.BlockSpec((None, D_MODEL, bn), lambda j: (l, 0, j)),
            pl.BlockSpec((None, D_MODEL, bn), lambda j: (l, 0, nb + j)),
            pl.BlockSpec((None, 3, bn), lambda j: (l, 0, j)),
            pl.BlockSpec((None, 3, bn), lambda j: (l, 0, nb + j)),
            pl.BlockSpec((None, 1, bn), lambda j: (l, 0, j)),
            pl.BlockSpec((None, 1, bn), lambda j: (l, 0, nb + j)),
        ],
        out_specs=pl.BlockSpec((T_ALL, bn), lambda j: (0, j)),
        scratch_shapes=[pltpu.VMEM((D_MODEL, bn), BF16)] * 2,
        compiler_params=_cparams(1),
        name="up_conv",
    )(h2, w_up, w_up, conv_w, conv_w, cb, cb)


def _rope_tables(n_tok, rot_dim, width):
    rows = n_tok // GRID_W
    row = jnp.repeat(jnp.arange(rows, dtype=F32), GRID_W)
    col = jnp.tile(jnp.arange(GRID_W, dtype=F32), rows)
    half = rot_dim // 2
    inv = jnp.power(ROPE_THETA, -jnp.arange(half // 2, dtype=F32) * (2.0 / half))
    ang = jnp.concatenate([row[:, None] * inv, col[:, None] * inv], axis=-1)
    cos, sin = jnp.cos(ang), jnp.sin(ang)
    zero = jnp.zeros_like(sin)
    c2 = jnp.stack([cos, cos], axis=-1).reshape(n_tok, rot_dim)
    s_even = jnp.stack([-sin, zero], axis=-1).reshape(n_tok, rot_dim)
    s_odd = jnp.stack([zero, sin], axis=-1).reshape(n_tok, rot_dim)
    pad = ((0, 0), (0, width - rot_dim))
    return jnp.stack([jnp.pad(c2, pad), jnp.pad(s_even, pad), jnp.pad(s_odd, pad)], axis=0)


def _rope(x, tab_ref):
    w = x.shape[-1]
    return (x * tab_ref[0] + pltpu.roll(x, w - 1, 1) * tab_ref[1] + pltpu.roll(x, 1, 1) * tab_ref[2])


def _mla_q_kernel(ql_ref, g1_ref, w_ref, g2_ref, *rest, rope):
    if rope:
        tab_ref, o_ref, wb_ref = rest
    else:
        o_ref, wb_ref = rest

    @pl.when(pl.program_id(0) == 0)
    def _():
        wb_ref[...] = w_ref[...].astype(BF16)

    x = ql_ref[...].astype(F32)
    xn = x * lax.rsqrt(jnp.mean(x * x, axis=-1, keepdims=True) + EPS) * g1_ref[...]
    q = jnp.dot(xn.astype(BF16), wb_ref[...], preferred_element_type=F32)
    scale = QK_B ** -0.5
    for h in range(H_B):
        lo = h * PAD_QK_B
        qn = q[:, lo:lo + NOPE_B]
        qr = q[:, lo + NOPE_B:lo + PAD_QK_B]
        ms = (jnp.sum(qn * qn, axis=-1, keepdims=True) + jnp.sum(qr * qr, axis=-1, keepdims=True)) / QK_B
        r = lax.rsqrt(ms + EPS)
        qn = qn * r * g2_ref[:, 0:NOPE_B]
        qr = qr * r * g2_ref[:, NOPE_B:PAD_QK_B]
        if rope:
            qr = _rope(qr, tab_ref)
        o_ref[:, lo:lo + NOPE_B] = (qn * scale).astype(o_ref.dtype)
        o_ref[:, lo + NOPE_B:lo + PAD_QK_B] = (qr * scale).astype(o_ref.dtype)


def mla_q_proj(z, qa_g, w_qb_pad, q_g_pad, l, *, row0, rows, tab):
    bm = 512
    rope = tab is not None
    kq = Q_LORA_B
    n = H_B * PAD_QK_B
    rb0 = row0 // bm
    in_specs = [
        pl.BlockSpec((bm, kq), lambda i: (rb0 + i, Z_QLORA // kq)),
        pl.BlockSpec((None, 1, kq), lambda i: (l, 0, 0)),
        pl.BlockSpec((None, kq, n), lambda i: (l, 0, 0)),
        pl.BlockSpec((None, 1, PAD_QK_B), lambda i: (l, 0, 0)),
    ]
    args = [z, qa_g.reshape(DEPTH, 1, kq), w_qb_pad, q_g_pad]
    if rope:
        per = DEC_SEQ // bm
        in_specs.append(pl.BlockSpec((3, bm, 128), lambda i: (0, i % per, 0)))
        args.append(tab)
    return pl.pallas_call(
        functools.partial(_mla_q_kernel, rope=rope),
        out_shape=jax.ShapeDtypeStruct((rows, n), BF16),
        grid=(rows // bm,),
        in_specs=in_specs,
        out_specs=pl.BlockSpec((bm, n), lambda i: (i, 0)),
        scratch_shapes=[pltpu.VMEM((kq, n), BF16)],
        compiler_params=_cparams(1),
        name="mla_q_proj",
    )(*args)


def _mla_kv_kernel(c_ref, kr_ref, g1_ref, w_ref, g2_ref, *rest, norm_in, rope):
    rest = list(rest)
    tab_ref = rest.pop(0) if rope else None
    if norm_in:
        k_ref, v_ref, ckv_ref, wb_ref = rest
    else:
        k_ref, v_ref, wb_ref = rest

    @pl.when(pl.program_id(0) == 0)
    def _():
        wb_ref[...] = w_ref[...].astype(BF16)

    c = c_ref[...].astype(F32)
    if norm_in:
        c = c * lax.rsqrt(jnp.mean(c * c, axis=-1, keepdims=True) + EPS) * g1_ref[...]
        ckv_ref[...] = c
    kv = jnp.dot(c.astype(BF16), wb_ref[...], preferred_element_type=F32)
    kr = kr_ref[...].astype(F32)
    if kr.shape[-1] == 128:
        lane = lax.broadcasted_iota(jnp.int32, kr.shape, 1)
        kr = jnp.where(lane < ROPE_B, kr, 0.0)
    else:
        kr = jnp.concatenate([kr, jnp.zeros_like(kr)], axis=-1)
    kr_ss = jnp.sum(kr * kr, axis=-1, keepdims=True)
    for h in range(H_B):
        kn = kv[:, h * 256:h * 256 + NOPE_B]
        v = kv[:, h * 256 + NOPE_B:(h + 1) * 256]
        r = lax.rsqrt((jnp.sum(kn * kn, axis=-1, keepdims=True) + kr_ss) / QK_B + EPS)
        krh = kr * r * g2_ref[:, NOPE_B:PAD_QK_B]
        if rope:
            krh = _rope(krh, tab_ref)
        k_ref[:, h * PAD_QK_B:h * PAD_QK_B + NOPE_B] = (kn * r * g2_ref[:, 0:NOPE_B]).astype(k_ref.dtype)
        k_ref[:, h * PAD_QK_B + NOPE_B:(h + 1) * PAD_QK_B] = krh.astype(k_ref.dtype)
        v_ref[:, h * V_B:(h + 1) * V_B] = v.astype(v_ref.dtype)


def mla_kv_proj(c_arr, c_spec, kr_arr, kr_spec, kva_g, w_kvb, k_g_pad, l, *, rows, bm, norm_in, tab):
    rope = tab is not None
    kc = KV_LORA_B
    n = H_B * (NOPE_B + V_B)
    in_specs = [
        c_spec, kr_spec,
        pl.BlockSpec((None, 1, kc), lambda i: (l, 0, 0)),
        pl.BlockSpec((None, kc, n), lambda i: (l, 0, 0)),
        pl.BlockSpec((None, 1, PAD_QK_B), lambda i: (l, 0, 0)),
    ]
    args = [c_arr, kr_arr, kva_g.reshape(DEPTH, 1, kc), w_kvb, k_g_pad]
    if rope:
        per = DEC_SEQ // bm
        in_specs.append(pl.BlockSpec((3, bm, 128), lambda i: (0, i % per, 0)))
        args.append(tab)
    out_shape = [jax.ShapeDtypeStruct((rows, H_B * PAD_QK_B), BF16),
                 jax.ShapeDtypeStruct((rows, H_B * V_B), BF16)]
    out_specs = [pl.BlockSpec((bm, H_B * PAD_QK_B), lambda i: (i, 0)),
                 pl.BlockSpec((bm, H_B * V_B), lambda i: (i, 0))]
    if norm_in:
        out_shape.append(jax.ShapeDtypeStruct((rows, kc), F32))
        out_specs.append(pl.BlockSpec((bm, kc), lambda i: (i, 0)))
    return pl.pallas_call(
        functools.partial(_mla_kv_kernel, norm_in=norm_in, rope=rope),
        out_shape=out_shape,
        grid=(rows // bm,),
        in_specs=in_specs,
        out_specs=out_specs,
        scratch_shapes=[pltpu.VMEM((kc, n), BF16)],
        compiler_params=_cparams(1),
        name="mla_kv_proj",
    )(*args)


def _gqa_prep_kernel(q_ref, k_ref, v_ref, gq_ref, gk_ref, *rest, rope, emit_kn):
    rest = list(rest)
    tab_ref = rest.pop(0) if rope else None
    qo_ref, ko_ref, vo_ref = rest[:3]
    kn_ref = rest[3] if emit_kn else None
    scale = HD_C ** -0.5

    def norm(x, g_ref):
        return x * lax.rsqrt(jnp.mean(x * x, axis=-1, keepdims=True) + EPS) * g_ref[...]

    for h in range(HQ_C):
        x = norm(q_ref[:, h * HD_C:(h + 1) * HD_C].astype(F32), gq_ref)
        if rope:
            x = _rope(x, tab_ref)
        qo_ref[:, h * HD_C:(h + 1) * HD_C] = (x * scale).astype(qo_ref.dtype)
    for h in range(HKV_C):
        x = norm(k_ref[:, h * HD_C:(h + 1) * HD_C].astype(F32), gk_ref)
        if emit_kn:
            kn_ref[:, h * HD_C:(h + 1) * HD_C] = x
        if rope:
            x = _rope(x, tab_ref)
        ko_ref[:, h * HD_C:(h + 1) * HD_C] = x.astype(ko_ref.dtype)
    vo_ref[...] = v_ref[...].astype(vo_ref.dtype)


def gqa_prep(z, gq, gk, l, *, row0, rows, tab, emit_kn):
    bm = 512
    rope = tab is not None
    rb0 = row0 // bm
    nq, nk = HQ_C * HD_C, HKV_C * HD_C
    in_specs = [
        pl.BlockSpec((bm, nq), lambda i: (rb0 + i, Z_QC // nq)),
        pl.BlockSpec((bm, nk), lambda i: (rb0 + i, Z_KC // nk)),
        pl.BlockSpec((bm, nk), lambda i: (rb0 + i, Z_VC // nk)),
        pl.BlockSpec((None, 1, HD_C), lambda i: (l, 0, 0)),
        pl.BlockSpec((None, 1, HD_C), lambda i: (l, 0, 0)),
    ]
    args = [z, z, z, gq.reshape(DEPTH, 1, HD_C), gk.reshape(DEPTH, 1, HD_C)]
    if rope:
        per = DEC_SEQ // bm
        in_specs.append(pl.BlockSpec((3, bm, 128), lambda i: (0, i % per, 0)))
        args.append(tab)
    out_shape = [jax.ShapeDtypeStruct((rows, nq), BF16),
                 jax.ShapeDtypeStruct((rows, nk), BF16),
                 jax.ShapeDtypeStruct((rows, nk), BF16)]
    out_specs = [pl.BlockSpec((bm, nq), lambda i: (i, 0)),
                 pl.BlockSpec((bm, nk), lambda i: (i, 0)),
                 pl.BlockSpec((bm, nk), lambda i: (i, 0))]
    if emit_kn:
        out_shape.append(jax.ShapeDtypeStruct((rows, nk), F32))
        out_specs.append(pl.BlockSpec((bm, nk), lambda i: (i, 0)))
    return pl.pallas_call(
        functools.partial(_gqa_prep_kernel, rope=rope, emit_kn=emit_kn),
        out_shape=out_shape,
        grid=(rows // bm,),
        in_specs=in_specs,
        out_specs=out_specs,
        compiler_params=_cparams(1),
        name="gqa_prep",
    )(*args)


def _attn_kernel(*refs, n_seg, seg_rows, hq, hkv, dqk, dv, tq):
    q_ref = refs[0]
    kv_refs = refs[1:1 + 2 * n_seg]
    o_ref = refs[1 + 2 * n_seg]
    k_sc, v_sc = refs[2 + 2 * n_seg:]
    g = hq // hkv
    nq = q_ref.shape[0] // tq
    for j in range(hkv):
        r = 0
        for s in range(n_seg):
            k_sc[r:r + seg_rows[s], :] = kv_refs[2 * s][:, j * dqk:(j + 1) * dqk].astype(BF16)
            v_sc[r:r + seg_rows[s], :] = kv_refs[2 * s + 1][:, j * dv:(j + 1) * dv].astype(BF16)
            r += seg_rows[s]
        for hh in range(g):
            h = j * g + hh

            def body(qb, carry, h=h):
                r0 = pl.multiple_of(qb * tq, tq)
                q = q_ref[pl.ds(r0, tq), h * dqk:(h + 1) * dqk]
                s_ = lax.dot_general(q, k_sc[...], (((1,), (1,)), ((), ())), preferred_element_type=F32)
                m = jnp.max(s_, axis=-1, keepdims=True)
                p = jnp.exp(s_ - m)
                den = jnp.sum(p, axis=-1, keepdims=True)
                o = jnp.dot(p.astype(BF16), v_sc[...], preferred_element_type=F32) / den
                o_ref[pl.ds(r0, tq), h * dv:(h + 1) * dv] = o.astype(o_ref.dtype)
                return carry

            if nq == 1:
                body(0, 0)
            else:
                lax.fori_loop(0, nq, body, 0)


def attention(q, segs, *, nb, q_rows, hq, hkv, dqk, dv, name):
    tq = 256
    in_specs = [pl.BlockSpec((q_rows, hq * dqk), lambda b: (b, 0))]
    args = [q]
    seg_rows = []
    for k_arr, k_spec, v_arr, v_spec, rows in segs:
        in_specs += [k_spec, v_spec]
        args += [k_arr, v_arr]
        seg_rows.append(rows)
    s_tot = sum(seg_rows)
    return pl.pallas_call(
        functools.partial(_attn_kernel, n_seg=len(segs), seg_rows=tuple(seg_rows), hq=hq, hkv=hkv,
                          dqk=dqk, dv=dv, tq=tq),
        out_shape=jax.ShapeDtypeStruct((nb * q_rows, hq * dv), BF16),
        grid=(nb,),
        in_specs=in_specs,
        out_specs=pl.BlockSpec((q_rows, hq * dv), lambda b: (b, 0)),
        scratch_shapes=[pltpu.VMEM((s_tot, dqk), BF16), pltpu.VMEM((s_tot, dv), BF16)],
        compiler_params=_cparams(1),
        name=name,
    )(*args)


def _rows_spec(rows, width, col_blk=0):
    return pl.BlockSpec((rows, width), lambda b: (b, col_blk))


def _gla_constants():
    c = GLA_CHUNK
    t = np.arange(c)[:, None]
    u = np.arange(c)[None, :]
    tril = (u <= t).astype(np.float32)
    hb = c // 2
    x = (t ^ u)[:hb, :hb]
    msb = np.where(x > 0, np.floor(np.log2(np.maximum(x, 1))), GLA_LEVELS - 1).astype(np.int32)
    return jnp.asarray(tril, BF16), jnp.asarray(msb)


def _split3(x):
    hi = x.astype(BF16)
    r = x - hi.astype(F32)
    mid = r.astype(BF16)
    lo = (r - mid.astype(F32)).astype(BF16)
    return [hi, mid, lo]


def _exp_neg_abs(x):
    return jnp.exp(-jnp.abs(x))


def _gla_kernel(q_ref, k_ref, v_ref, ga_ref, gk_ref, up_ref, bias_ref, ng_ref, tril_ref, msb_ref, *rest,
                n_chunks, has_s0):
    rest = list(rest)
    s0_ref = rest.pop(0) if has_s0 else None
    y_ref = rest.pop(0)
    st_ref = None if has_s0 else rest.pop(0)
    la_sc, cum_sc, kv_sc, tot_sc, sin_sc = rest
    c = GLA_CHUNK
    hb = c // 2
    nl = GLA_LEVELS
    dk = DK_A
    row = lax.broadcasted_iota(jnp.int32, (c, 1), 0)

    def pass1(ci, r0):
        rows = pl.ds(r0, c)
        gk = gk_ref[rows, :].astype(BF16)
        las = []
        for d in range(2):
            zg = jnp.dot(gk, up_ref[d].astype(BF16), preferred_element_type=F32) + bias_ref[d]
            las.append(jax.nn.log_sigmoid(zg) / GLA_GATE_NORM)
        parts = jnp.concatenate(_split3(las[0]) + _split3(las[1]), axis=1)
        cs = jnp.dot(tril_ref[...], parts, preferred_element_type=F32)
        cums = [cs[:, 0:dk] + cs[:, dk:2 * dk] + cs[:, 2 * dk:3 * dk],
                cs[:, 3 * dk:4 * dk] + cs[:, 4 * dk:5 * dk] + cs[:, 5 * dk:6 * dk]]
        for d in range(2):
            la_sc[d, rows, :] = las[d]
            cum_sc[d, rows, :] = cums[d]
        tot_f = cums[0][c - 1:c, :]
        tot_sc[ci] = jnp.concatenate([tot_f, cums[1][c - 1:c, :]], axis=1)
        k = k_ref[rows, :].astype(F32)
        kd = jnp.concatenate([k * _exp_neg_abs(tot_f - cums[0]), k * _exp_neg_abs(cums[1] - las[1])], axis=1)
        kv_sc[ci] = lax.dot_general(kd.astype(BF16), v_ref[rows, :].astype(BF16), (((0,), (0,)), ((), ())),
                                    preferred_element_type=F32)

    def pass2(ci, r0):
        rows = pl.ds(r0, c)
        laf, lab = la_sc[0, rows, :], la_sc[1, rows, :]
        cumf, cumb = cum_sc[0, rows, :], cum_sc[1, rows, :]
        cumbx = cumb - lab
        q = q_ref[rows, :].astype(F32) * (DK_A ** -0.5)
        k = k_ref[rows, :].astype(F32)
        v = v_ref[rows, :].astype(BF16)
        msb = msb_ref[...]
        q2 = (2.0 * q).astype(BF16)
        kb = k.astype(BF16)
        scd = [_dot_nt(q2[0:hb], kb[0:hb]), _dot_nt(q2[hb:], kb[hb:])]
        a_lo = a_up = None
        for lvl in range(nl):
            b = 1 << lvl
            second = (row & b) != 0
            if lvl == 0:
                qq = q * jnp.exp(jnp.where(second, laf, lab))
                kk = k
            elif lvl == 1:
                p = row & 3
                dq = jnp.where(p == 0, lab + pltpu.roll(lab, c - 1, 0),
                               jnp.where(p == 1, lab, jnp.where(p == 2, laf, laf + pltpu.roll(laf, 1, 0))))
                dkk = jnp.where(p == 0, pltpu.roll(laf, c - 1, 0), jnp.where(p == 3, pltpu.roll(lab, 1, 0), 0.0))
                qq = q * jnp.exp(dq)
                kk = k * jnp.exp(dkk)
            else:
                n = c // (2 * b)
                shp = (n, 2 * b, dk)
                cf3 = cumf.reshape(shp)
                cb3 = cumb.reshape(shp)
                ef = _exp_neg_abs(cf3 - cf3[:, b - 1:b, :]).reshape(c, dk)
                eb = _exp_neg_abs(cumbx.reshape(shp) - cb3[:, b - 1:b, :]).reshape(c, dk)
                qq = q * jnp.where(second, ef, eb)
                kk = k * jnp.where(second, eb, ef)
            qq = qq.astype(BF16)
            kk = kk.astype(BF16)
            if lvl < nl - 1:
                for blk in range(2):
                    a = _dot_nt(qq[blk * hb:(blk + 1) * hb], kk[blk * hb:(blk + 1) * hb])
                    scd[blk] = jnp.where(msb == lvl, a, scd[blk])
            else:
                a_lo = _dot_nt(qq[hb:], kk[:hb])
                a_up = _dot_nt(qq[:hb], kk[hb:])
        sc = jnp.concatenate([jnp.concatenate([scd[0], a_up], axis=1),
                              jnp.concatenate([a_lo, scd[1]], axis=1)], axis=0)
        o = jnp.dot(sc.astype(BF16), v, preferred_element_type=F32)
        if has_s0:
            tot_b = tot_sc[ci][:, dk:]
            qd = jnp.concatenate([q * _exp_neg_abs(cumf), q * _exp_neg_abs(tot_b - cumbx)], axis=1)
            o = o + jnp.dot(qd.astype(BF16), sin_sc[ci].astype(BF16), preferred_element_type=F32)
        o = o * lax.rsqrt(jnp.mean(o * o, axis=-1, keepdims=True) + EPS) * ng_ref[...]
        ga = ga_ref[rows, :].astype(F32)
        y_ref[rows, :] = (o * (ga * jax.nn.sigmoid(ga))).astype(y_ref.dtype)

    if n_chunks == 1:
        pass1(0, 0)
    else:
        lax.fori_loop(0, n_chunks, lambda i, carry: (pass1(i, pl.multiple_of(i * c, c)), carry)[1], 0)

    if has_s0:
        eye = lax.broadcasted_iota(jnp.int32, (2 * dk, 2 * dk), 0) == lax.broadcasted_iota(
            jnp.int32, (2 * dk, 2 * dk), 1)

        def decay_col(ci):
            tot = jnp.broadcast_to(tot_sc[ci], (2 * dk, 2 * dk))
            return jnp.exp(jnp.sum(jnp.where(eye, tot, 0.0), axis=1, keepdims=True))

        s = s0_ref[0]
        for ci in range(n_chunks):
            sin_sc[ci, 0:dk, :] = s
            if ci < n_chunks - 1:
                s = decay_col(ci)[0:dk] * s + kv_sc[ci, 0:dk, :]
        s = s0_ref[1]
        for ci in reversed(range(n_chunks)):
            sin_sc[ci, dk:2 * dk, :] = s
            if ci > 0:
                s = decay_col(ci)[dk:2 * dk] * s + kv_sc[ci, dk:2 * dk, :]
    else:
        st_ref[0] = kv_sc[0, 0:dk, :]
        st_ref[1] = kv_sc[0, dk:2 * dk, :]

    if n_chunks == 1:
        pass2(0, 0)
    else:
        lax.fori_loop(0, n_chunks, lambda i, carry: (pass2(i, pl.multiple_of(i * c, c)), carry)[1], 0)


def gla(z, up_pad, bias, norm_g, tril, msb, l, *, row0, nb, seq, s0):
    rb0 = row0 // seq
    n_chunks = seq // GLA_CHUNK
    has_s0 = s0 is not None
    assert has_s0 or n_chunks == 1
    hb = GLA_CHUNK // 2
    in_specs = [
        pl.BlockSpec((seq, DK_A), lambda b, h: (rb0 + b, Z_QA // DK_A + h)),
        pl.BlockSpec((seq, DK_A), lambda b, h: (rb0 + b, Z_KA // DK_A + h)),
        pl.BlockSpec((seq, DV_A), lambda b, h: (rb0 + b, Z_VA // DV_A + h)),
        pl.BlockSpec((seq, DV_A), lambda b, h: (rb0 + b, Z_GA // DV_A + h)),
        pl.BlockSpec((seq, 128), lambda b, h: (rb0 + b, Z_GK // 128)),
        pl.BlockSpec((None, 2, 128, DK_A), lambda b, h: (l, 0, 0, h)),
        pl.BlockSpec((None, 2, 1, DK_A), lambda b, h: (l, 0, 0, h)),
        pl.BlockSpec((None, 1, DV_A), lambda b, h: (l, 0, 0)),
        pl.BlockSpec((GLA_CHUNK, GLA_CHUNK), lambda b, h: (0, 0)),
        pl.BlockSpec((hb, hb), lambda b, h: (0, 0)),
    ]
    args = [z, z, z, z, z, up_pad, bias.reshape(DEPTH, 2, 1, H_A * DK_A),
            norm_g.reshape(DEPTH, 1, DV_A), tril, msb]
    out_shape = [jax.ShapeDtypeStruct((nb * seq, H_A * DV_A), BF16)]
    out_specs = [pl.BlockSpec((seq, DV_A), lambda b, h: (b, h))]
    if has_s0:
        in_specs.append(pl.BlockSpec((None, None, 2, None, DK_A, DV_A), lambda b, h: (b, l, 0, h, 0, 0)))
        args.append(s0)
    else:
        out_shape.append(jax.ShapeDtypeStruct((nb, 2, H_A, DK_A, DV_A), F32))
        out_specs.append(pl.BlockSpec((None, 2, None, DK_A, DV_A), lambda b, h: (b, 0, h, 0, 0)))
    return pl.pallas_call(
        functools.partial(_gla_kernel, n_chunks=n_chunks, has_s0=has_s0),
        out_shape=out_shape,
        grid=(nb, H_A),
        in_specs=in_specs,
        out_specs=out_specs,
        scratch_shapes=[pltpu.VMEM((2, seq, DK_A), F32), pltpu.VMEM((2, seq, DK_A), F32),
                        pltpu.VMEM((n_chunks, 2 * DK_A, DV_A), F32),
                        pltpu.VMEM((n_chunks, 1, 2 * DK_A), F32),
                        pltpu.VMEM((n_chunks, 2 * DK_A, DV_A), F32)],
        compiler_params=_cparams(2),
        name="gla",
    )(*args)


def kernel(x_prompt, x_sample, c, state_gla, cache_mla_ckv, cache_mla_krope, cache_gqa_k, cache_gqa_v,
           c_ctx, norm1_g, norm2_g, w_ada, b_ada, w_in, gla_gk_up, gla_gk_bias, gla_norm_g,
           mla_qa_norm_g, mla_w_qb, mla_kva_norm_g, mla_w_kvb, mla_q_norm_g, mla_k_norm_g,
           gqa_q_norm_g, gqa_k_norm_g, w_out_a, w_out_b, w_out_c, w_o,
           ffn_w_up, ffn_conv_w, ffn_conv_b, ffn_w_down):
    d = D_MODEL
    x = jnp.concatenate([x_prompt.reshape(T_P, d), x_sample.reshape(T_S, d)], axis=0)
    cvec = jnp.concatenate([c_ctx[None, :], c, jnp.zeros((MOD_ROWS - N_GROUPS, d), F32)], axis=0)
    mods = adaln_all(cvec, w_ada, b_ada)

    w_qb_pad = jnp.pad(mla_w_qb.reshape(DEPTH, Q_LORA_B, H_B, QK_B),
                       ((0, 0), (0, 0), (0, 0), (0, PAD_QK_B - QK_B))).reshape(DEPTH, Q_LORA_B, H_B * PAD_QK_B)
    q_g_pad = jnp.pad(mla_q_norm_g, ((0, 0), (0, PAD_QK_B - QK_B))).reshape(DEPTH, 1, PAD_QK_B)
    k_g_pad = jnp.pad(mla_k_norm_g, ((0, 0), (0, PAD_QK_B - QK_B))).reshape(DEPTH, 1, PAD_QK_B)
    up_pad = jnp.zeros((DEPTH, 2, 128, H_A * DK_A), F32)
    for dd in range(2):
        r0 = dd * GLA_GATE_RANK
        up_pad = up_pad.at[:, dd, r0:r0 + GLA_GATE_RANK, :].set(gla_gk_up[:, dd])
    tab_b = _rope_tables(DEC_SEQ, ROPE_B, 128)
    tab_c = _rope_tables(DEC_SEQ, HD_C, 128)
    tril, msb = _gla_constants()
    w_in_t = jnp.swapaxes(w_in, 1, 2)
    ck = cache_gqa_k.reshape(DEC_BATCH, DEPTH, PAST_LEN, HKV_C * HD_C)
    cv = cache_gqa_v.reshape(DEC_BATCH, DEPTH, PAST_LEN, HKV_C * HD_C)

    st_out, ckv_out, krope_out, kc_out, vc_out = [], [], [], [], []
    for l in range(DEPTH):
        h = norm_mod(x, norm1_g, mods, l, 0, 1)
        z = w_in_proj(h, w_in_t, l)

        ya_p, st = gla(z, up_pad, gla_gk_bias, gla_norm_g, tril, msb, l,
                       row0=0, nb=BATCH, seq=SEQ, s0=None)
        ya_s, = gla(z, up_pad, gla_gk_bias, gla_norm_g, tril, msb, l,
                    row0=T_P, nb=DEC_BATCH, seq=DEC_SEQ, s0=state_gla)

        bmk = 512
        qb_p = mla_q_proj(z, mla_qa_norm_g, w_qb_pad, q_g_pad, l, row0=0, rows=T_P, tab=None)
        qb_s = mla_q_proj(z, mla_qa_norm_g, w_qb_pad, q_g_pad, l, row0=T_P, rows=T_S, tab=tab_b)

        def tail_specs(row0):
            rb = row0 // bmk
            return (pl.BlockSpec((bmk, KV_LORA_B), lambda i: (rb + i, Z_CKV // KV_LORA_B)),
                    pl.BlockSpec((bmk, 128), lambda i: (rb + i, Z_KROPE // 128)))

        c_spec, kr_spec = tail_specs(0)
        kb_p, vb_p, ckv_p = mla_kv_proj(z, c_spec, z, kr_spec, mla_kva_norm_g, mla_w_kvb, k_g_pad, l,
                                        rows=T_P, bm=bmk, norm_in=True, tab=None)
        c_spec, kr_spec = tail_specs(T_P)
        kb_s, vb_s, _ = mla_kv_proj(z, c_spec, z, kr_spec, mla_kva_norm_g, mla_w_kvb, k_g_pad, l,
                                    rows=T_S, bm=bmk, norm_in=True, tab=tab_b)
        kb_c, vb_c = mla_kv_proj(
            cache_mla_ckv, pl.BlockSpec((None, None, PAST_LEN, KV_LORA_B), lambda i: (i, l, 0, 0)),
            cache_mla_krope, pl.BlockSpec((None, None, PAST_LEN, ROPE_B), lambda i: (i, l, 0, 0)),
            mla_kva_norm_g, mla_w_kvb, k_g_pad, l, rows=DEC_BATCH * PAST_LEN, bm=PAST_LEN, norm_in=False, tab=None)
        nkb, nvb = H_B * PAD_QK_B, H_B * V_B
        yb_p = attention(qb_p, [(kb_p, _rows_spec(SEQ, nkb), vb_p, _rows_spec(SEQ, nvb), SEQ)],
                         nb=BATCH, q_rows=SEQ, hq=H_B, hkv=H_B, dqk=PAD_QK_B, dv=V_B, name="mla_attn_p")
        yb_s = attention(qb_s, [(kb_s, _rows_spec(DEC_SEQ, nkb), vb_s, _rows_spec(DEC_SEQ, nvb), DEC_SEQ),
                                (kb_c, _rows_spec(PAST_LEN, nkb), vb_c, _rows_spec(PAST_LEN, nvb), PAST_LEN)],
                         nb=DEC_BATCH, q_rows=DEC_SEQ, hq=H_B, hkv=H_B, dqk=PAD_QK_B, dv=V_B, name="mla_attn_s")

        qc_p, kc_p, vc_p, kn_p = gqa_prep(z, gqa_q_norm_g, gqa_k_norm_g, l, row0=0, rows=T_P,
                                          tab=None, emit_kn=True)
        qc_s, kc_s, vc_s = gqa_prep(z, gqa_q_norm_g, gqa_k_norm_g, l, row0=T_P, rows=T_S,
                                    tab=tab_c, emit_kn=False)
        nkc = HKV_C * HD_C
        yc_p = attention(qc_p, [(kc_p, _rows_spec(SEQ, nkc), vc_p, _rows_spec(SEQ, nkc), SEQ)],
                         nb=BATCH, q_rows=SEQ, hq=HQ_C, hkv=HKV_C, dqk=HD_C, dv=HD_C, name="gqa_attn_p")
        cache_spec = pl.BlockSpec((None, None, PAST_LEN, nkc), lambda b: (b, l, 0, 0))
        yc_s = attention(qc_s, [(kc_s, _rows_spec(DEC_SEQ, nkc), vc_s, _rows_spec(DEC_SEQ, nkc), DEC_SEQ),
                                (ck, cache_spec, cv, cache_spec, PAST_LEN)],
                         nb=DEC_BATCH, q_rows=DEC_SEQ, hq=HQ_C, hkv=HKV_C, dqk=HD_C, dv=HD_C, name="gqa_attn_s")

        m = gated_merge((ya_p, ya_s, yb_p, yb_s, yc_p, yc_s), z, w_out_a, w_out_b, w_out_c, l)
        x = matmul_residual(m, w_o, l, x, mods, 2, bm=1024, bn=1024, name="w_o")
        h2 = norm_mod(x, norm2_g, mods, l, 3, 4)
        act = up_conv(h2, ffn_w_up, ffn_conv_w, ffn_conv_b, l)
        x = matmul_residual(act, ffn_w_down, l, x, mods, 5, bm=512, bn=512, name="ffn_down")

        st_out.append(st)
        ckv_out.append(ckv_p.reshape(BATCH, SEQ, KV_LORA_B))
        krope_out.append(z[:T_P, Z_KROPE:Z_KROPE + ROPE_B].astype(F32).reshape(BATCH, SEQ, ROPE_B))
        kc_out.append(kn_p.reshape(BATCH, SEQ, HKV_C, HD_C))
        vc_out.append(z[:T_P, Z_VC:Z_VC + nkc].astype(F32).reshape(BATCH, SEQ, HKV_C, HD_C))

    y_p = x[:T_P].reshape(BATCH, SEQ, d)
    y_s = x[T_P:].reshape(DEC_BATCH, DEC_SEQ, d)
    return (y_p, y_s, jnp.stack(st_out, axis=1), jnp.stack(ckv_out, axis=1), jnp.stack(krope_out, axis=1),
            jnp.stack(kc_out, axis=1), jnp.stack(vc_out, axis=1))
```

```python
import functools

import numpy as np
import jax
import jax.numpy as jnp
from jax import lax
from jax.experimental import pallas as pl
from jax.experimental.pallas import tpu as pltpu

F32 = jnp.float32
BF16 = jnp.bfloat16

D_MODEL = 2048
BATCH = 16
SEQ = 256
DEPTH = 4
DEC_BATCH = 2
DEC_SEQ = 1024
PAST_LEN = 512
GRID_W = 64
ROPE_THETA = 10000.0
EPS = 1e-6
H_A, DK_A, DV_A = 4, 128, 256
GLA_GATE_RANK = 16
GLA_GATE_NORM = 16.0
H_B, NOPE_B, ROPE_B, V_B = 8, 128, 64, 128
QK_B = NOPE_B + ROPE_B
Q_LORA_B = 512
KV_LORA_B = 256
HQ_C, HKV_C, HD_C = 8, 2, 128
D_FF = 5632

T_P = BATCH * SEQ
T_S = DEC_BATCH * DEC_SEQ
T_ALL = T_P + T_S
N_GROUPS = 1 + DEC_BATCH
SUBLANES = 8
MOD_ROWS = SUBLANES

_O_QA, _O_KA, _O_VA, _O_GA = 0, 512, 1024, 2048
_O_GKA = 3072
_O_QLORA = 3104
_O_KVA = 3616
_O_QC, _O_KC, _O_VC = 3936, 4960, 5216
_O_GATES = 5472
N_IN = 11616
W_IN_BN = 512
W_IN_RANGES = ((_O_QC, 15), (_O_QA, 6), (_O_QLORA, 2), (_O_GKA, 1))
N_Z = W_IN_BN * sum(nb for _, nb in W_IN_RANGES)
Z_QC, Z_KC, Z_VC, Z_GATES = 0, 1024, 1280, 1536
Z_QA, Z_KA, Z_VA, Z_GA = 7680, 8192, 8704, 9728
Z_QLORA, Z_CKV, Z_KROPE = 10752, 11264, 11520
Z_GK = 11776
PAD_QK_B = 256

GLA_CHUNK = 256
GLA_LEVELS = 8

VMEM_LIMIT = 56 * 1024 * 1024


def _cparams(n_axes):
    return pltpu.CompilerParams(
        dimension_semantics=("arbitrary",) * n_axes, vmem_limit_bytes=VMEM_LIMIT)


def _group_of_block(i, bm):
    n_p = T_P // bm
    per = DEC_SEQ // bm
    return jnp.where(i < n_p, 0, 1 + (i - n_p) // per)


def _adaln_kernel(c_ref, w_ref, b_ref, o_ref):
    c = c_ref[...]
    a = (c * jax.nn.sigmoid(c)).astype(BF16)
    o_ref[...] = jnp.dot(a, w_ref[...].astype(BF16), preferred_element_type=F32) + b_ref[...]


def adaln_all(cvec, w_ada, b_ada):
    bn = 1024
    n = 6 * D_MODEL
    return pl.pallas_call(
        _adaln_kernel,
        out_shape=jax.ShapeDtypeStruct((DEPTH, MOD_ROWS, n), F32),
        grid=(DEPTH, n // bn),
        in_specs=[
            pl.BlockSpec((MOD_ROWS, D_MODEL), lambda l, j: (0, 0)),
            pl.BlockSpec((None, D_MODEL, bn), lambda l, j: (l, 0, j)),
            pl.BlockSpec((None, 1, bn), lambda l, j: (l, 0, j)),
        ],
        out_specs=pl.BlockSpec((None, MOD_ROWS, bn), lambda l, j: (l, 0, j)),
        compiler_params=_cparams(2),
        name="adaln",
    )(cvec, w_ada, b_ada.reshape(DEPTH, 1, n))


def _norm_mod_kernel(x_ref, g_ref, sh_ref, sc_ref, o_ref, *, bm):
    grp = _group_of_block(pl.program_id(0), bm)
    x = x_ref[...]
    y = x * lax.rsqrt(jnp.mean(x * x, axis=-1, keepdims=True) + EPS) * g_ref[...]
    sh = sh_ref[pl.ds(grp, 1), :]
    sc = sc_ref[pl.ds(grp, 1), :]
    o_ref[...] = (y * (1.0 + sc) + sh).astype(o_ref.dtype)


def norm_mod(x, g, mods, l, which_shift, which_scale):
    bm = 512
    nd = D_MODEL
    return pl.pallas_call(
        functools.partial(_norm_mod_kernel, bm=bm),
        out_shape=jax.ShapeDtypeStruct((T_ALL, nd), BF16),
        grid=(T_ALL // bm,),
        in_specs=[
            pl.BlockSpec((bm, nd), lambda i: (i, 0)),
            pl.BlockSpec((None, 1, nd), lambda i: (l, 0, 0)),
            pl.BlockSpec((None, MOD_ROWS, nd), lambda i: (l, 0, which_shift)),
            pl.BlockSpec((None, MOD_ROWS, nd), lambda i: (l, 0, which_scale)),
        ],
        out_specs=pl.BlockSpec((bm, nd), lambda i: (i, 0)),
        compiler_params=_cparams(1),
        name="norm_mod",
    )(x, g.reshape(DEPTH, 1, nd), mods, mods)


def _dot_nt(a, b):
    return lax.dot_general(a, b, (((1,), (1,)), ((), ())), preferred_element_type=F32)


ROW_BLK = 1024


def _w_in_kernel(h_ref, wt_ref, o_ref, wb_ref):
    wb_ref[...] = wt_ref[0].T.astype(BF16)

    def body(r, carry):
        rows = pl.ds(pl.multiple_of(r * ROW_BLK, ROW_BLK), ROW_BLK)
        o_ref[rows, :] = jnp.dot(h_ref[rows, :], wb_ref[...], preferred_element_type=F32).astype(o_ref.dtype)
        return carry

    lax.fori_loop(0, T_ALL // ROW_BLK, body, 0)


def _w_in_first_col(j):
    col8 = None
    start = sum(nb for _, nb in W_IN_RANGES)
    for c0, nb in reversed(W_IN_RANGES):
        assert c0 % 8 == 0
        start -= nb
        here = c0 // 8 + (j - start) * (W_IN_BN // 8)
        col8 = here if col8 is None else jnp.where(j < start + nb, here, col8)
    return col8 * 8


def w_in_proj(h, w_in_t, l):
    bn = W_IN_BN
    return pl.pallas_call(
        _w_in_kernel,
        out_shape=jax.ShapeDtypeStruct((T_ALL, N_Z), BF16),
        grid=(N_Z // bn,),
        in_specs=[
            pl.BlockSpec((T_ALL, D_MODEL), lambda j: (0, 0), pipeline_mode=pl.Buffered(1)),
            pl.BlockSpec((pl.Element(1), pl.Element(bn), pl.Element(D_MODEL)),
                         lambda j: (l, _w_in_first_col(j), 0)),
        ],
        out_specs=pl.BlockSpec((T_ALL, bn), lambda j: (0, j)),
        scratch_shapes=[pltpu.VMEM((D_MODEL, bn), BF16)],
        compiler_params=_cparams(1),
        name="w_in",
    )(h, w_in_t)


def _mm_res_kernel(a_ref, w_ref, x_ref, gate_ref, o_ref, wb_ref, *, bm):
    @pl.when(pl.program_id(1) == 0)
    def _():
        wb_ref[...] = w_ref[...].astype(BF16)

    grp = _group_of_block(pl.program_id(1), bm)
    acc = jnp.dot(a_ref[...], wb_ref[...], preferred_element_type=F32)
    o_ref[...] = x_ref[...] + gate_ref[pl.ds(grp, 1), :] * acc


def matmul_residual(a, w, l, x, mods, which_gate, *, bm, bn, name):
    m, k = a.shape
    n = w.shape[-1]
    gate_blk = which_gate * (D_MODEL // bn)
    return pl.pallas_call(
        functools.partial(_mm_res_kernel, bm=bm),
        out_shape=jax.ShapeDtypeStruct((m, n), F32),
        grid=(n // bn, m // bm),
        in_specs=[
            pl.BlockSpec((bm, k), lambda j, i: (i, 0)),
            pl.BlockSpec((None, k, bn), lambda j, i: (l, 0, j)),
            pl.BlockSpec((bm, bn), lambda j, i: (i, j)),
            pl.BlockSpec((None, MOD_ROWS, bn), lambda j, i: (l, 0, gate_blk + j)),
        ],
        out_specs=pl.BlockSpec((bm, bn), lambda j, i: (i, j)),
        scratch_shapes=[pltpu.VMEM((k, bn), BF16)],
        compiler_params=_cparams(2),
        name=name,
    )(a, w, x, mods)


def _w_o_norm_kernel(m_ref, w_ref, x_ref, gate_ref, g_ref, sh_ref, sc_ref, xo_ref, h_ref, wb_ref, *, bm):
    i = pl.program_id(0)

    @pl.when(i == 0)
    def _():
        wb_ref[...] = w_ref[...].astype(BF16)

    grp = _group_of_block(i, bm)
    acc = jnp.dot(m_ref[...], wb_ref[...], preferred_element_type=F32)
    x = x_ref[...] + gate_ref[pl.ds(grp, 1), :] * acc
    xo_ref[...] = x
    y = x * lax.rsqrt(jnp.mean(x * x, axis=-1, keepdims=True) + EPS) * g_ref[...]
    h_ref[...] = (y * (1.0 + sc_ref[pl.ds(grp, 1), :]) + sh_ref[pl.ds(grp, 1), :]).astype(h_ref.dtype)


def w_o_norm(m, w_o, l, x, mods, norm_g):
    bm = 512
    nd = D_MODEL

    def mod_spec(which):
        return pl.BlockSpec((None, MOD_ROWS, nd), lambda i: (l, 0, which))

    return pl.pallas_call(
        functools.partial(_w_o_norm_kernel, bm=bm),
        out_shape=[jax.ShapeDtypeStruct((T_ALL, nd), F32), jax.ShapeDtypeStruct((T_ALL, nd), BF16)],
        grid=(T_ALL // bm,),
        in_specs=[
            pl.BlockSpec((bm, nd), lambda i: (i, 0)),
            pl.BlockSpec((None, nd, nd), lambda i: (l, 0, 0), pipeline_mode=pl.Buffered(1)),
            pl.BlockSpec((bm, nd), lambda i: (i, 0)),
            mod_spec(2),
            pl.BlockSpec((None, 1, nd), lambda i: (l, 0, 0)),
            mod_spec(3), mod_spec(4),
        ],
        out_specs=[pl.BlockSpec((bm, nd), lambda i: (i, 0)), pl.BlockSpec((bm, nd), lambda i: (i, 0))],
        scratch_shapes=[pltpu.VMEM((nd, nd), BF16)],
        compiler_params=_cparams(1),
        name="w_o_norm",
    )(m, w_o, x, mods, norm_g.reshape(DEPTH, 1, nd), mods, mods)


def _merge_kernel(yap_ref, yas_ref, ybp_ref, ybs_ref, ycp_ref, ycs_ref, wa_ref, wb_ref, wc_ref,
                  ga_ref, gb_ref, gc_ref, o_ref, wab_ref, wbb_ref, wcb_ref, *, bm):
    i = pl.program_id(1)

    @pl.when(i == 0)
    def _():
        wab_ref[...] = wa_ref[...].astype(BF16)
        wbb_ref[...] = wb_ref[...].astype(BF16)
        wcb_ref[...] = wc_ref[...].astype(BF16)

    is_ctx = i < T_P // bm

    def branch(yp_ref, ys_ref, w_ref, g_ref):
        y = jnp.where(is_ctx, yp_ref[...], ys_ref[...])
        return jax.nn.sigmoid(g_ref[...].astype(F32)) * jnp.dot(y, w_ref[...], preferred_element_type=F32)

    m = (branch(yap_ref, yas_ref, wab_ref, ga_ref) + branch(ybp_ref, ybs_ref, wbb_ref, gb_ref)
         + branch(ycp_ref, ycs_ref, wcb_ref, gc_ref))
    o_ref[...] = m.astype(o_ref.dtype)


def gated_merge(ys, z, w_out_a, w_out_b, w_out_c, l):
    bm, bn = 512, 512
    k = 1024
    gate0 = Z_GATES // bn
    per = D_MODEL // bn
    n_p = T_P // bm
    yp_spec = pl.BlockSpec((bm, k), lambda j, i: (jnp.minimum(i, n_p - 1), 0))
    ys_spec = pl.BlockSpec((bm, k), lambda j, i: (jnp.maximum(i - n_p, 0), 0))
    w_spec = pl.BlockSpec((None, k, bn), lambda j, i: (l, 0, j))

    def g_spec(br):
        return pl.BlockSpec((bm, bn), lambda j, i: (i, gate0 + br * per + j))

    return pl.pallas_call(
        functools.partial(_merge_kernel, bm=bm),
        out_shape=jax.ShapeDtypeStruct((T_ALL, D_MODEL), BF16),
        grid=(D_MODEL // bn, T_ALL // bm),
        in_specs=[yp_spec, ys_spec] * 3 + [w_spec] * 3 + [g_spec(0), g_spec(1), g_spec(2)],
        out_specs=pl.BlockSpec((bm, bn), lambda j, i: (i, j)),
        scratch_shapes=[pltpu.VMEM((k, bn), BF16)] * 3,
        compiler_params=_cparams(2),
        name="gated_merge",
    )(*ys, w_out_a, w_out_b, w_out_c, z, z, z)


def _up_conv_kernel(h_ref, wa_ref, wg_ref, cwa_ref, cwg_ref, cba_ref, cbg_ref, o_ref, wab_ref, wgb_ref):
    wab_ref[...] = wa_ref[...].astype(BF16)
    wgb_ref[...] = wg_ref[...].astype(BF16)
    bm = ROW_BLK
    sub = lax.broadcasted_iota(jnp.int32, (1, SUBLANES, 1), 1)

    n_rb = T_ALL // bm

    def dots(r):
        h = h_ref[r * bm:(r + 1) * bm, :]
        return (jnp.dot(h, wab_ref[...], preferred_element_type=F32),
                jnp.dot(h, wgb_ref[...], preferred_element_type=F32))

    nxt_u = dots(0)
    for r in range(n_rb):
        seq = SEQ if r < T_P // bm else DEC_SEQ
        tps = seq // SUBLANES
        n_seq = bm // seq
        ua, ug = nxt_u
        if r + 1 < n_rb:
            nxt_u = dots(r + 1)

        def conv(u, cw_ref, cb_ref):
            u3 = u.reshape(bm // SUBLANES, SUBLANES, u.shape[1])
            dn = pltpu.roll(u3, 1, 1)
            up = pltpu.roll(u3, SUBLANES - 1, 1)
            zero = jnp.zeros_like(u3[0:1])
            dn_prev = jnp.concatenate(
                [p for s in range(n_seq) for p in (zero, dn[s * tps:(s + 1) * tps - 1])], axis=0)
            up_next = jnp.concatenate(
                [p for s in range(n_seq) for p in (up[s * tps + 1:(s + 1) * tps], zero)], axis=0)
            prev = jnp.where(sub == 0, dn_prev, dn).reshape(u.shape)
            nxt = jnp.where(sub == SUBLANES - 1, up_next, up).reshape(u.shape)
            return prev * cw_ref[0:1, :] + u * cw_ref[1:2, :] + nxt * cw_ref[2:3, :] + cb_ref[...]

        a = conv(ua, cwa_ref, cba_ref)
        g = conv(ug, cwg_ref, cbg_ref)
        o_ref[r * bm:(r + 1) * bm, :] = (g * jax.nn.sigmoid(g) * a).astype(o_ref.dtype)


def up_conv(h2, w_up, conv_w, conv_b, l):
    bn = 256
    nb = D_FF // bn
    cb = conv_b.reshape(DEPTH, 1, 2 * D_FF)
    return pl.pallas_call(
        _up_conv_kernel,
        out_shape=jax.ShapeDtypeStruct((T_ALL, D_FF), BF16),
        grid=(nb,),
        in_specs=[
            pl.BlockSpec((T_ALL, D_MODEL), lambda j: (0, 0), pipeline_mode=pl.Buffered(1)),
            pl.BlockSpec((None, D_MODEL, bn), lambda j: (l, 0, j)),
            pl.BlockSpec((None, D_MODEL, bn), lambda j: (l, 0, nb + j)),
            pl.BlockSpec((None, 3, bn), lambda j: (l, 0, j)),
            pl.BlockSpec((None, 3, bn), lambda j: (l, 0, nb + j)),
            pl.BlockSpec((None, 1, bn), lambda j: (l, 0, j)),
            pl.BlockSpec((None, 1, bn), lambda j: (l, 0, nb + j)),
        ],
        out_specs=pl.BlockSpec((T_ALL, bn), lambda j: (0, j)),
        scratch_shapes=[pltpu.VMEM((D_MODEL, bn), BF16)] * 2,
        compiler_params=_cparams(1),
        name="up_conv",
    )(h2, w_up, w_up, conv_w, conv_w, cb, cb)


def _rope_tables(n_tok, rot_dim, width):
    rows = n_tok // GRID_W
    row = jnp.repeat(jnp.arange(rows, dtype=F32), GRID_W)
    col = jnp.tile(jnp.arange(GRID_W, dtype=F32), rows)
    half = rot_dim // 2
    inv = jnp.power(ROPE_THETA, -jnp.arange(half // 2, dtype=F32) * (2.0 / half))
    ang = jnp.concatenate([row[:, None] * inv, col[:, None] * inv], axis=-1)
    cos, sin = jnp.cos(ang), jnp.sin(ang)
    zero = jnp.zeros_like(sin)
    c2 = jnp.stack([cos, cos], axis=-1).reshape(n_tok, rot_dim)
    s_even = jnp.stack([-sin, zero], axis=-1).reshape(n_tok, rot_dim)
    s_odd = jnp.stack([zero, sin], axis=-1).reshape(n_tok, rot_dim)
    pad = ((0, 0), (0, width - rot_dim))
    return jnp.stack([jnp.pad(c2, pad), jnp.pad(s_even, pad), jnp.pad(s_odd, pad)], axis=0)


def _rope(x, tab_ref):
    w = x.shape[-1]
    return (x * tab_ref[0] + pltpu.roll(x, w - 1, 1) * tab_ref[1] + pltpu.roll(x, 1, 1) * tab_ref[2])


def _mla_q_kernel(ql_ref, g1_ref, w_ref, g2_ref, *rest, rope):
    if rope:
        tab_ref, o_ref, wb_ref = rest
    else:
        o_ref, wb_ref = rest

    @pl.when(pl.program_id(0) == 0)
    def _():
        wb_ref[...] = w_ref[...].astype(BF16)

    x = ql_ref[...].astype(F32)
    xn = x * lax.rsqrt(jnp.mean(x * x, axis=-1, keepdims=True) + EPS) * g1_ref[...]
    q = jnp.dot(xn.astype(BF16), wb_ref[...], preferred_element_type=F32)
    scale = QK_B ** -0.5
    for h in range(H_B):
        lo = h * PAD_QK_B
        qn = q[:, lo:lo + NOPE_B]
        qr = q[:, lo + NOPE_B:lo + PAD_QK_B]
        ms = (jnp.sum(qn * qn, axis=-1, keepdims=True) + jnp.sum(qr * qr, axis=-1, keepdims=True)) / QK_B
        r = lax.rsqrt(ms + EPS)
        qn = qn * r * g2_ref[:, 0:NOPE_B]
        qr = qr * r * g2_ref[:, NOPE_B:PAD_QK_B]
        if rope:
            qr = _rope(qr, tab_ref)
        o_ref[:, lo:lo + NOPE_B] = (qn * scale).astype(o_ref.dtype)
        o_ref[:, lo + NOPE_B:lo + PAD_QK_B] = (qr * scale).astype(o_ref.dtype)


def mla_q_proj(z, qa_g, w_qb_pad, q_g_pad, l, *, row0, rows, tab):
    bm = 512
    rope = tab is not None
    kq = Q_LORA_B
    n = H_B * PAD_QK_B
    rb0 = row0 // bm
    in_specs = [
        pl.BlockSpec((bm, kq), lambda i: (rb0 + i, Z_QLORA // kq)),
        pl.BlockSpec((None, 1, kq), lambda i: (l, 0, 0)),
        pl.BlockSpec((None, kq, n), lambda i: (l, 0, 0)),
        pl.BlockSpec((None, 1, PAD_QK_B), lambda i: (l, 0, 0)),
    ]
    args = [z, qa_g.reshape(DEPTH, 1, kq), w_qb_pad, q_g_pad]
    if rope:
        per = DEC_SEQ // bm
        in_specs.append(pl.BlockSpec((3, bm, 128), lambda i: (0, i % per, 0)))
        args.append(tab)
    return pl.pallas_call(
        functools.partial(_mla_q_kernel, rope=rope),
        out_shape=jax.ShapeDtypeStruct((rows, n), BF16),
        grid=(rows // bm,),
        in_specs=in_specs,
        out_specs=pl.BlockSpec((bm, n), lambda i: (i, 0)),
        scratch_shapes=[pltpu.VMEM((kq, n), BF16)],
        compiler_params=_cparams(1),
        name="mla_q_proj",
    )(*args)


def _mla_kv_kernel(c_ref, kr_ref, g1_ref, w_ref, g2_ref, *rest, norm_in, rope):
    rest = list(rest)
    tab_ref = rest.pop(0) if rope else None
    if norm_in:
        k_ref, v_ref, ckv_ref, wb_ref = rest
    else:
        k_ref, v_ref, wb_ref = rest

    @pl.when(pl.program_id(0) == 0)
    def _():
        wb_ref[...] = w_ref[...].astype(BF16)

    c = c_ref[...].astype(F32)
    if norm_in:
        c = c * lax.rsqrt(jnp.mean(c * c, axis=-1, keepdims=True) + EPS) * g1_ref[...]
        ckv_ref[...] = c
    kv = jnp.dot(c.astype(BF16), wb_ref[...], preferred_element_type=F32)
    kr = kr_ref[...].astype(F32)
    if kr.shape[-1] == 128:
        lane = lax.broadcasted_iota(jnp.int32, kr.shape, 1)
        kr = jnp.where(lane < ROPE_B, kr, 0.0)
    else:
        kr = jnp.concatenate([kr, jnp.zeros_like(kr)], axis=-1)
    kr_ss = jnp.sum(kr * kr, axis=-1, keepdims=True)
    for h in range(H_B):
        kn = kv[:, h * 256:h * 256 + NOPE_B]
        v = kv[:, h * 256 + NOPE_B:(h + 1) * 256]
        r = lax.rsqrt((jnp.sum(kn * kn, axis=-1, keepdims=True) + kr_ss) / QK_B + EPS)
        krh = kr * r * g2_ref[:, NOPE_B:PAD_QK_B]
        if rope:
            krh = _rope(krh, tab_ref)
        k_ref[:, h * PAD_QK_B:h * PAD_QK_B + NOPE_B] = (kn * r * g2_ref[:, 0:NOPE_B]).astype(k_ref.dtype)
        k_ref[:, h * PAD_QK_B + NOPE_B:(h + 1) * PAD_QK_B] = krh.astype(k_ref.dtype)
        v_ref[:, h * V_B:(h + 1) * V_B] = v.astype(v_ref.dtype)


def mla_kv_proj(c_arr, c_spec, kr_arr, kr_spec, kva_g, w_kvb, k_g_pad, l, *, rows, bm, norm_in, tab):
    rope = tab is not None
    kc = KV_LORA_B
    n = H_B * (NOPE_B + V_B)
    in_specs = [
        c_spec, kr_spec,
        pl.BlockSpec((None, 1, kc), lambda i: (l, 0, 0)),
        pl.BlockSpec((None, kc, n), lambda i: (l, 0, 0)),
        pl.BlockSpec((None, 1, PAD_QK_B), lambda i: (l, 0, 0)),
    ]
    args = [c_arr, kr_arr, kva_g.reshape(DEPTH, 1, kc), w_kvb, k_g_pad]
    if rope:
        per = DEC_SEQ // bm
        in_specs.append(pl.BlockSpec((3, bm, 128), lambda i: (0, i % per, 0)))
        args.append(tab)
    out_shape = [jax.ShapeDtypeStruct((rows, H_B * PAD_QK_B), BF16),
                 jax.ShapeDtypeStruct((rows, H_B * V_B), BF16)]
    out_specs = [pl.BlockSpec((bm, H_B * PAD_QK_B), lambda i: (i, 0)),
                 pl.BlockSpec((bm, H_B * V_B), lambda i: (i, 0))]
    if norm_in:
        out_shape.append(jax.ShapeDtypeStruct((rows, kc), F32))
        out_specs.append(pl.BlockSpec((bm, kc), lambda i: (i, 0)))
    return pl.pallas_call(
        functools.partial(_mla_kv_kernel, norm_in=norm_in, rope=rope),
        out_shape=out_shape,
        grid=(rows // bm,),
        in_specs=in_specs,
        out_specs=out_specs,
        scratch_shapes=[pltpu.VMEM((kc, n), BF16)],
        compiler_params=_cparams(1),
        name="mla_kv_proj",
    )(*args)


def _gqa_prep_kernel(q_ref, k_ref, v_ref, gq_ref, gk_ref, *rest, rope, emit_kn):
    rest = list(rest)
    tab_ref = rest.pop(0) if rope else None
    qo_ref, ko_ref, vo_ref = rest[:3]
    kn_ref = rest[3] if emit_kn else None
    scale = HD_C ** -0.5

    def norm(x, g_ref):
        return x * lax.rsqrt(jnp.mean(x * x, axis=-1, keepdims=True) + EPS) * g_ref[...]

    for h in range(HQ_C):
        x = norm(q_ref[:, h * HD_C:(h + 1) * HD_C].astype(F32), gq_ref)
        if rope:
            x = _rope(x, tab_ref)
        qo_ref[:, h * HD_C:(h + 1) * HD_C] = (x * scale).astype(qo_ref.dtype)
    for h in range(HKV_C):
        x = norm(k_ref[:, h * HD_C:(h + 1) * HD_C].astype(F32), gk_ref)
        if emit_kn:
            kn_ref[:, h * HD_C:(h + 1) * HD_C] = x
        if rope:
            x = _rope(x, tab_ref)
        ko_ref[:, h * HD_C:(h + 1) * HD_C] = x.astype(ko_ref.dtype)
    vo_ref[...] = v_ref[...].astype(vo_ref.dtype)


def gqa_prep(z, gq, gk, l, *, row0, rows, tab, emit_kn):
    bm = 512
    rope = tab is not None
    rb0 = row0 // bm
    nq, nk = HQ_C * HD_C, HKV_C * HD_C
    in_specs = [
        pl.BlockSpec((bm, nq), lambda i: (rb0 + i, Z_QC // nq)),
        pl.BlockSpec((bm, nk), lambda i: (rb0 + i, Z_KC // nk)),
        pl.BlockSpec((bm, nk), lambda i: (rb0 + i, Z_VC // nk)),
        pl.BlockSpec((None, 1, HD_C), lambda i: (l, 0, 0)),
        pl.BlockSpec((None, 1, HD_C), lambda i: (l, 0, 0)),
    ]
    args = [z, z, z, gq.reshape(DEPTH, 1, HD_C), gk.reshape(DEPTH, 1, HD_C)]
    if rope:
        per = DEC_SEQ // bm
        in_specs.append(pl.BlockSpec((3, bm, 128), lambda i: (0, i % per, 0)))
        args.append(tab)
    out_shape = [jax.ShapeDtypeStruct((rows, nq), BF16),
                 jax.ShapeDtypeStruct((rows, nk), BF16),
                 jax.ShapeDtypeStruct((rows, nk), BF16)]
    out_specs = [pl.BlockSpec((bm, nq), lambda i: (i, 0)),
                 pl.BlockSpec((bm, nk), lambda i: (i, 0)),
                 pl.BlockSpec((bm, nk), lambda i: (i, 0))]
    if emit_kn:
        out_shape.append(jax.ShapeDtypeStruct((rows, nk), F32))
        out_specs.append(pl.BlockSpec((bm, nk), lambda i: (i, 0)))
    return pl.pallas_call(
        functools.partial(_gqa_prep_kernel, rope=rope, emit_kn=emit_kn),
        out_shape=out_shape,
        grid=(rows // bm,),
        in_specs=in_specs,
        out_specs=out_specs,
        compiler_params=_cparams(1),
        name="gqa_prep",
    )(*args)


def _attn_kernel(*refs, n_seg, seg_rows, hq, hkv, dqk, dv, tq):
    q_ref = refs[0]
    kv_refs = refs[1:1 + 2 * n_seg]
    o_ref = refs[1 + 2 * n_seg]
    k_sc, v_sc = refs[2 + 2 * n_seg:]
    g = hq // hkv
    nq = q_ref.shape[0] // tq
    for j in range(hkv):
        r = 0
        for s in range(n_seg):
            k_sc[j, r:r + seg_rows[s], :] = kv_refs[2 * s][:, j * dqk:(j + 1) * dqk].astype(BF16)
            v_sc[j, r:r + seg_rows[s], :] = kv_refs[2 * s + 1][:, j * dv:(j + 1) * dv].astype(BF16)
            r += seg_rows[s]

    def q_block(qb, carry):
        rows = pl.ds(pl.multiple_of(qb * tq, tq), tq)
        for h in range(hq):
            j = h // g
            q = q_ref[rows, h * dqk:(h + 1) * dqk]
            s_ = _dot_nt(q, k_sc[j])
            m = jnp.max(s_, axis=-1, keepdims=True)
            p = jnp.exp(s_ - m)
            den = jnp.sum(p, axis=-1, keepdims=True)
            o = jnp.dot(p.astype(BF16), v_sc[j], preferred_element_type=F32) / den
            o_ref[rows, h * dv:(h + 1) * dv] = o.astype(o_ref.dtype)
        return carry

    if nq == 1:
        q_block(0, 0)
    else:
        lax.fori_loop(0, nq, q_block, 0)


def attention(q, segs, *, nb, q_rows, hq, hkv, dqk, dv, name):
    tq = 256
    in_specs = [pl.BlockSpec((q_rows, hq * dqk), lambda b: (b, 0))]
    args = [q]
    seg_rows = []
    for k_arr, k_spec, v_arr, v_spec, rows in segs:
        in_specs += [k_spec, v_spec]
        args += [k_arr, v_arr]
        seg_rows.append(rows)
    s_tot = sum(seg_rows)
    return pl.pallas_call(
        functools.partial(_attn_kernel, n_seg=len(segs), seg_rows=tuple(seg_rows), hq=hq, hkv=hkv,
                          dqk=dqk, dv=dv, tq=tq),
        out_shape=jax.ShapeDtypeStruct((nb * q_rows, hq * dv), BF16),
        grid=(nb,),
        in_specs=in_specs,
        out_specs=pl.BlockSpec((q_rows, hq * dv), lambda b: (b, 0)),
        scratch_shapes=[pltpu.VMEM((hkv, s_tot, dqk), BF16), pltpu.VMEM((hkv, s_tot, dv), BF16)],
        compiler_params=_cparams(1),
        name=name,
    )(*args)


def _rows_spec(rows, width, col_blk=0):
    return pl.BlockSpec((rows, width), lambda b: (b, col_blk))


def _gla_constants():
    c = GLA_CHUNK
    t = np.arange(c)[:, None]
    u = np.arange(c)[None, :]
    tril = (u <= t).astype(np.float32)
    hb = c // 2
    x = (t ^ u)[:hb, :hb]
    msb = np.where(x > 0, np.floor(np.log2(np.maximum(x, 1))), GLA_LEVELS - 1).astype(np.int32)
    return jnp.asarray(tril, BF16), jnp.asarray(msb)


def _split3(x):
    hi = x.astype(BF16)
    r = x - hi.astype(F32)
    mid = r.astype(BF16)
    lo = (r - mid.astype(F32)).astype(BF16)
    return [hi, mid, lo]


def _exp_neg_abs(x):
    return jnp.exp(-jnp.abs(x))


def _gla_kernel(q_ref, k_ref, v_ref, ga_ref, gk_ref, up_ref, bias_ref, ng_ref, tril_ref, msb_ref, *rest,
                n_chunks, has_s0):
    rest = list(rest)
    s0_ref = rest.pop(0) if has_s0 else None
    y_ref = rest.pop(0)
    st_ref = None if has_s0 else rest.pop(0)
    la_sc, cum_sc, kv_sc, tot_sc, sin_sc = rest
    c = GLA_CHUNK
    hb = c // 2
    nl = GLA_LEVELS
    dk = DK_A
    row = lax.broadcasted_iota(jnp.int32, (c, 1), 0)

    def pass1(ci, r0):
        rows = pl.ds(r0, c)
        gk = gk_ref[rows, :].astype(BF16)
        las = []
        for d in range(2):
            zg = jnp.dot(gk, up_ref[d].astype(BF16), preferred_element_type=F32) + bias_ref[d]
            las.append(jax.nn.log_sigmoid(zg) / GLA_GATE_NORM)
        parts = jnp.concatenate(_split3(las[0]) + _split3(las[1]), axis=1)
        cs = jnp.dot(tril_ref[...], parts, preferred_element_type=F32)
        cums = [cs[:, 0:dk] + cs[:, dk:2 * dk] + cs[:, 2 * dk:3 * dk],
                cs[:, 3 * dk:4 * dk] + cs[:, 4 * dk:5 * dk] + cs[:, 5 * dk:6 * dk]]
        for d in range(2):
            la_sc[d, rows, :] = las[d]
            cum_sc[d, rows, :] = cums[d]
        tot_f = cums[0][c - 1:c, :]
        tot_sc[ci] = jnp.concatenate([tot_f, cums[1][c - 1:c, :]], axis=1)
        k = k_ref[rows, :].astype(F32)
        kd = jnp.concatenate([k * _exp_neg_abs(tot_f - cums[0]), k * _exp_neg_abs(cums[1] - las[1])], axis=1)
        kv_sc[ci] = lax.dot_general(kd.astype(BF16), v_ref[rows, :].astype(BF16), (((0,), (0,)), ((), ())),
                                    preferred_element_type=F32)

    def pass2(ci, r0):
        rows = pl.ds(r0, c)
        laf, lab = la_sc[0, rows, :], la_sc[1, rows, :]
        cumf, cumb = cum_sc[0, rows, :], cum_sc[1, rows, :]
        cumbx = cumb - lab
        q = q_ref[rows, :].astype(F32) * (DK_A ** -0.5)
        k = k_ref[rows, :].astype(F32)
        v = v_ref[rows, :].astype(BF16)
        msb = msb_ref[...]
        q2 = (2.0 * q).astype(BF16)
        kb = k.astype(BF16)
        scd = [_dot_nt(q2[0:hb], kb[0:hb]), _dot_nt(q2[hb:], kb[hb:])]
        a_lo = a_up = None
        for lvl in range(nl):
            b = 1 << lvl
            second = (row & b) != 0
            if lvl == 0:
                qq = q * jnp.exp(jnp.where(second, laf, lab))
                kk = k
            elif lvl == 1:
                p = row & 3
                dq = jnp.where(p == 0, lab + pltpu.roll(lab, c - 1, 0),
                               jnp.where(p == 1, lab, jnp.where(p == 2, laf, laf + pltpu.roll(laf, 1, 0))))
                dkk = jnp.where(p == 0, pltpu.roll(laf, c - 1, 0), jnp.where(p == 3, pltpu.roll(lab, 1, 0), 0.0))
                qq = q * jnp.exp(dq)
                kk = k * jnp.exp(dkk)
            else:
                n = c // (2 * b)
                shp = (n, 2 * b, dk)
                cf3 = cumf.reshape(shp)
                cb3 = cumb.reshape(shp)
                ef = _exp_neg_abs(cf3 - cf3[:, b - 1:b, :]).reshape(c, dk)
                eb = _exp_neg_abs(cumbx.reshape(shp) - cb3[:, b - 1:b, :]).reshape(c, dk)
                qq = q * jnp.where(second, ef, eb)
                kk = k * jnp.where(second, eb, ef)
            qq = qq.astype(BF16)
            kk = kk.astype(BF16)
            if lvl < nl - 1:
                for blk in range(2):
                    a = _dot_nt(qq[blk * hb:(blk + 1) * hb], kk[blk * hb:(blk + 1) * hb])
                    scd[blk] = jnp.where(msb == lvl, a, scd[blk])
            else:
                a_lo = _dot_nt(qq[hb:], kk[:hb])
                a_up = _dot_nt(qq[:hb], kk[hb:])
        sc = jnp.concatenate([jnp.concatenate([scd[0], a_up], axis=1),
                              jnp.concatenate([a_lo, scd[1]], axis=1)], axis=0)
        o = jnp.dot(sc.astype(BF16), v, preferred_element_type=F32)
        if has_s0:
            tot_b = tot_sc[ci][:, dk:]
            qd = jnp.concatenate([q * _exp_neg_abs(cumf), q * _exp_neg_abs(tot_b - cumbx)], axis=1)
            o = o + jnp.dot(qd.astype(BF16), sin_sc[ci].astype(BF16), preferred_element_type=F32)
        o = o * lax.rsqrt(jnp.mean(o * o, axis=-1, keepdims=True) + EPS) * ng_ref[...]
        ga = ga_ref[rows, :].astype(F32)
        y_ref[rows, :] = (o * (ga * jax.nn.sigmoid(ga))).astype(y_ref.dtype)

    if n_chunks == 1:
        pass1(0, 0)
    else:
        lax.fori_loop(0, n_chunks, lambda i, carry: (pass1(i, pl.multiple_of(i * c, c)), carry)[1], 0)

    if has_s0:
        eye = lax.broadcasted_iota(jnp.int32, (2 * dk, 2 * dk), 0) == lax.broadcasted_iota(
            jnp.int32, (2 * dk, 2 * dk), 1)

        def decay_col(ci):
            tot = jnp.broadcast_to(tot_sc[ci], (2 * dk, 2 * dk))
            return jnp.exp(jnp.sum(jnp.where(eye, tot, 0.0), axis=1, keepdims=True))

        s = s0_ref[0]
        for ci in range(n_chunks):
            sin_sc[ci, 0:dk, :] = s
            if ci < n_chunks - 1:
                s = decay_col(ci)[0:dk] * s + kv_sc[ci, 0:dk, :]
        s = s0_ref[1]
        for ci in reversed(range(n_chunks)):
            sin_sc[ci, dk:2 * dk, :] = s
            if ci > 0:
                s = decay_col(ci)[dk:2 * dk] * s + kv_sc[ci, dk:2 * dk, :]
    else:
        st_ref[0] = kv_sc[0, 0:dk, :]
        st_ref[1] = kv_sc[0, dk:2 * dk, :]

    if n_chunks == 1:
        pass2(0, 0)
    else:
        lax.fori_loop(0, n_chunks, lambda i, carry: (pass2(i, pl.multiple_of(i * c, c)), carry)[1], 0)


def gla(z, up_pad, bias, norm_g, tril, msb, l, *, row0, nb, seq, s0):
    rb0 = row0 // seq
    n_chunks = seq // GLA_CHUNK
    has_s0 = s0 is not None
    assert has_s0 or n_chunks == 1
    hb = GLA_CHUNK // 2
    in_specs = [
        pl.BlockSpec((seq, DK_A), lambda b, h: (rb0 + b, Z_QA // DK_A + h)),
        pl.BlockSpec((seq, DK_A), lambda b, h: (rb0 + b, Z_KA // DK_A + h)),
        pl.BlockSpec((seq, DV_A), lambda b, h: (rb0 + b, Z_VA // DV_A + h)),
        pl.BlockSpec((seq, DV_A), lambda b, h: (rb0 + b, Z_GA // DV_A + h)),
        pl.BlockSpec((seq, 128), lambda b, h: (rb0 + b, Z_GK // 128)),
        pl.BlockSpec((None, 2, 128, DK_A), lambda b, h: (l, 0, 0, h)),
        pl.BlockSpec((None, 2, 1, DK_A), lambda b, h: (l, 0, 0, h)),
        pl.BlockSpec((None, 1, DV_A), lambda b, h: (l, 0, 0)),
        pl.BlockSpec((GLA_CHUNK, GLA_CHUNK), lambda b, h: (0, 0)),
        pl.BlockSpec((hb, hb), lambda b, h: (0, 0)),
    ]
    args = [z, z, z, z, z, up_pad, bias.reshape(DEPTH, 2, 1, H_A * DK_A),
            norm_g.reshape(DEPTH, 1, DV_A), tril, msb]
    out_shape = [jax.ShapeDtypeStruct((nb * seq, H_A * DV_A), BF16)]
    out_specs = [pl.BlockSpec((seq, DV_A), lambda b, h: (b, h))]
    if has_s0:
        in_specs.append(pl.BlockSpec((None, None, 2, None, DK_A, DV_A), lambda b, h: (b, l, 0, h, 0, 0)))
        args.append(s0)
    else:
        out_shape.append(jax.ShapeDtypeStruct((nb, 2, H_A, DK_A, DV_A), F32))
        out_specs.append(pl.BlockSpec((None, 2, None, DK_A, DV_A), lambda b, h: (b, 0, h, 0, 0)))
    return pl.pallas_call(
        functools.partial(_gla_kernel, n_chunks=n_chunks, has_s0=has_s0),
        out_shape=out_shape,
        grid=(nb, H_A),
        in_specs=in_specs,
        out_specs=out_specs,
        scratch_shapes=[pltpu.VMEM((2, seq, DK_A), F32), pltpu.VMEM((2, seq, DK_A), F32),
                        pltpu.VMEM((n_chunks, 2 * DK_A, DV_A), F32),
                        pltpu.VMEM((n_chunks, 1, 2 * DK_A), F32),
                        pltpu.VMEM((n_chunks, 2 * DK_A, DV_A), F32)],
        compiler_params=_cparams(2),
        name="gla",
    )(*args)


def kernel(x_prompt, x_sample, c, state_gla, cache_mla_ckv, cache_mla_krope, cache_gqa_k, cache_gqa_v,
           c_ctx, norm1_g, norm2_g, w_ada, b_ada, w_in, gla_gk_up, gla_gk_bias, gla_norm_g,
           mla_qa_norm_g, mla_w_qb, mla_kva_norm_g, mla_w_kvb, mla_q_norm_g, mla_k_norm_g,
           gqa_q_norm_g, gqa_k_norm_g, w_out_a, w_out_b, w_out_c, w_o,
           ffn_w_up, ffn_conv_w, ffn_conv_b, ffn_w_down):
    d = D_MODEL
    x = jnp.concatenate([x_prompt.reshape(T_P, d), x_sample.reshape(T_S, d)], axis=0)
    cvec = jnp.concatenate([c_ctx[None, :], c, jnp.zeros((MOD_ROWS - N_GROUPS, d), F32)], axis=0)
    mods = adaln_all(cvec, w_ada, b_ada)

    w_qb_pad = jnp.pad(mla_w_qb.reshape(DEPTH, Q_LORA_B, H_B, QK_B),
                       ((0, 0), (0, 0), (0, 0), (0, PAD_QK_B - QK_B))).reshape(DEPTH, Q_LORA_B, H_B * PAD_QK_B)
    q_g_pad = jnp.pad(mla_q_norm_g, ((0, 0), (0, PAD_QK_B - QK_B))).reshape(DEPTH, 1, PAD_QK_B)
    k_g_pad = jnp.pad(mla_k_norm_g, ((0, 0), (0, PAD_QK_B - QK_B))).reshape(DEPTH, 1, PAD_QK_B)
    up_pad = jnp.zeros((DEPTH, 2, 128, H_A * DK_A), F32)
    for dd in range(2):
        r0 = dd * GLA_GATE_RANK
        up_pad = up_pad.at[:, dd, r0:r0 + GLA_GATE_RANK, :].set(gla_gk_up[:, dd])
    tab_b = _rope_tables(DEC_SEQ, ROPE_B, 128)
    tab_c = _rope_tables(DEC_SEQ, HD_C, 128)
    tril, msb = _gla_constants()
    w_in_t = jnp.swapaxes(w_in, 1, 2)
    ck = cache_gqa_k.reshape(DEC_BATCH, DEPTH, PAST_LEN, HKV_C * HD_C)
    cv = cache_gqa_v.reshape(DEC_BATCH, DEPTH, PAST_LEN, HKV_C * HD_C)

    st_out, ckv_out, krope_out, kc_out, vc_out = [], [], [], [], []
    for l in range(DEPTH):
        h = norm_mod(x, norm1_g, mods, l, 0, 1)
        z = w_in_proj(h, w_in_t, l)

        ya_p, st = gla(z, up_pad, gla_gk_bias, gla_norm_g, tril, msb, l,
                       row0=0, nb=BATCH, seq=SEQ, s0=None)
        ya_s, = gla(z, up_pad, gla_gk_bias, gla_norm_g, tril, msb, l,
                    row0=T_P, nb=DEC_BATCH, seq=DEC_SEQ, s0=state_gla)

        bmk = 512
        qb_p = mla_q_proj(z, mla_qa_norm_g, w_qb_pad, q_g_pad, l, row0=0, rows=T_P, tab=None)
        qb_s = mla_q_proj(z, mla_qa_norm_g, w_qb_pad, q_g_pad, l, row0=T_P, rows=T_S, tab=tab_b)

        def tail_specs(row0):
            rb = row0 // bmk
            return (pl.BlockSpec((bmk, KV_LORA_B), lambda i: (rb + i, Z_CKV // KV_LORA_B)),
                    pl.BlockSpec((bmk, 128), lambda i: (rb + i, Z_KROPE // 128)))

        c_spec, kr_spec = tail_specs(0)
        kb_p, vb_p, ckv_p = mla_kv_proj(z, c_spec, z, kr_spec, mla_kva_norm_g, mla_w_kvb, k_g_pad, l,
                                        rows=T_P, bm=bmk, norm_in=True, tab=None)
        c_spec, kr_spec = tail_specs(T_P)
        kb_s, vb_s, _ = mla_kv_proj(z, c_spec, z, kr_spec, mla_kva_norm_g, mla_w_kvb, k_g_pad, l,
                                    rows=T_S, bm=bmk, norm_in=True, tab=tab_b)
        kb_c, vb_c = mla_kv_proj(
            cache_mla_ckv, pl.BlockSpec((None, None, PAST_LEN, KV_LORA_B), lambda i: (i, l, 0, 0)),
            cache_mla_krope, pl.BlockSpec((None, None, PAST_LEN, ROPE_B), lambda i: (i, l, 0, 0)),
            mla_kva_norm_g, mla_w_kvb, k_g_pad, l, rows=DEC_BATCH * PAST_LEN, bm=PAST_LEN, norm_in=False, tab=None)
        nkb, nvb = H_B * PAD_QK_B, H_B * V_B
        yb_p = attention(qb_p, [(kb_p, _rows_spec(SEQ, nkb), vb_p, _rows_spec(SEQ, nvb), SEQ)],
                         nb=BATCH, q_rows=SEQ, hq=H_B, hkv=H_B, dqk=PAD_QK_B, dv=V_B, name="mla_attn_p")
        yb_s = attention(qb_s, [(kb_s, _rows_spec(DEC_SEQ, nkb), vb_s, _rows_spec(DEC_SEQ, nvb), DEC_SEQ),
                                (kb_c, _rows_spec(PAST_LEN, nkb), vb_c, _rows_spec(PAST_LEN, nvb), PAST_LEN)],
                         nb=DEC_BATCH, q_rows=DEC_SEQ, hq=H_B, hkv=H_B, dqk=PAD_QK_B, dv=V_B, name="mla_attn_s")

        qc_p, kc_p, vc_p, kn_p = gqa_prep(z, gqa_q_norm_g, gqa_k_norm_g, l, row0=0, rows=T_P,
                                          tab=None, emit_kn=True)
        qc_s, kc_s, vc_s = gqa_prep(z, gqa_q_norm_g, gqa_k_norm_g, l, row0=T_P, rows=T_S,
                                    tab=tab_c, emit_kn=False)
        nkc = HKV_C * HD_C
        yc_p = attention(qc_p, [(kc_p, _rows_spec(SEQ, nkc), vc_p, _rows_spec(SEQ, nkc), SEQ)],
                         nb=BATCH, q_rows=SEQ, hq=HQ_C, hkv=HKV_C, dqk=HD_C, dv=HD_C, name="gqa_attn_p")
        cache_spec = pl.BlockSpec((None, None, PAST_LEN, nkc), lambda b: (b, l, 0, 0))
        yc_s = attention(qc_s, [(kc_s, _rows_spec(DEC_SEQ, nkc), vc_s, _rows_spec(DEC_SEQ, nkc), DEC_SEQ),
                                (ck, cache_spec, cv, cache_spec, PAST_LEN)],
                         nb=DEC_BATCH, q_rows=DEC_SEQ, hq=HQ_C, hkv=HKV_C, dqk=HD_C, dv=HD_C, name="gqa_attn_s")

        m = gated_merge((ya_p, ya_s, yb_p, yb_s, yc_p, yc_s), z, w_out_a, w_out_b, w_out_c, l)
        x, h2 = w_o_norm(m, w_o, l, x, mods, norm2_g)
        act = up_conv(h2, ffn_w_up, ffn_conv_w, ffn_conv_b, l)
        x = matmul_residual(act, ffn_w_down, l, x, mods, 5, bm=512, bn=512, name="ffn_down")

        st_out.append(st)
        ckv_out.append(ckv_p.reshape(BATCH, SEQ, KV_LORA_B))
        krope_out.append(z[:T_P, Z_KROPE:Z_KROPE + ROPE_B].astype(F32).reshape(BATCH, SEQ, ROPE_B))
        kc_out.append(kn_p.reshape(BATCH, SEQ, HKV_C, HD_C))
        vc_out.append(z[:T_P, Z_VC:Z_VC + nkc].astype(F32).reshape(BATCH, SEQ, HKV_C, HD_C))

    y_p = x[:T_P].reshape(BATCH, SEQ, d)
    y_s = x[T_P:].reshape(DEC_BATCH, DEC_SEQ, d)
    return (y_p, y_s, jnp.stack(st_out, axis=1), jnp.stack(ckv_out, axis=1), jnp.stack(krope_out, axis=1),
            jnp.stack(kc_out, axis=1), jnp.stack(vc_out, axis=1))
```

```python
import functools

import numpy as np
import jax
import jax.numpy as jnp
from jax import lax
from jax.experimental import pallas as pl
from jax.experimental.pallas import tpu as pltpu

F32 = jnp.float32
BF16 = jnp.bfloat16

D_MODEL = 2048
BATCH = 16
SEQ = 256
DEPTH = 4
DEC_BATCH = 2
DEC_SEQ = 1024
PAST_LEN = 512
GRID_W = 64
ROPE_THETA = 10000.0
EPS = 1e-6
H_A, DK_A, DV_A = 4, 128, 256
GLA_GATE_RANK = 16
GLA_GATE_NORM = 16.0
H_B, NOPE_B, ROPE_B, V_B = 8, 128, 64, 128
QK_B = NOPE_B + ROPE_B
Q_LORA_B = 512
KV_LORA_B = 256
HQ_C, HKV_C, HD_C = 8, 2, 128
D_FF = 5632

T_P = BATCH * SEQ
T_S = DEC_BATCH * DEC_SEQ
T_ALL = T_P + T_S
N_GROUPS = 1 + DEC_BATCH
SUBLANES = 8
MOD_ROWS = SUBLANES

_O_QA, _O_KA, _O_VA, _O_GA = 0, 512, 1024, 2048
_O_GKA = 3072
_O_QLORA = 3104
_O_KVA = 3616
_O_QC, _O_KC, _O_VC = 3936, 4960, 5216
_O_GATES = 5472
N_IN = 11616
W_IN_BN = 512
W_IN_RANGES = ((_O_QC, 15), (_O_QA, 6), (_O_QLORA, 2), (_O_GKA, 1))
N_Z = W_IN_BN * sum(nb for _, nb in W_IN_RANGES)
Z_QC, Z_KC, Z_VC, Z_GATES = 0, 1024, 1280, 1536
Z_QA, Z_KA, Z_VA, Z_GA = 7680, 8192, 8704, 9728
Z_QLORA, Z_CKV, Z_KROPE = 10752, 11264, 11520
Z_GK = 11776
PAD_QK_B = 256

GLA_CHUNK = 256
GLA_LEVELS = 8

VMEM_LIMIT = 56 * 1024 * 1024


def _cparams(n_axes):
    return pltpu.CompilerParams(
        dimension_semantics=("arbitrary",) * n_axes, vmem_limit_bytes=VMEM_LIMIT)


def _group_of_block(i, bm):
    n_p = T_P // bm
    per = DEC_SEQ // bm
    return jnp.where(i < n_p, 0, 1 + (i - n_p) // per)


def _adaln_kernel(c_ref, w_ref, b_ref, o_ref):
    c = c_ref[...]
    a = (c * jax.nn.sigmoid(c)).astype(BF16)
    o_ref[...] = jnp.dot(a, w_ref[...].astype(BF16), preferred_element_type=F32) + b_ref[...]


def adaln_all(cvec, w_ada, b_ada):
    bn = 1024
    n = 6 * D_MODEL
    return pl.pallas_call(
        _adaln_kernel,
        out_shape=jax.ShapeDtypeStruct((DEPTH, MOD_ROWS, n), F32),
        grid=(DEPTH, n // bn),
        in_specs=[
            pl.BlockSpec((MOD_ROWS, D_MODEL), lambda l, j: (0, 0)),
            pl.BlockSpec((None, D_MODEL, bn), lambda l, j: (l, 0, j)),
            pl.BlockSpec((None, 1, bn), lambda l, j: (l, 0, j)),
        ],
        out_specs=pl.BlockSpec((None, MOD_ROWS, bn), lambda l, j: (l, 0, j)),
        compiler_params=_cparams(2),
        name="adaln",
    )(cvec, w_ada, b_ada.reshape(DEPTH, 1, n))


def _norm_mod_kernel(x_ref, g_ref, sh_ref, sc_ref, o_ref, *, bm):
    grp = _group_of_block(pl.program_id(0), bm)
    x = x_ref[...]
    y = x * lax.rsqrt(jnp.mean(x * x, axis=-1, keepdims=True) + EPS) * g_ref[...]
    sh = sh_ref[pl.ds(grp, 1), :]
    sc = sc_ref[pl.ds(grp, 1), :]
    o_ref[...] = (y * (1.0 + sc) + sh).astype(o_ref.dtype)


def norm_mod(x, g, mods, l, which_shift, which_scale):
    bm = 512
    nd = D_MODEL
    return pl.pallas_call(
        functools.partial(_norm_mod_kernel, bm=bm),
        out_shape=jax.ShapeDtypeStruct((T_ALL, nd), BF16),
        grid=(T_ALL // bm,),
        in_specs=[
            pl.BlockSpec((bm, nd), lambda i: (i, 0)),
            pl.BlockSpec((None, 1, nd), lambda i: (l, 0, 0)),
            pl.BlockSpec((None, MOD_ROWS, nd), lambda i: (l, 0, which_shift)),
            pl.BlockSpec((None, MOD_ROWS, nd), lambda i: (l, 0, which_scale)),
        ],
        out_specs=pl.BlockSpec((bm, nd), lambda i: (i, 0)),
        compiler_params=_cparams(1),
        name="norm_mod",
    )(x, g.reshape(DEPTH, 1, nd), mods, mods)


def _dot_nt(a, b):
    return lax.dot_general(a, b, (((1,), (1,)), ((), ())), preferred_element_type=F32)


ROW_BLK = 1024


def _w_in_kernel(h_ref, wt_ref, o_ref, wb_ref):
    wb_ref[...] = wt_ref[0].T.astype(BF16)

    def body(r, carry):
        rows = pl.ds(pl.multiple_of(r * ROW_BLK, ROW_BLK), ROW_BLK)
        o_ref[rows, :] = jnp.dot(h_ref[rows, :], wb_ref[...], preferred_element_type=F32).astype(o_ref.dtype)
        return carry

    lax.fori_loop(0, T_ALL // ROW_BLK, body, 0)


def _w_in_first_col(j):
    col8 = None
    start = sum(nb for _, nb in W_IN_RANGES)
    for c0, nb in reversed(W_IN_RANGES):
        assert c0 % 8 == 0
        start -= nb
        here = c0 // 8 + (j - start) * (W_IN_BN // 8)
        col8 = here if col8 is None else jnp.where(j < start + nb, here, col8)
    return col8 * 8


def w_in_proj(h, w_in_t, l):
    bn = W_IN_BN
    return pl.pallas_call(
        _w_in_kernel,
        out_shape=jax.ShapeDtypeStruct((T_ALL, N_Z), BF16),
        grid=(N_Z // bn,),
        in_specs=[
            pl.BlockSpec((T_ALL, D_MODEL), lambda j: (0, 0), pipeline_mode=pl.Buffered(1)),
            pl.BlockSpec((pl.Element(1), pl.Element(bn), pl.Element(D_MODEL)),
                         lambda j: (l, _w_in_first_col(j), 0)),
        ],
        out_specs=pl.BlockSpec((T_ALL, bn), lambda j: (0, j)),
        scratch_shapes=[pltpu.VMEM((D_MODEL, bn), BF16)],
        compiler_params=_cparams(1),
        name="w_in",
    )(h, w_in_t)


def _mm_res_kernel(a_ref, w_ref, x_ref, gate_ref, o_ref, wb_ref, *, bm):
    @pl.when(pl.program_id(1) == 0)
    def _():
        wb_ref[...] = w_ref[...].astype(BF16)

    grp = _group_of_block(pl.program_id(1), bm)
    acc = jnp.dot(a_ref[...], wb_ref[...], preferred_element_type=F32)
    o_ref[...] = x_ref[...] + gate_ref[pl.ds(grp, 1), :] * acc


def matmul_residual(a, w, l, x, mods, which_gate, *, bm, bn, name):
    m, k = a.shape
    n = w.shape[-1]
    gate_blk = which_gate * (D_MODEL // bn)
    return pl.pallas_call(
        functools.partial(_mm_res_kernel, bm=bm),
        out_shape=jax.ShapeDtypeStruct((m, n), F32),
        grid=(n // bn, m // bm),
        in_specs=[
            pl.BlockSpec((bm, k), lambda j, i: (i, 0)),
            pl.BlockSpec((None, k, bn), lambda j, i: (l, 0, j)),
            pl.BlockSpec((bm, bn), lambda j, i: (i, j)),
            pl.BlockSpec((None, MOD_ROWS, bn), lambda j, i: (l, 0, gate_blk + j)),
        ],
        out_specs=pl.BlockSpec((bm, bn), lambda j, i: (i, j)),
        scratch_shapes=[pltpu.VMEM((k, bn), BF16)],
        compiler_params=_cparams(2),
        name=name,
    )(a, w, x, mods)


def _w_o_norm_kernel(m_ref, w_ref, x_ref, gate_ref, g_ref, sh_ref, sc_ref, xo_ref, h_ref, wb_ref, *, bm):
    i = pl.program_id(0)

    @pl.when(i == 0)
    def _():
        wb_ref[...] = w_ref[...].astype(BF16)

    grp = _group_of_block(i, bm)
    acc = jnp.dot(m_ref[...], wb_ref[...], preferred_element_type=F32)
    x = x_ref[...] + gate_ref[pl.ds(grp, 1), :] * acc
    xo_ref[...] = x
    y = x * lax.rsqrt(jnp.mean(x * x, axis=-1, keepdims=True) + EPS) * g_ref[...]
    h_ref[...] = (y * (1.0 + sc_ref[pl.ds(grp, 1), :]) + sh_ref[pl.ds(grp, 1), :]).astype(h_ref.dtype)


def w_o_norm(m, w_o, l, x, mods, norm_g):
    bm = 512
    nd = D_MODEL

    def mod_spec(which):
        return pl.BlockSpec((None, MOD_ROWS, nd), lambda i: (l, 0, which))

    return pl.pallas_call(
        functools.partial(_w_o_norm_kernel, bm=bm),
        out_shape=[jax.ShapeDtypeStruct((T_ALL, nd), F32), jax.ShapeDtypeStruct((T_ALL, nd), BF16)],
        grid=(T_ALL // bm,),
        in_specs=[
            pl.BlockSpec((bm, nd), lambda i: (i, 0)),
            pl.BlockSpec((None, nd, nd), lambda i: (l, 0, 0), pipeline_mode=pl.Buffered(1)),
            pl.BlockSpec((bm, nd), lambda i: (i, 0)),
            mod_spec(2),
            pl.BlockSpec((None, 1, nd), lambda i: (l, 0, 0)),
            mod_spec(3), mod_spec(4),
        ],
        out_specs=[pl.BlockSpec((bm, nd), lambda i: (i, 0)), pl.BlockSpec((bm, nd), lambda i: (i, 0))],
        scratch_shapes=[pltpu.VMEM((nd, nd), BF16)],
        compiler_params=_cparams(1),
        name="w_o_norm",
    )(m, w_o, x, mods, norm_g.reshape(DEPTH, 1, nd), mods, mods)


def _merge_kernel(yap_ref, yas_ref, ybp_ref, ybs_ref, ycp_ref, ycs_ref, wa_ref, wb_ref, wc_ref,
                  ga_ref, gb_ref, gc_ref, o_ref, wab_ref, wbb_ref, wcb_ref, *, bm):
    i = pl.program_id(1)

    @pl.when(i == 0)
    def _():
        wab_ref[...] = wa_ref[...].astype(BF16)
        wbb_ref[...] = wb_ref[...].astype(BF16)
        wcb_ref[...] = wc_ref[...].astype(BF16)

    is_ctx = i < T_P // bm

    def branch(yp_ref, ys_ref, w_ref, g_ref):
        y = jnp.where(is_ctx, yp_ref[...], ys_ref[...])
        return jax.nn.sigmoid(g_ref[...].astype(F32)) * jnp.dot(y, w_ref[...], preferred_element_type=F32)

    m = (branch(yap_ref, yas_ref, wab_ref, ga_ref) + branch(ybp_ref, ybs_ref, wbb_ref, gb_ref)
         + branch(ycp_ref, ycs_ref, wcb_ref, gc_ref))
    o_ref[...] = m.astype(o_ref.dtype)


def gated_merge(ys, z, w_out_a, w_out_b, w_out_c, l):
    bm, bn = 512, 512
    k = 1024
    gate0 = Z_GATES // bn
    per = D_MODEL // bn
    n_p = T_P // bm
    yp_spec = pl.BlockSpec((bm, k), lambda j, i: (jnp.minimum(i, n_p - 1), 0))
    ys_spec = pl.BlockSpec((bm, k), lambda j, i: (jnp.maximum(i - n_p, 0), 0))
    w_spec = pl.BlockSpec((None, k, bn), lambda j, i: (l, 0, j))

    def g_spec(br):
        return pl.BlockSpec((bm, bn), lambda j, i: (i, gate0 + br * per + j))

    return pl.pallas_call(
        functools.partial(_merge_kernel, bm=bm),
        out_shape=jax.ShapeDtypeStruct((T_ALL, D_MODEL), BF16),
        grid=(D_MODEL // bn, T_ALL // bm),
        in_specs=[yp_spec, ys_spec] * 3 + [w_spec] * 3 + [g_spec(0), g_spec(1), g_spec(2)],
        out_specs=pl.BlockSpec((bm, bn), lambda j, i: (i, j)),
        scratch_shapes=[pltpu.VMEM((k, bn), BF16)] * 3,
        compiler_params=_cparams(2),
        name="gated_merge",
    )(*ys, w_out_a, w_out_b, w_out_c, z, z, z)


def _up_conv_kernel(h_ref, wa_ref, wg_ref, cwa_ref, cwg_ref, cba_ref, cbg_ref, o_ref, wab_ref, wgb_ref,
                    *sc_refs):
    wab_ref[...] = wa_ref[...].astype(BF16)
    wgb_ref[...] = wg_ref[...].astype(BF16)
    bm = ROW_BLK
    nt = bm // SUBLANES
    bn = o_ref.shape[1]
    sub = lax.broadcasted_iota(jnp.int32, (1, SUBLANES, 1), 1)
    for sc_ref in sc_refs:
        sc_ref[0:SUBLANES, :] = jnp.zeros((SUBLANES, bn), F32)
        sc_ref[SUBLANES + bm:2 * SUBLANES + bm, :] = jnp.zeros((SUBLANES, bn), F32)

    def taps(cw_ref, cb_ref):
        return [jnp.broadcast_to(cw_ref[k:k + 1, :], (SUBLANES, bn))[None] for k in range(3)] + [
            jnp.broadcast_to(cb_ref[...], (SUBLANES, bn))[None]]

    taps_a = taps(cwa_ref, cba_ref)
    taps_g = taps(cwg_ref, cbg_ref)

    for r in range(T_ALL // bm):
        seq = SEQ if r < T_P // bm else DEC_SEQ
        tps = seq // SUBLANES
        n_seq = bm // seq
        h = h_ref[r * bm:(r + 1) * bm, :]

        def conv(w_ref, tp, sc_ref):
            u = jnp.dot(h, w_ref[...], preferred_element_type=F32)
            sc_ref[SUBLANES:SUBLANES + bm, :] = u
            prev = sc_ref[SUBLANES - 1:SUBLANES - 1 + bm, :].reshape(nt, SUBLANES, bn)
            nxt = sc_ref[SUBLANES + 1:SUBLANES + 1 + bm, :].reshape(nt, SUBLANES, bn)
            if n_seq > 1:
                prev = jnp.concatenate(
                    [p for s in range(n_seq) for p in (
                        jnp.where(sub == 0, 0.0, prev[s * tps:s * tps + 1]), prev[s * tps + 1:(s + 1) * tps])],
                    axis=0)
                nxt = jnp.concatenate(
                    [p for s in range(n_seq) for p in (
                        nxt[s * tps:(s + 1) * tps - 1],
                        jnp.where(sub == SUBLANES - 1, 0.0, nxt[(s + 1) * tps - 1:(s + 1) * tps]))],
                    axis=0)
            return prev * tp[0] + u.reshape(nt, SUBLANES, bn) * tp[1] + nxt * tp[2] + tp[3]

        a = conv(wab_ref, taps_a, sc_refs[0])
        g = conv(wgb_ref, taps_g, sc_refs[1])
        act = (g * jax.nn.sigmoid(g) * a).reshape(bm, bn)
        o_ref[r * bm:(r + 1) * bm, :] = act.astype(o_ref.dtype)


def up_conv(h2, w_up, conv_w, conv_b, l):
    bn = 256
    nb = D_FF // bn
    cb = conv_b.reshape(DEPTH, 1, 2 * D_FF)
    return pl.pallas_call(
        _up_conv_kernel,
        out_shape=jax.ShapeDtypeStruct((T_ALL, D_FF), BF16),
        grid=(nb,),
        in_specs=[
            pl.BlockSpec((T_ALL, D_MODEL), lambda j: (0, 0), pipeline_mode=pl.Buffered(1)),
            pl.BlockSpec((None, D_MODEL, bn), lambda j: (l, 0, j)),
            pl.BlockSpec((None, D_MODEL, bn), lambda j: (l, 0, nb + j)),
            pl.BlockSpec((None, 3, bn), lambda j: (l, 0, j)),
            pl.BlockSpec((None, 3, bn), lambda j: (l, 0, nb + j)),
            pl.BlockSpec((None, 1, bn), lambda j: (l, 0, j)),
            pl.BlockSpec((None, 1, bn), lambda j: (l, 0, nb + j)),
        ],
        out_specs=pl.BlockSpec((T_ALL, bn), lambda j: (0, j)),
        scratch_shapes=[pltpu.VMEM((D_MODEL, bn), BF16)] * 2
        + [pltpu.VMEM((ROW_BLK + 2 * SUBLANES, bn), F32)] * 2,
        compiler_params=_cparams(1),
        name="up_conv",
    )(h2, w_up, w_up, conv_w, conv_w, cb, cb)


def _rope_tables(n_tok, rot_dim, width):
    rows = n_tok // GRID_W
    row = jnp.repeat(jnp.arange(rows, dtype=F32), GRID_W)
    col = jnp.tile(jnp.arange(GRID_W, dtype=F32), rows)
    half = rot_dim // 2
    inv = jnp.power(ROPE_THETA, -jnp.arange(half // 2, dtype=F32) * (2.0 / half))
    ang = jnp.concatenate([row[:, None] * inv, col[:, None] * inv], axis=-1)
    cos, sin = jnp.cos(ang), jnp.sin(ang)
    zero = jnp.zeros_like(sin)
    c2 = jnp.stack([cos, cos], axis=-1).reshape(n_tok, rot_dim)
    s_even = jnp.stack([-sin, zero], axis=-1).reshape(n_tok, rot_dim)
    s_odd = jnp.stack([zero, sin], axis=-1).reshape(n_tok, rot_dim)
    pad = ((0, 0), (0, width - rot_dim))
    return jnp.stack([jnp.pad(c2, pad), jnp.pad(s_even, pad), jnp.pad(s_odd, pad)], axis=0)


def _rope(x, tab_ref):
    w = x.shape[-1]
    return (x * tab_ref[0] + pltpu.roll(x, w - 1, 1) * tab_ref[1] + pltpu.roll(x, 1, 1) * tab_ref[2])


def _mla_q_kernel(ql_ref, g1_ref, w_ref, g2_ref, *rest, rope):
    if rope:
        tab_ref, o_ref, wb_ref = rest
    else:
        o_ref, wb_ref = rest

    @pl.when(pl.program_id(0) == 0)
    def _():
        wb_ref[...] = w_ref[...].astype(BF16)

    x = ql_ref[...].astype(F32)
    xn = x * lax.rsqrt(jnp.mean(x * x, axis=-1, keepdims=True) + EPS) * g1_ref[...]
    q = jnp.dot(xn.astype(BF16), wb_ref[...], preferred_element_type=F32)
    scale = QK_B ** -0.5
    for h in range(H_B):
        lo = h * PAD_QK_B
        qn = q[:, lo:lo + NOPE_B]
        qr = q[:, lo + NOPE_B:lo + PAD_QK_B]
        ms = (jnp.sum(qn * qn, axis=-1, keepdims=True) + jnp.sum(qr * qr, axis=-1, keepdims=True)) / QK_B
        r = lax.rsqrt(ms + EPS)
        qn = qn * r * g2_ref[:, 0:NOPE_B]
        qr = qr * r * g2_ref[:, NOPE_B:PAD_QK_B]
        if rope:
            qr = _rope(qr, tab_ref)
        o_ref[:, lo:lo + NOPE_B] = (qn * scale).astype(o_ref.dtype)
        o_ref[:, lo + NOPE_B:lo + PAD_QK_B] = (qr * scale).astype(o_ref.dtype)


def mla_q_proj(z, qa_g, w_qb_pad, q_g_pad, l, *, row0, rows, tab):
    bm = 512
    rope = tab is not None
    kq = Q_LORA_B
    n = H_B * PAD_QK_B
    rb0 = row0 // bm
    in_specs = [
        pl.BlockSpec((bm, kq), lambda i: (rb0 + i, Z_QLORA // kq)),
        pl.BlockSpec((None, 1, kq), lambda i: (l, 0, 0)),
        pl.BlockSpec((None, kq, n), lambda i: (l, 0, 0)),
        pl.BlockSpec((None, 1, PAD_QK_B), lambda i: (l, 0, 0)),
    ]
    args = [z, qa_g.reshape(DEPTH, 1, kq), w_qb_pad, q_g_pad]
    if rope:
        per = DEC_SEQ // bm
        in_specs.append(pl.BlockSpec((3, bm, 128), lambda i: (0, i % per, 0)))
        args.append(tab)
    return pl.pallas_call(
        functools.partial(_mla_q_kernel, rope=rope),
        out_shape=jax.ShapeDtypeStruct((rows, n), BF16),
        grid=(rows // bm,),
        in_specs=in_specs,
        out_specs=pl.BlockSpec((bm, n), lambda i: (i, 0)),
        scratch_shapes=[pltpu.VMEM((kq, n), BF16)],
        compiler_params=_cparams(1),
        name="mla_q_proj",
    )(*args)


def _mla_kv_kernel(c_ref, kr_ref, g1_ref, w_ref, g2_ref, *rest, norm_in, rope):
    rest = list(rest)
    tab_ref = rest.pop(0) if rope else None
    if norm_in:
        k_ref, v_ref, ckv_ref, wb_ref = rest
    else:
        k_ref, v_ref, wb_ref = rest

    @pl.when(pl.program_id(0) == 0)
    def _():
        wb_ref[...] = w_ref[...].astype(BF16)

    c = c_ref[...].astype(F32)
    if norm_in:
        c = c * lax.rsqrt(jnp.mean(c * c, axis=-1, keepdims=True) + EPS) * g1_ref[...]
        ckv_ref[...] = c
    kv = jnp.dot(c.astype(BF16), wb_ref[...], preferred_element_type=F32)
    kr = kr_ref[...].astype(F32)
    if kr.shape[-1] == 128:
        lane = lax.broadcasted_iota(jnp.int32, kr.shape, 1)
        kr = jnp.where(lane < ROPE_B, kr, 0.0)
    else:
        kr = jnp.concatenate([kr, jnp.zeros_like(kr)], axis=-1)
    kr_ss = jnp.sum(kr * kr, axis=-1, keepdims=True)
    for h in range(H_B):
        kn = kv[:, h * 256:h * 256 + NOPE_B]
        v = kv[:, h * 256 + NOPE_B:(h + 1) * 256]
        r = lax.rsqrt((jnp.sum(kn * kn, axis=-1, keepdims=True) + kr_ss) / QK_B + EPS)
        krh = kr * r * g2_ref[:, NOPE_B:PAD_QK_B]
        if rope:
            krh = _rope(krh, tab_ref)
        k_ref[:, h * PAD_QK_B:h * PAD_QK_B + NOPE_B] = (kn * r * g2_ref[:, 0:NOPE_B]).astype(k_ref.dtype)
        k_ref[:, h * PAD_QK_B + NOPE_B:(h + 1) * PAD_QK_B] = krh.astype(k_ref.dtype)
        v_ref[:, h * V_B:(h + 1) * V_B] = v.astype(v_ref.dtype)


def mla_kv_proj(c_arr, c_spec, kr_arr, kr_spec, kva_g, w_kvb, k_g_pad, l, *, rows, bm, norm_in, tab):
    rope = tab is not None
    kc = KV_LORA_B
    n = H_B * (NOPE_B + V_B)
    in_specs = [
        c_spec, kr_spec,
        pl.BlockSpec((None, 1, kc), lambda i: (l, 0, 0)),
        pl.BlockSpec((None, kc, n), lambda i: (l, 0, 0)),
        pl.BlockSpec((None, 1, PAD_QK_B), lambda i: (l, 0, 0)),
    ]
    args = [c_arr, kr_arr, kva_g.reshape(DEPTH, 1, kc), w_kvb, k_g_pad]
    if rope:
        per = DEC_SEQ // bm
        in_specs.append(pl.BlockSpec((3, bm, 128), lambda i: (0, i % per, 0)))
        args.append(tab)
    out_shape = [jax.ShapeDtypeStruct((rows, H_B * PAD_QK_B), BF16),
                 jax.ShapeDtypeStruct((rows, H_B * V_B), BF16)]
    out_specs = [pl.BlockSpec((bm, H_B * PAD_QK_B), lambda i: (i, 0)),
                 pl.BlockSpec((bm, H_B * V_B), lambda i: (i, 0))]
    if norm_in:
        out_shape.append(jax.ShapeDtypeStruct((rows, kc), F32))
        out_specs.append(pl.BlockSpec((bm, kc), lambda i: (i, 0)))
    return pl.pallas_call(
        functools.partial(_mla_kv_kernel, norm_in=norm_in, rope=rope),
        out_shape=out_shape,
        grid=(rows // bm,),
        in_specs=in_specs,
        out_specs=out_specs,
        scratch_shapes=[pltpu.VMEM((kc, n), BF16)],
        compiler_params=_cparams(1),
        name="mla_kv_proj",
    )(*args)


def _gqa_prep_kernel(q_ref, k_ref, v_ref, gq_ref, gk_ref, *rest, rope, emit_kn):
    rest = list(rest)
    tab_ref = rest.pop(0) if rope else None
    qo_ref, ko_ref, vo_ref = rest[:3]
    kn_ref = rest[3] if emit_kn else None
    scale = HD_C ** -0.5

    def norm(x, g_ref):
        return x * lax.rsqrt(jnp.mean(x * x, axis=-1, keepdims=True) + EPS) * g_ref[...]

    for h in range(HQ_C):
        x = norm(q_ref[:, h * HD_C:(h + 1) * HD_C].astype(F32), gq_ref)
        if rope:
            x = _rope(x, tab_ref)
        qo_ref[:, h * HD_C:(h + 1) * HD_C] = (x * scale).astype(qo_ref.dtype)
    for h in range(HKV_C):
        x = norm(k_ref[:, h * HD_C:(h + 1) * HD_C].astype(F32), gk_ref)
        if emit_kn:
            kn_ref[:, h * HD_C:(h + 1) * HD_C] = x
        if rope:
            x = _rope(x, tab_ref)
        ko_ref[:, h * HD_C:(h + 1) * HD_C] = x.astype(ko_ref.dtype)
    vo_ref[...] = v_ref[...].astype(vo_ref.dtype)


def gqa_prep(z, gq, gk, l, *, row0, rows, tab, emit_kn):
    bm = 512
    rope = tab is not None
    rb0 = row0 // bm
    nq, nk = HQ_C * HD_C, HKV_C * HD_C
    in_specs = [
        pl.BlockSpec((bm, nq), lambda i: (rb0 + i, Z_QC // nq)),
        pl.BlockSpec((bm, nk), lambda i: (rb0 + i, Z_KC // nk)),
        pl.BlockSpec((bm, nk), lambda i: (rb0 + i, Z_VC // nk)),
        pl.BlockSpec((None, 1, HD_C), lambda i: (l, 0, 0)),
        pl.BlockSpec((None, 1, HD_C), lambda i: (l, 0, 0)),
    ]
    args = [z, z, z, gq.reshape(DEPTH, 1, HD_C), gk.reshape(DEPTH, 1, HD_C)]
    if rope:
        per = DEC_SEQ // bm
        in_specs.append(pl.BlockSpec((3, bm, 128), lambda i: (0, i % per, 0)))
        args.append(tab)
    out_shape = [jax.ShapeDtypeStruct((rows, nq), BF16),
                 jax.ShapeDtypeStruct((rows, nk), BF16),
                 jax.ShapeDtypeStruct((rows, nk), BF16)]
    out_specs = [pl.BlockSpec((bm, nq), lambda i: (i, 0)),
                 pl.BlockSpec((bm, nk), lambda i: (i, 0)),
                 pl.BlockSpec((bm, nk), lambda i: (i, 0))]
    if emit_kn:
        out_shape.append(jax.ShapeDtypeStruct((rows, nk), F32))
        out_specs.append(pl.BlockSpec((bm, nk), lambda i: (i, 0)))
    return pl.pallas_call(
        functools.partial(_gqa_prep_kernel, rope=rope, emit_kn=emit_kn),
        out_shape=out_shape,
        grid=(rows // bm,),
        in_specs=in_specs,
        out_specs=out_specs,
        compiler_params=_cparams(1),
        name="gqa_prep",
    )(*args)


def _attn_kernel(*refs, n_seg, seg_rows, hq, hkv, dqk, dv, tq):
    q_ref = refs[0]
    kv_refs = refs[1:1 + 2 * n_seg]
    o_ref = refs[1 + 2 * n_seg]
    k_sc, v_sc = refs[2 + 2 * n_seg:]
    g = hq // hkv
    nq = q_ref.shape[0] // tq
    for j in range(hkv):
        r = 0
        for s in range(n_seg):
            k_sc[j, r:r + seg_rows[s], :] = kv_refs[2 * s][:, j * dqk:(j + 1) * dqk].astype(BF16)
            v_sc[j, r:r + seg_rows[s], :] = kv_refs[2 * s + 1][:, j * dv:(j + 1) * dv].astype(BF16)
            r += seg_rows[s]

    def q_block(qb, carry):
        rows = pl.ds(pl.multiple_of(qb * tq, tq), tq)
        for h in range(hq):
            j = h // g
            q = q_ref[rows, h * dqk:(h + 1) * dqk]
            s_ = _dot_nt(q, k_sc[j])
            m = jnp.max(s_, axis=-1, keepdims=True)
            p = jnp.exp(s_ - m)
            den = jnp.sum(p, axis=-1, keepdims=True)
            o = jnp.dot(p.astype(BF16), v_sc[j], preferred_element_type=F32) / den
            o_ref[rows, h * dv:(h + 1) * dv] = o.astype(o_ref.dtype)
        return carry

    if nq == 1:
        q_block(0, 0)
    else:
        lax.fori_loop(0, nq, q_block, 0)


def attention(q, segs, *, nb, q_rows, hq, hkv, dqk, dv, name):
    tq = 256
    in_specs = [pl.BlockSpec((q_rows, hq * dqk), lambda b: (b, 0))]
    args = [q]
    seg_rows = []
    for k_arr, k_spec, v_arr, v_spec, rows in segs:
        in_specs += [k_spec, v_spec]
        args += [k_arr, v_arr]
        seg_rows.append(rows)
    s_tot = sum(seg_rows)
    return pl.pallas_call(
        functools.partial(_attn_kernel, n_seg=len(segs), seg_rows=tuple(seg_rows), hq=hq, hkv=hkv,
                          dqk=dqk, dv=dv, tq=tq),
        out_shape=jax.ShapeDtypeStruct((nb * q_rows, hq * dv), BF16),
        grid=(nb,),
        in_specs=in_specs,
        out_specs=pl.BlockSpec((q_rows, hq * dv), lambda b: (b, 0)),
        scratch_shapes=[pltpu.VMEM((hkv, s_tot, dqk), BF16), pltpu.VMEM((hkv, s_tot, dv), BF16)],
        compiler_params=_cparams(1),
        name=name,
    )(*args)


def _rows_spec(rows, width, col_blk=0):
    return pl.BlockSpec((rows, width), lambda b: (b, col_blk))


def _gla_constants():
    c = GLA_CHUNK
    t = np.arange(c)[:, None]
    u = np.arange(c)[None, :]
    tril = (u <= t).astype(np.float32)
    hb = c // 2
    x = (t ^ u)[:hb, :hb]
    msb = np.where(x > 0, np.floor(np.log2(np.maximum(x, 1))), GLA_LEVELS - 1).astype(np.int32)
    return jnp.asarray(tril, BF16), jnp.asarray(msb)


def _split3(x):
    hi = x.astype(BF16)
    r = x - hi.astype(F32)
    mid = r.astype(BF16)
    lo = (r - mid.astype(F32)).astype(BF16)
    return [hi, mid, lo]


def _exp_neg_abs(x):
    return jnp.exp(-jnp.abs(x))


def _gla_kernel(q_ref, k_ref, v_ref, ga_ref, gk_ref, up_ref, bias_ref, ng_ref, tril_ref, msb_ref, *rest,
                n_chunks, has_s0):
    rest = list(rest)
    s0_ref = rest.pop(0) if has_s0 else None
    y_ref = rest.pop(0)
    st_ref = None if has_s0 else rest.pop(0)
    la_sc, cum_sc, kv_sc, tot_sc, sin_sc = rest
    c = GLA_CHUNK
    hb = c // 2
    nl = GLA_LEVELS
    dk = DK_A
    row = lax.broadcasted_iota(jnp.int32, (c, 1), 0)

    def pass1(ci, r0):
        rows = pl.ds(r0, c)
        gk = gk_ref[rows, :].astype(BF16)
        las = []
        for d in range(2):
            zg = jnp.dot(gk, up_ref[d].astype(BF16), preferred_element_type=F32) + bias_ref[d]
            las.append(jax.nn.log_sigmoid(zg) / GLA_GATE_NORM)
        parts = jnp.concatenate(_split3(las[0]) + _split3(las[1]), axis=1)
        cs = jnp.dot(tril_ref[...], parts, preferred_element_type=F32)
        cums = [cs[:, 0:dk] + cs[:, dk:2 * dk] + cs[:, 2 * dk:3 * dk],
                cs[:, 3 * dk:4 * dk] + cs[:, 4 * dk:5 * dk] + cs[:, 5 * dk:6 * dk]]
        for d in range(2):
            la_sc[d, rows, :] = las[d]
            cum_sc[d, rows, :] = cums[d]
        tot_f = cums[0][c - 1:c, :]
        tot_sc[ci] = jnp.concatenate([tot_f, cums[1][c - 1:c, :]], axis=1)
        k = k_ref[rows, :].astype(F32)
        kd = jnp.concatenate([k * _exp_neg_abs(tot_f - cums[0]), k * _exp_neg_abs(cums[1] - las[1])], axis=1)
        kv_sc[ci] = lax.dot_general(kd.astype(BF16), v_ref[rows, :].astype(BF16), (((0,), (0,)), ((), ())),
                                    preferred_element_type=F32)

    def pass2(ci, r0):
        rows = pl.ds(r0, c)
        laf, lab = la_sc[0, rows, :], la_sc[1, rows, :]
        cumf, cumb = cum_sc[0, rows, :], cum_sc[1, rows, :]
        cumbx = cumb - lab
        q = q_ref[rows, :].astype(F32) * (DK_A ** -0.5)
        k = k_ref[rows, :].astype(F32)
        v = v_ref[rows, :].astype(BF16)
        msb = msb_ref[...]
        q2 = (2.0 * q).astype(BF16)
        kb = k.astype(BF16)
        scd = [_dot_nt(q2[0:hb], kb[0:hb]), _dot_nt(q2[hb:], kb[hb:])]
        a_lo = a_up = None
        for lvl in range(nl):
            b = 1 << lvl
            second = (row & b) != 0
            if lvl == 0:
                qq = q * jnp.exp(jnp.where(second, laf, lab))
                kk = k
            elif lvl == 1:
                p = row & 3
                dq = jnp.where(p == 0, lab + pltpu.roll(lab, c - 1, 0),
                               jnp.where(p == 1, lab, jnp.where(p == 2, laf, laf + pltpu.roll(laf, 1, 0))))
                dkk = jnp.where(p == 0, pltpu.roll(laf, c - 1, 0), jnp.where(p == 3, pltpu.roll(lab, 1, 0), 0.0))
                qq = q * jnp.exp(dq)
                kk = k * jnp.exp(dkk)
            else:
                n = c // (2 * b)
                shp = (n, 2 * b, dk)
                cf3 = cumf.reshape(shp)
                cb3 = cumb.reshape(shp)
                ef = _exp_neg_abs(cf3 - cf3[:, b - 1:b, :]).reshape(c, dk)
                eb = _exp_neg_abs(cumbx.reshape(shp) - cb3[:, b - 1:b, :]).reshape(c, dk)
                qq = q * jnp.where(second, ef, eb)
                kk = k * jnp.where(second, eb, ef)
            qq = qq.astype(BF16)
            kk = kk.astype(BF16)
            if lvl < nl - 1:
                for blk in range(2):
                    a = _dot_nt(qq[blk * hb:(blk + 1) * hb], kk[blk * hb:(blk + 1) * hb])
                    scd[blk] = jnp.where(msb == lvl, a, scd[blk])
            else:
                a_lo = _dot_nt(qq[hb:], kk[:hb])
                a_up = _dot_nt(qq[:hb], kk[hb:])
        sc = jnp.concatenate([jnp.concatenate([scd[0], a_up], axis=1),
                              jnp.concatenate([a_lo, scd[1]], axis=1)], axis=0)
        o = jnp.dot(sc.astype(BF16), v, preferred_element_type=F32)
        if has_s0:
            tot_b = tot_sc[ci][:, dk:]
            qd = jnp.concatenate([q * _exp_neg_abs(cumf), q * _exp_neg_abs(tot_b - cumbx)], axis=1)
            o = o + jnp.dot(qd.astype(BF16), sin_sc[ci].astype(BF16), preferred_element_type=F32)
        o = o * lax.rsqrt(jnp.mean(o * o, axis=-1, keepdims=True) + EPS) * ng_ref[...]
        ga = ga_ref[rows, :].astype(F32)
        y_ref[rows, :] = (o * (ga * jax.nn.sigmoid(ga))).astype(y_ref.dtype)

    if n_chunks == 1:
        pass1(0, 0)
    else:
        lax.fori_loop(0, n_chunks, lambda i, carry: (pass1(i, pl.multiple_of(i * c, c)), carry)[1], 0)

    if has_s0:
        eye = lax.broadcasted_iota(jnp.int32, (2 * dk, 2 * dk), 0) == lax.broadcasted_iota(
            jnp.int32, (2 * dk, 2 * dk), 1)

        def decay_col(ci):
            tot = jnp.broadcast_to(tot_sc[ci], (2 * dk, 2 * dk))
            return jnp.exp(jnp.sum(jnp.where(eye, tot, 0.0), axis=1, keepdims=True))

        s = s0_ref[0]
        for ci in range(n_chunks):
            sin_sc[ci, 0:dk, :] = s
            if ci < n_chunks - 1:
                s = decay_col(ci)[0:dk] * s + kv_sc[ci, 0:dk, :]
        s = s0_ref[1]
        for ci in reversed(range(n_chunks)):
            sin_sc[ci, dk:2 * dk, :] = s
            if ci > 0:
                s = decay_col(ci)[dk:2 * dk] * s + kv_sc[ci, dk:2 * dk, :]
    else:
        st_ref[0] = kv_sc[0, 0:dk, :]
        st_ref[1] = kv_sc[0, dk:2 * dk, :]

    if n_chunks == 1:
        pass2(0, 0)
    else:
        lax.fori_loop(0, n_chunks, lambda i, carry: (pass2(i, pl.multiple_of(i * c, c)), carry)[1], 0)


def gla(z, up_pad, bias, norm_g, tril, msb, l, *, row0, nb, seq, s0):
    rb0 = row0 // seq
    n_chunks = seq // GLA_CHUNK
    has_s0 = s0 is not None
    assert has_s0 or n_chunks == 1
    hb = GLA_CHUNK // 2
    in_specs = [
        pl.BlockSpec((seq, DK_A), lambda b, h: (rb0 + b, Z_QA // DK_A + h)),
        pl.BlockSpec((seq, DK_A), lambda b, h: (rb0 + b, Z_KA // DK_A + h)),
        pl.BlockSpec((seq, DV_A), lambda b, h: (rb0 + b, Z_VA // DV_A + h)),
        pl.BlockSpec((seq, DV_A), lambda b, h: (rb0 + b, Z_GA // DV_A + h)),
        pl.BlockSpec((seq, 128), lambda b, h: (rb0 + b, Z_GK // 128)),
        pl.BlockSpec((None, 2, 128, DK_A), lambda b, h: (l, 0, 0, h)),
        pl.BlockSpec((None, 2, 1, DK_A), lambda b, h: (l, 0, 0, h)),
        pl.BlockSpec((None, 1, DV_A), lambda b, h: (l, 0, 0)),
        pl.BlockSpec((GLA_CHUNK, GLA_CHUNK), lambda b, h: (0, 0)),
        pl.BlockSpec((hb, hb), lambda b, h: (0, 0)),
    ]
    args = [z, z, z, z, z, up_pad, bias.reshape(DEPTH, 2, 1, H_A * DK_A),
            norm_g.reshape(DEPTH, 1, DV_A), tril, msb]
    out_shape = [jax.ShapeDtypeStruct((nb * seq, H_A * DV_A), BF16)]
    out_specs = [pl.BlockSpec((seq, DV_A), lambda b, h: (b, h))]
    if has_s0:
        in_specs.append(pl.BlockSpec((None, None, 2, None, DK_A, DV_A), lambda b, h: (b, l, 0, h, 0, 0)))
        args.append(s0)
    else:
        out_shape.append(jax.ShapeDtypeStruct((nb, 2, H_A, DK_A, DV_A), F32))
        out_specs.append(pl.BlockSpec((None, 2, None, DK_A, DV_A), lambda b, h: (b, 0, h, 0, 0)))
    return pl.pallas_call(
        functools.partial(_gla_kernel, n_chunks=n_chunks, has_s0=has_s0),
        out_shape=out_shape,
        grid=(nb, H_A),
        in_specs=in_specs,
        out_specs=out_specs,
        scratch_shapes=[pltpu.VMEM((2, seq, DK_A), F32), pltpu.VMEM((2, seq, DK_A), F32),
                        pltpu.VMEM((n_chunks, 2 * DK_A, DV_A), F32),
                        pltpu.VMEM((n_chunks, 1, 2 * DK_A), F32),
                        pltpu.VMEM((n_chunks, 2 * DK_A, DV_A), F32)],
        compiler_params=_cparams(2),
        name="gla",
    )(*args)


def kernel(x_prompt, x_sample, c, state_gla, cache_mla_ckv, cache_mla_krope, cache_gqa_k, cache_gqa_v,
           c_ctx, norm1_g, norm2_g, w_ada, b_ada, w_in, gla_gk_up, gla_gk_bias, gla_norm_g,
           mla_qa_norm_g, mla_w_qb, mla_kva_norm_g, mla_w_kvb, mla_q_norm_g, mla_k_norm_g,
           gqa_q_norm_g, gqa_k_norm_g, w_out_a, w_out_b, w_out_c, w_o,
           ffn_w_up, ffn_conv_w, ffn_conv_b, ffn_w_down):
    d = D_MODEL
    x = jnp.concatenate([x_prompt.reshape(T_P, d), x_sample.reshape(T_S, d)], axis=0)
    cvec = jnp.concatenate([c_ctx[None, :], c, jnp.zeros((MOD_ROWS - N_GROUPS, d), F32)], axis=0)
    mods = adaln_all(cvec, w_ada, b_ada)

    w_qb_pad = jnp.pad(mla_w_qb.reshape(DEPTH, Q_LORA_B, H_B, QK_B),
                       ((0, 0), (0, 0), (0, 0), (0, PAD_QK_B - QK_B))).reshape(DEPTH, Q_LORA_B, H_B * PAD_QK_B)
    q_g_pad = jnp.pad(mla_q_norm_g, ((0, 0), (0, PAD_QK_B - QK_B))).reshape(DEPTH, 1, PAD_QK_B)
    k_g_pad = jnp.pad(mla_k_norm_g, ((0, 0), (0, PAD_QK_B - QK_B))).reshape(DEPTH, 1, PAD_QK_B)
    up_pad = jnp.zeros((DEPTH, 2, 128, H_A * DK_A), F32)
    for dd in range(2):
        r0 = dd * GLA_GATE_RANK
        up_pad = up_pad.at[:, dd, r0:r0 + GLA_GATE_RANK, :].set(gla_gk_up[:, dd])
    tab_b = _rope_tables(DEC_SEQ, ROPE_B, 128)
    tab_c = _rope_tables(DEC_SEQ, HD_C, 128)
    tril, msb = _gla_constants()
    w_in_t = jnp.swapaxes(w_in, 1, 2)
    ck = cache_gqa_k.reshape(DEC_BATCH, DEPTH, PAST_LEN, HKV_C * HD_C)
    cv = cache_gqa_v.reshape(DEC_BATCH, DEPTH, PAST_LEN, HKV_C * HD_C)

    st_out, ckv_out, krope_out, kc_out, vc_out = [], [], [], [], []
    for l in range(DEPTH):
        h = norm_mod(x, norm1_g, mods, l, 0, 1)
        z = w_in_proj(h, w_in_t, l)

        ya_p, st = gla(z, up_pad, gla_gk_bias, gla_norm_g, tril, msb, l,
                       row0=0, nb=BATCH, seq=SEQ, s0=None)
        ya_s, = gla(z, up_pad, gla_gk_bias, gla_norm_g, tril, msb, l,
                    row0=T_P, nb=DEC_BATCH, seq=DEC_SEQ, s0=state_gla)

        bmk = 512
        qb_p = mla_q_proj(z, mla_qa_norm_g, w_qb_pad, q_g_pad, l, row0=0, rows=T_P, tab=None)
        qb_s = mla_q_proj(z, mla_qa_norm_g, w_qb_pad, q_g_pad, l, row0=T_P, rows=T_S, tab=tab_b)

        def tail_specs(row0):
            rb = row0 // bmk
            return (pl.BlockSpec((bmk, KV_LORA_B), lambda i: (rb + i, Z_CKV // KV_LORA_B)),
                    pl.BlockSpec((bmk, 128), lambda i: (rb + i, Z_KROPE // 128)))

        c_spec, kr_spec = tail_specs(0)
        kb_p, vb_p, ckv_p = mla_kv_proj(z, c_spec, z, kr_spec, mla_kva_norm_g, mla_w_kvb, k_g_pad, l,
                                        rows=T_P, bm=bmk, norm_in=True, tab=None)
        c_spec, kr_spec = tail_specs(T_P)
        kb_s, vb_s, _ = mla_kv_proj(z, c_spec, z, kr_spec, mla_kva_norm_g, mla_w_kvb, k_g_pad, l,
                                    rows=T_S, bm=bmk, norm_in=True, tab=tab_b)
        kb_c, vb_c = mla_kv_proj(
            cache_mla_ckv, pl.BlockSpec((None, None, PAST_LEN, KV_LORA_B), lambda i: (i, l, 0, 0)),
            cache_mla_krope, pl.BlockSpec((None, None, PAST_LEN, ROPE_B), lambda i: (i, l, 0, 0)),
            mla_kva_norm_g, mla_w_kvb, k_g_pad, l, rows=DEC_BATCH * PAST_LEN, bm=PAST_LEN, norm_in=False, tab=None)
        nkb, nvb = H_B * PAD_QK_B, H_B * V_B
        yb_p = attention(qb_p, [(kb_p, _rows_spec(SEQ, nkb), vb_p, _rows_spec(SEQ, nvb), SEQ)],
                         nb=BATCH, q_rows=SEQ, hq=H_B, hkv=H_B, dqk=PAD_QK_B, dv=V_B, name="mla_attn_p")
        yb_s = attention(qb_s, [(kb_s, _rows_spec(DEC_SEQ, nkb), vb_s, _rows_spec(DEC_SEQ, nvb), DEC_SEQ),
                                (kb_c, _rows_spec(PAST_LEN, nkb), vb_c, _rows_spec(PAST_LEN, nvb), PAST_LEN)],
                         nb=DEC_BATCH, q_rows=DEC_SEQ, hq=H_B, hkv=H_B, dqk=PAD_QK_B, dv=V_B, name="mla_attn_s")

        qc_p, kc_p, vc_p, kn_p = gqa_prep(z, gqa_q_norm_g, gqa_k_norm_g, l, row0=0, rows=T_P,
                                          tab=None, emit_kn=True)
        qc_s, kc_s, vc_s = gqa_prep(z, gqa_q_norm_g, gqa_k_norm_g, l, row0=T_P, rows=T_S,
                                    tab=tab_c, emit_kn=False)
        nkc = HKV_C * HD_C
        yc_p = attention(qc_p, [(kc_p, _rows_spec(SEQ, nkc), vc_p, _rows_spec(SEQ, nkc), SEQ)],
                         nb=BATCH, q_rows=SEQ, hq=HQ_C, hkv=HKV_C, dqk=HD_C, dv=HD_C, name="gqa_attn_p")
        cache_spec = pl.BlockSpec((None, None, PAST_LEN, nkc), lambda b: (b, l, 0, 0))
        yc_s = attention(qc_s, [(kc_s, _rows_spec(DEC_SEQ, nkc), vc_s, _rows_spec(DEC_SEQ, nkc), DEC_SEQ),
                                (ck, cache_spec, cv, cache_spec, PAST_LEN)],
                         nb=DEC_BATCH, q_rows=DEC_SEQ, hq=HQ_C, hkv=HKV_C, dqk=HD_C, dv=HD_C, name="gqa_attn_s")

        m = gated_merge((ya_p, ya_s, yb_p, yb_s, yc_p, yc_s), z, w_out_a, w_out_b, w_out_c, l)
        x, h2 = w_o_norm(m, w_o, l, x, mods, norm2_g)
        act = up_conv(h2, ffn_w_up, ffn_conv_w, ffn_conv_b, l)
        x = matmul_residual(act, ffn_w_down, l, x, mods, 5, bm=512, bn=512, name="ffn_down")

        st_out.append(st)
        ckv_out.append(ckv_p.reshape(BATCH, SEQ, KV_LORA_B))
        krope_out.append(z[:T_P, Z_KROPE:Z_KROPE + ROPE_B].astype(F32).reshape(BATCH, SEQ, ROPE_B))
        kc_out.append(kn_p.reshape(BATCH, SEQ, HKV_C, HD_C))
        vc_out.append(z[:T_P, Z_VC:Z_VC + nkc].astype(F32).reshape(BATCH, SEQ, HKV_C, HD_C))

    y_p = x[:T_P].reshape(BATCH, SEQ, d)
    y_s = x[T_P:].reshape(DEC_BATCH, DEC_SEQ, d)
    return (y_p, y_s, jnp.stack(st_out, axis=1), jnp.stack(ckv_out, axis=1), jnp.stack(krope_out, axis=1),
            jnp.stack(kc_out, axis=1), jnp.stack(vc_out, axis=1))
```

```python
import functools

import numpy as np
import jax
import jax.numpy as jnp
from jax import lax
from jax.experimental import pallas as pl
from jax.experimental.pallas import tpu as pltpu

F32 = jnp.float32
BF16 = jnp.bfloat16

D_MODEL = 2048
BATCH = 16
SEQ = 256
DEPTH = 4
DEC_BATCH = 2
DEC_SEQ = 1024
PAST_LEN = 512
GRID_W = 64
ROPE_THETA = 10000.0
EPS = 1e-6
H_A, DK_A, DV_A = 4, 128, 256
GLA_GATE_RANK = 16
GLA_GATE_NORM = 16.0
H_B, NOPE_B, ROPE_B, V_B = 8, 128, 64, 128
QK_B = NOPE_B + ROPE_B
Q_LORA_B = 512
KV_LORA_B = 256
HQ_C, HKV_C, HD_C = 8, 2, 128
D_FF = 5632

T_P = BATCH * SEQ
T_S = DEC_BATCH * DEC_SEQ
T_ALL = T_P + T_S
N_GROUPS = 1 + DEC_BATCH
SUBLANES = 8
MOD_ROWS = SUBLANES

_O_QA, _O_KA, _O_VA, _O_GA = 0, 512, 1024, 2048
_O_GKA = 3072
_O_QLORA = 3104
_O_KVA = 3616
_O_QC, _O_KC, _O_VC = 3936, 4960, 5216
_O_GATES = 5472
N_IN = 11616
W_IN_BN = 512
W_IN_RANGES = ((_O_QC, 15), (_O_GKA, 1), (_O_QA, 6), (_O_QLORA, 2))
N_Z = W_IN_BN * sum(nb for _, nb in W_IN_RANGES)
Z_QC, Z_KC, Z_VC, Z_GATES = 0, 1024, 1280, 1536
Z_GK = 7680
Z_QA, Z_KA, Z_VA, Z_GA = 8192, 8704, 9216, 10240
Z_QLORA, Z_CKV, Z_KROPE = 11264, 11776, 12032
PAD_QK_B = 256

GLA_CHUNK = 256
GLA_LEVELS = 8

VMEM_LIMIT = 56 * 1024 * 1024


def _cparams(n_axes):
    return pltpu.CompilerParams(
        dimension_semantics=("arbitrary",) * n_axes, vmem_limit_bytes=VMEM_LIMIT)


def _group_of_block(i, bm):
    n_p = T_P // bm
    per = DEC_SEQ // bm
    return jnp.where(i < n_p, 0, 1 + (i - n_p) // per)


def _adaln_kernel(c_ref, w_ref, b_ref, o_ref):
    c = c_ref[...]
    a = (c * jax.nn.sigmoid(c)).astype(BF16)
    o_ref[...] = jnp.dot(a, w_ref[...].astype(BF16), preferred_element_type=F32) + b_ref[...]


def adaln_all(cvec, w_ada, b_ada):
    bn = 1024
    n = 6 * D_MODEL
    return pl.pallas_call(
        _adaln_kernel,
        out_shape=jax.ShapeDtypeStruct((DEPTH, MOD_ROWS, n), F32),
        grid=(DEPTH, n // bn),
        in_specs=[
            pl.BlockSpec((MOD_ROWS, D_MODEL), lambda l, j: (0, 0)),
            pl.BlockSpec((None, D_MODEL, bn), lambda l, j: (l, 0, j)),
            pl.BlockSpec((None, 1, bn), lambda l, j: (l, 0, j)),
        ],
        out_specs=pl.BlockSpec((None, MOD_ROWS, bn), lambda l, j: (l, 0, j)),
        compiler_params=_cparams(2),
        name="adaln",
    )(cvec, w_ada, b_ada.reshape(DEPTH, 1, n))


def _norm_mod_kernel(x_ref, g_ref, sh_ref, sc_ref, o_ref, *, bm):
    grp = _group_of_block(pl.program_id(0), bm)
    x = x_ref[...]
    y = x * lax.rsqrt(jnp.mean(x * x, axis=-1, keepdims=True) + EPS) * g_ref[...]
    sh = sh_ref[pl.ds(grp, 1), :]
    sc = sc_ref[pl.ds(grp, 1), :]
    o_ref[...] = (y * (1.0 + sc) + sh).astype(o_ref.dtype)


def norm_mod(x, g, mods, l, which_shift, which_scale):
    bm = 512
    nd = D_MODEL
    return pl.pallas_call(
        functools.partial(_norm_mod_kernel, bm=bm),
        out_shape=jax.ShapeDtypeStruct((T_ALL, nd), BF16),
        grid=(T_ALL // bm,),
        in_specs=[
            pl.BlockSpec((bm, nd), lambda i: (i, 0)),
            pl.BlockSpec((None, 1, nd), lambda i: (l, 0, 0)),
            pl.BlockSpec((None, MOD_ROWS, nd), lambda i: (l, 0, which_shift)),
            pl.BlockSpec((None, MOD_ROWS, nd), lambda i: (l, 0, which_scale)),
        ],
        out_specs=pl.BlockSpec((bm, nd), lambda i: (i, 0)),
        compiler_params=_cparams(1),
        name="norm_mod",
    )(x, g.reshape(DEPTH, 1, nd), mods, mods)


def _dot_nt(a, b):
    return lax.dot_general(a, b, (((1,), (1,)), ((), ())), preferred_element_type=F32)


ROW_BLK = 1024


def _w_in_kernel(h_ref, wt_ref, o_ref, wb_ref):
    wb_ref[...] = wt_ref[0].T.astype(BF16)

    def body(r, carry):
        rows = pl.ds(pl.multiple_of(r * ROW_BLK, ROW_BLK), ROW_BLK)
        o_ref[rows, :] = jnp.dot(h_ref[rows, :], wb_ref[...], preferred_element_type=F32).astype(o_ref.dtype)
        return carry

    lax.fori_loop(0, T_ALL // ROW_BLK, body, 0)


def _w_in_first_col(j):
    col8 = None
    start = sum(nb for _, nb in W_IN_RANGES)
    for c0, nb in reversed(W_IN_RANGES):
        assert c0 % 8 == 0
        start -= nb
        here = c0 // 8 + (j - start) * (W_IN_BN // 8)
        col8 = here if col8 is None else jnp.where(j < start + nb, here, col8)
    return col8 * 8


def w_in_proj(h, w_in_t, l):
    bn = W_IN_BN
    return pl.pallas_call(
        _w_in_kernel,
        out_shape=jax.ShapeDtypeStruct((T_ALL, N_Z), BF16),
        grid=(N_Z // bn,),
        in_specs=[
            pl.BlockSpec((T_ALL, D_MODEL), lambda j: (0, 0), pipeline_mode=pl.Buffered(1)),
            pl.BlockSpec((pl.Element(1), pl.Element(bn), pl.Element(D_MODEL)),
                         lambda j: (l, _w_in_first_col(j), 0)),
        ],
        out_specs=pl.BlockSpec((T_ALL, bn), lambda j: (0, j)),
        scratch_shapes=[pltpu.VMEM((D_MODEL, bn), BF16)],
        compiler_params=_cparams(1),
        name="w_in",
    )(h, w_in_t)


def _mm_res_kernel(a_ref, w_ref, x_ref, gate_ref, o_ref, wb_ref, *, bm):
    @pl.when(pl.program_id(1) == 0)
    def _():
        wb_ref[...] = w_ref[...].astype(BF16)

    grp = _group_of_block(pl.program_id(1), bm)
    acc = jnp.dot(a_ref[...], wb_ref[...], preferred_element_type=F32)
    o_ref[...] = x_ref[...] + gate_ref[pl.ds(grp, 1), :] * acc


def matmul_residual(a, w, l, x, mods, which_gate, *, bm, bn, name):
    m, k = a.shape
    n = w.shape[-1]
    gate_blk = which_gate * (D_MODEL // bn)
    return pl.pallas_call(
        functools.partial(_mm_res_kernel, bm=bm),
        out_shape=jax.ShapeDtypeStruct((m, n), F32),
        grid=(n // bn, m // bm),
        in_specs=[
            pl.BlockSpec((bm, k), lambda j, i: (i, 0)),
            pl.BlockSpec((None, k, bn), lambda j, i: (l, 0, j)),
            pl.BlockSpec((bm, bn), lambda j, i: (i, j)),
            pl.BlockSpec((None, MOD_ROWS, bn), lambda j, i: (l, 0, gate_blk + j)),
        ],
        out_specs=pl.BlockSpec((bm, bn), lambda j, i: (i, j)),
        scratch_shapes=[pltpu.VMEM((k, bn), BF16)],
        compiler_params=_cparams(2),
        name=name,
    )(a, w, x, mods)


def _w_o_norm_kernel(m_ref, w_ref, x_ref, gate_ref, g_ref, sh_ref, sc_ref, xo_ref, h_ref, wb_ref, *, bm):
    i = pl.program_id(0)

    @pl.when(i == 0)
    def _():
        wb_ref[...] = w_ref[...].astype(BF16)

    grp = _group_of_block(i, bm)
    acc = jnp.dot(m_ref[...], wb_ref[...], preferred_element_type=F32)
    x = x_ref[...] + gate_ref[pl.ds(grp, 1), :] * acc
    xo_ref[...] = x
    y = x * lax.rsqrt(jnp.mean(x * x, axis=-1, keepdims=True) + EPS) * g_ref[...]
    h_ref[...] = (y * (1.0 + sc_ref[pl.ds(grp, 1), :]) + sh_ref[pl.ds(grp, 1), :]).astype(h_ref.dtype)


def w_o_norm(m, w_o, l, x, mods, norm_g):
    bm = 512
    nd = D_MODEL

    def mod_spec(which):
        return pl.BlockSpec((None, MOD_ROWS, nd), lambda i: (l, 0, which))

    return pl.pallas_call(
        functools.partial(_w_o_norm_kernel, bm=bm),
        out_shape=[jax.ShapeDtypeStruct((T_ALL, nd), F32), jax.ShapeDtypeStruct((T_ALL, nd), BF16)],
        grid=(T_ALL // bm,),
        in_specs=[
            pl.BlockSpec((bm, nd), lambda i: (i, 0)),
            pl.BlockSpec((None, nd, nd), lambda i: (l, 0, 0), pipeline_mode=pl.Buffered(1)),
            pl.BlockSpec((bm, nd), lambda i: (i, 0)),
            mod_spec(2),
            pl.BlockSpec((None, 1, nd), lambda i: (l, 0, 0)),
            mod_spec(3), mod_spec(4),
        ],
        out_specs=[pl.BlockSpec((bm, nd), lambda i: (i, 0)), pl.BlockSpec((bm, nd), lambda i: (i, 0))],
        scratch_shapes=[pltpu.VMEM((nd, nd), BF16)],
        compiler_params=_cparams(1),
        name="w_o_norm",
    )(m, w_o, x, mods, norm_g.reshape(DEPTH, 1, nd), mods, mods)


def _merge_kernel(yap_ref, yas_ref, ybp_ref, ybs_ref, ycp_ref, ycs_ref, wa_ref, wb_ref, wc_ref,
                  ga_ref, gb_ref, gc_ref, o_ref, wab_ref, wbb_ref, wcb_ref, *, bm):
    i = pl.program_id(1)

    @pl.when(i == 0)
    def _():
        wab_ref[...] = wa_ref[...].astype(BF16)
        wbb_ref[...] = wb_ref[...].astype(BF16)
        wcb_ref[...] = wc_ref[...].astype(BF16)

    is_ctx = i < T_P // bm

    def branch(yp_ref, ys_ref, w_ref, g_ref):
        y = jnp.where(is_ctx, yp_ref[...], ys_ref[...])
        return jax.nn.sigmoid(g_ref[...].astype(F32)) * jnp.dot(y, w_ref[...], preferred_element_type=F32)

    m = (branch(yap_ref, yas_ref, wab_ref, ga_ref) + branch(ybp_ref, ybs_ref, wbb_ref, gb_ref)
         + branch(ycp_ref, ycs_ref, wcb_ref, gc_ref))
    o_ref[...] = m.astype(o_ref.dtype)


def gated_merge(ys, z, w_out_a, w_out_b, w_out_c, l):
    bm, bn = 512, 512
    k = 1024
    gate0 = Z_GATES // bn
    per = D_MODEL // bn
    n_p = T_P // bm
    yp_spec = pl.BlockSpec((bm, k), lambda j, i: (jnp.minimum(i, n_p - 1), 0))
    ys_spec = pl.BlockSpec((bm, k), lambda j, i: (jnp.maximum(i - n_p, 0), 0))
    w_spec = pl.BlockSpec((None, k, bn), lambda j, i: (l, 0, j))

    def g_spec(br):
        return pl.BlockSpec((bm, bn), lambda j, i: (i, gate0 + br * per + j))

    return pl.pallas_call(
        functools.partial(_merge_kernel, bm=bm),
        out_shape=jax.ShapeDtypeStruct((T_ALL, D_MODEL), BF16),
        grid=(D_MODEL // bn, T_ALL // bm),
        in_specs=[yp_spec, ys_spec] * 3 + [w_spec] * 3 + [g_spec(0), g_spec(1), g_spec(2)],
        out_specs=pl.BlockSpec((bm, bn), lambda j, i: (i, j)),
        scratch_shapes=[pltpu.VMEM((k, bn), BF16)] * 3,
        compiler_params=_cparams(2),
        name="gated_merge",
    )(*ys, w_out_a, w_out_b, w_out_c, z, z, z)


def _up_conv_kernel(h_ref, wa_ref, wg_ref, cwa_ref, cwg_ref, cba_ref, cbg_ref, o_ref, wab_ref, wgb_ref,
                    *sc_refs):
    wab_ref[...] = wa_ref[...].astype(BF16)
    wgb_ref[...] = wg_ref[...].astype(BF16)
    bm = ROW_BLK
    nt = bm // SUBLANES
    bn = o_ref.shape[1]
    sub = lax.broadcasted_iota(jnp.int32, (1, SUBLANES, 1), 1)
    for sc_ref in sc_refs:
        sc_ref[0:SUBLANES, :] = jnp.zeros((SUBLANES, bn), F32)
        sc_ref[SUBLANES + bm:2 * SUBLANES + bm, :] = jnp.zeros((SUBLANES, bn), F32)

    def taps(cw_ref, cb_ref):
        return [jnp.broadcast_to(cw_ref[k:k + 1, :], (SUBLANES, bn))[None] for k in range(3)] + [
            jnp.broadcast_to(cb_ref[...], (SUBLANES, bn))[None]]

    taps_a = taps(cwa_ref, cba_ref)
    taps_g = taps(cwg_ref, cbg_ref)

    for r in range(T_ALL // bm):
        seq = SEQ if r < T_P // bm else DEC_SEQ
        tps = seq // SUBLANES
        n_seq = bm // seq
        h = h_ref[r * bm:(r + 1) * bm, :]

        def conv(w_ref, tp, sc_ref):
            u = jnp.dot(h, w_ref[...], preferred_element_type=F32)
            sc_ref[SUBLANES:SUBLANES + bm, :] = u
            prev = sc_ref[SUBLANES - 1:SUBLANES - 1 + bm, :].reshape(nt, SUBLANES, bn)
            nxt = sc_ref[SUBLANES + 1:SUBLANES + 1 + bm, :].reshape(nt, SUBLANES, bn)
            if n_seq > 1:
                prev = jnp.concatenate(
                    [p for s in range(n_seq) for p in (
                        jnp.where(sub == 0, 0.0, prev[s * tps:s * tps + 1]), prev[s * tps + 1:(s + 1) * tps])],
                    axis=0)
                nxt = jnp.concatenate(
                    [p for s in range(n_seq) for p in (
                        nxt[s * tps:(s + 1) * tps - 1],
                        jnp.where(sub == SUBLANES - 1, 0.0, nxt[(s + 1) * tps - 1:(s + 1) * tps]))],
                    axis=0)
            return prev * tp[0] + u.reshape(nt, SUBLANES, bn) * tp[1] + nxt * tp[2] + tp[3]

        a = conv(wab_ref, taps_a, sc_refs[0])
        g = conv(wgb_ref, taps_g, sc_refs[1])
        act = (g * jax.nn.sigmoid(g) * a).reshape(bm, bn)
        o_ref[r * bm:(r + 1) * bm, :] = act.astype(o_ref.dtype)


def up_conv(h2, w_up, conv_w, conv_b, l):
    bn = 256
    nb = D_FF // bn
    cb = conv_b.reshape(DEPTH, 1, 2 * D_FF)
    return pl.pallas_call(
        _up_conv_kernel,
        out_shape=jax.ShapeDtypeStruct((T_ALL, D_FF), BF16),
        grid=(nb,),
        in_specs=[
            pl.BlockSpec((T_ALL, D_MODEL), lambda j: (0, 0), pipeline_mode=pl.Buffered(1)),
            pl.BlockSpec((None, D_MODEL, bn), lambda j: (l, 0, j)),
            pl.BlockSpec((None, D_MODEL, bn), lambda j: (l, 0, nb + j)),
            pl.BlockSpec((None, 3, bn), lambda j: (l, 0, j)),
            pl.BlockSpec((None, 3, bn), lambda j: (l, 0, nb + j)),
            pl.BlockSpec((None, 1, bn), lambda j: (l, 0, j)),
            pl.BlockSpec((None, 1, bn), lambda j: (l, 0, nb + j)),
        ],
        out_specs=pl.BlockSpec((T_ALL, bn), lambda j: (0, j)),
        scratch_shapes=[pltpu.VMEM((D_MODEL, bn), BF16)] * 2
        + [pltpu.VMEM((ROW_BLK + 2 * SUBLANES, bn), F32)] * 2,
        compiler_params=_cparams(1),
        name="up_conv",
    )(h2, w_up, w_up, conv_w, conv_w, cb, cb)


def _rope_tables(n_tok, rot_dim, width):
    rows = n_tok // GRID_W
    row = jnp.repeat(jnp.arange(rows, dtype=F32), GRID_W)
    col = jnp.tile(jnp.arange(GRID_W, dtype=F32), rows)
    half = rot_dim // 2
    inv = jnp.power(ROPE_THETA, -jnp.arange(half // 2, dtype=F32) * (2.0 / half))
    ang = jnp.concatenate([row[:, None] * inv, col[:, None] * inv], axis=-1)
    cos, sin = jnp.cos(ang), jnp.sin(ang)
    zero = jnp.zeros_like(sin)
    c2 = jnp.stack([cos, cos], axis=-1).reshape(n_tok, rot_dim)
    s_even = jnp.stack([-sin, zero], axis=-1).reshape(n_tok, rot_dim)
    s_odd = jnp.stack([zero, sin], axis=-1).reshape(n_tok, rot_dim)
    pad = ((0, 0), (0, width - rot_dim))
    return jnp.stack([jnp.pad(c2, pad), jnp.pad(s_even, pad), jnp.pad(s_odd, pad)], axis=0)


def _rope(x, tab_ref):
    w = x.shape[-1]
    return (x * tab_ref[0] + pltpu.roll(x, w - 1, 1) * tab_ref[1] + pltpu.roll(x, 1, 1) * tab_ref[2])


def _mla_q_kernel(ql_ref, g1_ref, w_ref, g2_ref, *rest, rope):
    if rope:
        tab_ref, o_ref, wb_ref = rest
    else:
        o_ref, wb_ref = rest

    @pl.when(pl.program_id(0) == 0)
    def _():
        wb_ref[...] = w_ref[...].astype(BF16)

    x = ql_ref[...].astype(F32)
    xn = x * lax.rsqrt(jnp.mean(x * x, axis=-1, keepdims=True) + EPS) * g1_ref[...]
    q = jnp.dot(xn.astype(BF16), wb_ref[...], preferred_element_type=F32)
    scale = QK_B ** -0.5
    for h in range(H_B):
        lo = h * PAD_QK_B
        qn = q[:, lo:lo + NOPE_B]
        qr = q[:, lo + NOPE_B:lo + PAD_QK_B]
        ms = (jnp.sum(qn * qn, axis=-1, keepdims=True) + jnp.sum(qr * qr, axis=-1, keepdims=True)) / QK_B
        r = lax.rsqrt(ms + EPS) * scale
        qn = qn * r * g2_ref[:, 0:NOPE_B]
        qr = qr * r * g2_ref[:, NOPE_B:PAD_QK_B]
        if rope:
            qr = _rope(qr, tab_ref)
        o_ref[:, lo:lo + NOPE_B] = qn.astype(o_ref.dtype)
        o_ref[:, lo + NOPE_B:lo + PAD_QK_B] = qr.astype(o_ref.dtype)


def mla_q_proj(z, qa_g, w_qb_pad, q_g_pad, l, *, row0, rows, tab):
    bm = 512
    rope = tab is not None
    kq = Q_LORA_B
    n = H_B * PAD_QK_B
    rb0 = row0 // bm
    in_specs = [
        pl.BlockSpec((bm, kq), lambda i: (rb0 + i, Z_QLORA // kq)),
        pl.BlockSpec((None, 1, kq), lambda i: (l, 0, 0)),
        pl.BlockSpec((None, kq, n), lambda i: (l, 0, 0)),
        pl.BlockSpec((None, 1, PAD_QK_B), lambda i: (l, 0, 0)),
    ]
    args = [z, qa_g.reshape(DEPTH, 1, kq), w_qb_pad, q_g_pad]
    if rope:
        per = DEC_SEQ // bm
        in_specs.append(pl.BlockSpec((3, bm, 128), lambda i: (0, i % per, 0)))
        args.append(tab)
    return pl.pallas_call(
        functools.partial(_mla_q_kernel, rope=rope),
        out_shape=jax.ShapeDtypeStruct((rows, n), BF16),
        grid=(rows // bm,),
        in_specs=in_specs,
        out_specs=pl.BlockSpec((bm, n), lambda i: (i, 0)),
        scratch_shapes=[pltpu.VMEM((kq, n), BF16)],
        compiler_params=_cparams(1),
        name="mla_q_proj",
    )(*args)


def _mla_kv_kernel(c_ref, kr_ref, g1_ref, w_ref, g2_ref, *rest, norm_in, rope):
    rest = list(rest)
    tab_ref = rest.pop(0) if rope else None
    if norm_in:
        k_ref, v_ref, ckv_ref, wb_ref = rest
    else:
        k_ref, v_ref, wb_ref = rest

    @pl.when(pl.program_id(0) == 0)
    def _():
        wb_ref[...] = w_ref[...].astype(BF16)

    c = c_ref[...].astype(F32)
    if norm_in:
        c = c * lax.rsqrt(jnp.mean(c * c, axis=-1, keepdims=True) + EPS) * g1_ref[...]
        ckv_ref[...] = c
    kv = jnp.dot(c.astype(BF16), wb_ref[...], preferred_element_type=F32)
    kr = kr_ref[...].astype(F32)
    if kr.shape[-1] == 128:
        lane = lax.broadcasted_iota(jnp.int32, kr.shape, 1)
        kr = jnp.where(lane < ROPE_B, kr, 0.0)
    else:
        kr = jnp.concatenate([kr, jnp.zeros_like(kr)], axis=-1)
    kr_ss = jnp.sum(kr * kr, axis=-1, keepdims=True)
    for h in range(H_B):
        kn = kv[:, h * 256:h * 256 + NOPE_B]
        v = kv[:, h * 256 + NOPE_B:(h + 1) * 256]
        r = lax.rsqrt((jnp.sum(kn * kn, axis=-1, keepdims=True) + kr_ss) / QK_B + EPS)
        krh = kr * r * g2_ref[:, NOPE_B:PAD_QK_B]
        if rope:
            krh = _rope(krh, tab_ref)
        k_ref[:, h * PAD_QK_B:h * PAD_QK_B + NOPE_B] = (kn * r * g2_ref[:, 0:NOPE_B]).astype(k_ref.dtype)
        k_ref[:, h * PAD_QK_B + NOPE_B:(h + 1) * PAD_QK_B] = krh.astype(k_ref.dtype)
        v_ref[:, h * V_B:(h + 1) * V_B] = v.astype(v_ref.dtype)


def mla_kv_proj(c_arr, c_spec, kr_arr, kr_spec, kva_g, w_kvb, k_g_pad, l, *, rows, bm, norm_in, tab):
    rope = tab is not None
    kc = KV_LORA_B
    n = H_B * (NOPE_B + V_B)
    in_specs = [
        c_spec, kr_spec,
        pl.BlockSpec((None, 1, kc), lambda i: (l, 0, 0)),
        pl.BlockSpec((None, kc, n), lambda i: (l, 0, 0)),
        pl.BlockSpec((None, 1, PAD_QK_B), lambda i: (l, 0, 0)),
    ]
    args = [c_arr, kr_arr, kva_g.reshape(DEPTH, 1, kc), w_kvb, k_g_pad]
    if rope:
        per = DEC_SEQ // bm
        in_specs.append(pl.BlockSpec((3, bm, 128), lambda i: (0, i % per, 0)))
        args.append(tab)
    out_shape = [jax.ShapeDtypeStruct((rows, H_B * PAD_QK_B), BF16),
                 jax.ShapeDtypeStruct((rows, H_B * V_B), BF16)]
    out_specs = [pl.BlockSpec((bm, H_B * PAD_QK_B), lambda i: (i, 0)),
                 pl.BlockSpec((bm, H_B * V_B), lambda i: (i, 0))]
    if norm_in:
        out_shape.append(jax.ShapeDtypeStruct((rows, kc), F32))
        out_specs.append(pl.BlockSpec((bm, kc), lambda i: (i, 0)))
    return pl.pallas_call(
        functools.partial(_mla_kv_kernel, norm_in=norm_in, rope=rope),
        out_shape=out_shape,
        grid=(rows // bm,),
        in_specs=in_specs,
        out_specs=out_specs,
        scratch_shapes=[pltpu.VMEM((kc, n), BF16)],
        compiler_params=_cparams(1),
        name="mla_kv_proj",
    )(*args)


def _gqa_prep_kernel(q_ref, k_ref, v_ref, gq_ref, gk_ref, *rest, rope, emit_kn):
    rest = list(rest)
    tab_ref = rest.pop(0) if rope else None
    qo_ref, ko_ref, vo_ref = rest[:3]
    kn_ref = rest[3] if emit_kn else None
    scale = HD_C ** -0.5

    def norm(x, g_ref):
        return x * lax.rsqrt(jnp.mean(x * x, axis=-1, keepdims=True) + EPS) * g_ref[...]

    for h in range(HQ_C):
        x = norm(q_ref[:, h * HD_C:(h + 1) * HD_C].astype(F32), gq_ref)
        if rope:
            x = _rope(x, tab_ref)
        qo_ref[:, h * HD_C:(h + 1) * HD_C] = (x * scale).astype(qo_ref.dtype)
    for h in range(HKV_C):
        x = norm(k_ref[:, h * HD_C:(h + 1) * HD_C].astype(F32), gk_ref)
        if emit_kn:
            kn_ref[:, h * HD_C:(h + 1) * HD_C] = x
        if rope:
            x = _rope(x, tab_ref)
        ko_ref[:, h * HD_C:(h + 1) * HD_C] = x.astype(ko_ref.dtype)
    vo_ref[...] = v_ref[...].astype(vo_ref.dtype)


def gqa_prep(z, gq, gk, l, *, row0, rows, tab, emit_kn):
    bm = 512
    rope = tab is not None
    rb0 = row0 // bm
    nq, nk = HQ_C * HD_C, HKV_C * HD_C
    in_specs = [
        pl.BlockSpec((bm, nq), lambda i: (rb0 + i, Z_QC // nq)),
        pl.BlockSpec((bm, nk), lambda i: (rb0 + i, Z_KC // nk)),
        pl.BlockSpec((bm, nk), lambda i: (rb0 + i, Z_VC // nk)),
        pl.BlockSpec((None, 1, HD_C), lambda i: (l, 0, 0)),
        pl.BlockSpec((None, 1, HD_C), lambda i: (l, 0, 0)),
    ]
    args = [z, z, z, gq.reshape(DEPTH, 1, HD_C), gk.reshape(DEPTH, 1, HD_C)]
    if rope:
        per = DEC_SEQ // bm
        in_specs.append(pl.BlockSpec((3, bm, 128), lambda i: (0, i % per, 0)))
        args.append(tab)
    out_shape = [jax.ShapeDtypeStruct((rows, nq), BF16),
                 jax.ShapeDtypeStruct((rows, nk), BF16),
                 jax.ShapeDtypeStruct((rows, nk), BF16)]
    out_specs = [pl.BlockSpec((bm, nq), lambda i: (i, 0)),
                 pl.BlockSpec((bm, nk), lambda i: (i, 0)),
                 pl.BlockSpec((bm, nk), lambda i: (i, 0))]
    if emit_kn:
        out_shape.append(jax.ShapeDtypeStruct((rows, nk), F32))
        out_specs.append(pl.BlockSpec((bm, nk), lambda i: (i, 0)))
    return pl.pallas_call(
        functools.partial(_gqa_prep_kernel, rope=rope, emit_kn=emit_kn),
        out_shape=out_shape,
        grid=(rows // bm,),
        in_specs=in_specs,
        out_specs=out_specs,
        compiler_params=_cparams(1),
        name="gqa_prep",
    )(*args)


def _attn_kernel(*refs, n_seg, seg_rows, hq, hkv, dqk, dv, tq):
    q_ref = refs[0]
    kv_refs = refs[1:1 + 2 * n_seg]
    o_ref = refs[1 + 2 * n_seg]
    k_sc, v_sc = refs[2 + 2 * n_seg:]
    g = hq // hkv
    nq = q_ref.shape[0] // tq
    for j in range(hkv):
        r = 0
        for s in range(n_seg):
            k_sc[j, r:r + seg_rows[s], :] = kv_refs[2 * s][:, j * dqk:(j + 1) * dqk].astype(BF16)
            v_sc[j, r:r + seg_rows[s], :] = kv_refs[2 * s + 1][:, j * dv:(j + 1) * dv].astype(BF16)
            r += seg_rows[s]

    def q_block(qb, carry):
        rows = pl.ds(pl.multiple_of(qb * tq, tq), tq)
        for h in range(hq):
            j = h // g
            q = q_ref[rows, h * dqk:(h + 1) * dqk]
            s_ = _dot_nt(q, k_sc[j])
            m = jnp.max(s_, axis=-1, keepdims=True)
            p = jnp.exp(s_ - m)
            den = jnp.sum(p, axis=-1, keepdims=True)
            o = jnp.dot(p.astype(BF16), v_sc[j], preferred_element_type=F32) / den
            o_ref[rows, h * dv:(h + 1) * dv] = o.astype(o_ref.dtype)
        return carry

    if nq == 1:
        q_block(0, 0)
    else:
        lax.fori_loop(0, nq, q_block, 0)


def attention(q, segs, *, nb, q_rows, hq, hkv, dqk, dv, name):
    tq = 256
    in_specs = [pl.BlockSpec((q_rows, hq * dqk), lambda b: (b, 0))]
    args = [q]
    seg_rows = []
    for k_arr, k_spec, v_arr, v_spec, rows in segs:
        in_specs += [k_spec, v_spec]
        args += [k_arr, v_arr]
        seg_rows.append(rows)
    s_tot = sum(seg_rows)
    return pl.pallas_call(
        functools.partial(_attn_kernel, n_seg=len(segs), seg_rows=tuple(seg_rows), hq=hq, hkv=hkv,
                          dqk=dqk, dv=dv, tq=tq),
        out_shape=jax.ShapeDtypeStruct((nb * q_rows, hq * dv), BF16),
        grid=(nb,),
        in_specs=in_specs,
        out_specs=pl.BlockSpec((q_rows, hq * dv), lambda b: (b, 0)),
        scratch_shapes=[pltpu.VMEM((hkv, s_tot, dqk), BF16), pltpu.VMEM((hkv, s_tot, dv), BF16)],
        compiler_params=_cparams(1),
        name=name,
    )(*args)


def _rows_spec(rows, width, col_blk=0):
    return pl.BlockSpec((rows, width), lambda b: (b, col_blk))


def _gla_constants():
    c = GLA_CHUNK
    t = np.arange(c)[:, None]
    u = np.arange(c)[None, :]
    tril = (u <= t).astype(np.float32)
    hb = c // 2
    x = (t ^ u)[:hb, :hb]
    msb = np.where(x > 0, np.floor(np.log2(np.maximum(x, 1))), GLA_LEVELS - 1).astype(np.int32)
    return jnp.asarray(tril, BF16), jnp.asarray(msb)


def _split3(x):
    hi = x.astype(BF16)
    r = x - hi.astype(F32)
    mid = r.astype(BF16)
    lo = (r - mid.astype(F32)).astype(BF16)
    return [hi, mid, lo]


LOG2E = 1.4426950408889634


def _exp2_neg_abs(x):
    return jnp.exp2(-jnp.abs(x))


def _gla_kernel(q_ref, k_ref, v_ref, ga_ref, gk_ref, up_ref, bias_ref, ng_ref, tril_ref, msb_ref, *rest,
                n_chunks, has_s0, heads):
    rest = list(rest)
    s0_ref = rest.pop(0) if has_s0 else None
    y_ref = rest.pop(0)
    st_ref = None if has_s0 else rest.pop(0)
    c = GLA_CHUNK
    hb = c // 2
    nl = GLA_LEVELS
    dk = DK_A
    dv = DV_A
    row = lax.broadcasted_iota(jnp.int32, (c, 1), 0)

    def pass1(hd, rows):
        gk = gk_ref[rows, :].astype(BF16)
        las = []
        for d in range(2):
            zg = (jnp.dot(gk, up_ref[d, :, hd * dk:(hd + 1) * dk].astype(BF16), preferred_element_type=F32)
                  + bias_ref[d, :, hd * dk:(hd + 1) * dk])
            log_sig = jnp.minimum(zg, 0.0) - jnp.log1p(jnp.exp(-jnp.abs(zg)))
            las.append(log_sig * (LOG2E / GLA_GATE_NORM))
        parts = jnp.concatenate(_split3(las[0]) + _split3(las[1]), axis=1)
        cs = jnp.dot(tril_ref[...], parts, preferred_element_type=F32)
        cums = [cs[:, 0:dk] + cs[:, dk:2 * dk] + cs[:, 2 * dk:3 * dk],
                cs[:, 3 * dk:4 * dk] + cs[:, 4 * dk:5 * dk] + cs[:, 5 * dk:6 * dk]]
        tot_f = cums[0][c - 1:c, :]
        tot = jnp.concatenate([tot_f, cums[1][c - 1:c, :]], axis=1)
        k = k_ref[rows, hd * dk:(hd + 1) * dk].astype(F32)
        kd = jnp.concatenate([k * _exp2_neg_abs(tot_f - cums[0]), k * _exp2_neg_abs(cums[1] - las[1])], axis=1)
        kv = lax.dot_general(kd.astype(BF16), v_ref[rows, hd * dv:(hd + 1) * dv].astype(BF16),
                             (((0,), (0,)), ((), ())), preferred_element_type=F32)
        return las, cums, tot, kv

    def pass2(hd, rows, las, cums, tot, s_in):
        laf, lab = las
        cumf, cumb = cums
        cumbx = cumb - lab
        q = q_ref[rows, hd * dk:(hd + 1) * dk].astype(F32) * (DK_A ** -0.5)
        k = k_ref[rows, hd * dk:(hd + 1) * dk].astype(F32)
        v = v_ref[rows, hd * dv:(hd + 1) * dv].astype(BF16)
        msb = msb_ref[...]
        q2 = (2.0 * q).astype(BF16)
        kb = k.astype(BF16)
        scd = [_dot_nt(q2[0:hb], kb[0:hb]), _dot_nt(q2[hb:], kb[hb:])]
        a_lo = a_up = None
        for lvl in range(nl):
            b = 1 << lvl
            second = (row & b) != 0
            if lvl == 0:
                qq = q * jnp.exp2(jnp.where(second, laf, lab))
                kk = k
            elif lvl == 1:
                p = row & 3
                dq = jnp.where(p == 0, lab + pltpu.roll(lab, c - 1, 0),
                               jnp.where(p == 1, lab, jnp.where(p == 2, laf, laf + pltpu.roll(laf, 1, 0))))
                dkk = jnp.where(p == 0, pltpu.roll(laf, c - 1, 0), jnp.where(p == 3, pltpu.roll(lab, 1, 0), 0.0))
                qq = q * jnp.exp2(dq)
                kk = k * jnp.exp2(dkk)
            else:
                n = c // (2 * b)
                shp = (n, 2 * b, dk)
                cf3 = cumf.reshape(shp)
                cb3 = cumb.reshape(shp)
                ef = _exp2_neg_abs(cf3 - cf3[:, b - 1:b, :])
                eb = _exp2_neg_abs(cumbx.reshape(shp) - cb3[:, b - 1:b, :])
                if b >= SUBLANES:
                    eq = jnp.concatenate([eb[:, :b], ef[:, b:]], axis=1).reshape(c, dk)
                    ek = jnp.concatenate([ef[:, :b], eb[:, b:]], axis=1).reshape(c, dk)
                else:
                    ef = ef.reshape(c, dk)
                    eb = eb.reshape(c, dk)
                    eq = jnp.where(second, ef, eb)
                    ek = jnp.where(second, eb, ef)
                qq = q * eq
                kk = k * ek
            qq = qq.astype(BF16)
            kk = kk.astype(BF16)
            if lvl < nl - 1:
                for blk in range(2):
                    a = _dot_nt(qq[blk * hb:(blk + 1) * hb], kk[blk * hb:(blk + 1) * hb])
                    scd[blk] = jnp.where(msb == lvl, a, scd[blk])
            else:
                a_lo = _dot_nt(qq[hb:], kk[:hb])
                a_up = _dot_nt(qq[:hb], kk[hb:])
        sc = jnp.concatenate([jnp.concatenate([scd[0], a_up], axis=1),
                              jnp.concatenate([a_lo, scd[1]], axis=1)], axis=0)
        o = jnp.dot(sc.astype(BF16), v, preferred_element_type=F32)
        if s_in is not None:
            qd = jnp.concatenate([q * _exp2_neg_abs(cumf), q * _exp2_neg_abs(tot[:, dk:] - cumbx)], axis=1)
            o = o + jnp.dot(qd.astype(BF16), s_in.astype(BF16), preferred_element_type=F32)
        o = o * lax.rsqrt(jnp.mean(o * o, axis=-1, keepdims=True) + EPS) * ng_ref[...]
        ga = ga_ref[rows, hd * dv:(hd + 1) * dv].astype(F32)
        y_ref[rows, hd * dv:(hd + 1) * dv] = (o * (ga * jax.nn.sigmoid(ga))).astype(y_ref.dtype)

    if not has_s0:
        rows = slice(0, c)
        for hd in range(heads):
            las, cums, tot, kv = pass1(hd, rows)
            st_ref[0, hd] = kv[0:dk]
            st_ref[1, hd] = kv[dk:2 * dk]
            pass2(hd, rows, las, cums, tot, None)
        return

    la_sc, cum_sc, kv_sc, tot_sc, sin_sc = rest
    assert heads == 1

    def run_pass1(ci, carry):
        rows = pl.ds(pl.multiple_of(ci * c, c), c)
        las, cums, tot, kv = pass1(0, rows)
        for d in range(2):
            la_sc[d, rows, :] = las[d]
            cum_sc[d, rows, :] = cums[d]
        tot_sc[ci] = tot
        kv_sc[ci] = kv
        return carry

    lax.fori_loop(0, n_chunks, run_pass1, 0)

    eye = lax.broadcasted_iota(jnp.int32, (2 * dk, 2 * dk), 0) == lax.broadcasted_iota(
        jnp.int32, (2 * dk, 2 * dk), 1)

    def decay_col(ci):
        tot = jnp.broadcast_to(tot_sc[ci], (2 * dk, 2 * dk))
        return jnp.exp2(jnp.sum(jnp.where(eye, tot, 0.0), axis=1, keepdims=True))

    s = s0_ref[0]
    for ci in range(n_chunks):
        sin_sc[ci, 0:dk, :] = s
        if ci < n_chunks - 1:
            s = decay_col(ci)[0:dk] * s + kv_sc[ci, 0:dk, :]
    s = s0_ref[1]
    for ci in reversed(range(n_chunks)):
        sin_sc[ci, dk:2 * dk, :] = s
        if ci > 0:
            s = decay_col(ci)[dk:2 * dk] * s + kv_sc[ci, dk:2 * dk, :]

    def run_pass2(ci, carry):
        rows = pl.ds(pl.multiple_of(ci * c, c), c)
        pass2(0, rows, (la_sc[0, rows, :], la_sc[1, rows, :]), (cum_sc[0, rows, :], cum_sc[1, rows, :]),
              tot_sc[ci], sin_sc[ci])
        return carry

    lax.fori_loop(0, n_chunks, run_pass2, 0)


def gla(z, up_pad, bias, norm_g, tril, msb, l, *, row0, nb, seq, s0):
    rb0 = row0 // seq
    n_chunks = seq // GLA_CHUNK
    has_s0 = s0 is not None
    assert has_s0 or n_chunks == 1
    heads = 1 if has_s0 else H_A
    hb = GLA_CHUNK // 2
    if has_s0:
        grid = (nb, H_A)
        im = lambda f: (lambda b, h: f(b, h))
    else:
        grid = (nb,)
        im = lambda f: (lambda b: f(b, 0))
    nk, nv = heads * DK_A, heads * DV_A
    in_specs = [
        pl.BlockSpec((seq, nk), im(lambda b, h: (rb0 + b, Z_QA // nk + h))),
        pl.BlockSpec((seq, nk), im(lambda b, h: (rb0 + b, Z_KA // nk + h))),
        pl.BlockSpec((seq, nv), im(lambda b, h: (rb0 + b, Z_VA // nv + h))),
        pl.BlockSpec((seq, nv), im(lambda b, h: (rb0 + b, Z_GA // nv + h))),
        pl.BlockSpec((seq, 128), im(lambda b, h: (rb0 + b, Z_GK // 128))),
        pl.BlockSpec((None, 2, 128, nk), im(lambda b, h: (l, 0, 0, h))),
        pl.BlockSpec((None, 2, 1, nk), im(lambda b, h: (l, 0, 0, h))),
        pl.BlockSpec((None, 1, DV_A), im(lambda b, h: (l, 0, 0))),
        pl.BlockSpec((GLA_CHUNK, GLA_CHUNK), im(lambda b, h: (0, 0))),
        pl.BlockSpec((hb, hb), im(lambda b, h: (0, 0))),
    ]
    args = [z, z, z, z, z, up_pad, bias.reshape(DEPTH, 2, 1, H_A * DK_A),
            norm_g.reshape(DEPTH, 1, DV_A), tril, msb]
    out_shape = [jax.ShapeDtypeStruct((nb * seq, H_A * DV_A), BF16)]
    out_specs = [pl.BlockSpec((seq, nv), im(lambda b, h: (b, h)))]
    scratch = []
    if has_s0:
        in_specs.append(pl.BlockSpec((None, None, 2, None, DK_A, DV_A), im(lambda b, h: (b, l, 0, h, 0, 0))))
        args.append(s0)
        scratch = [pltpu.VMEM((2, seq, DK_A), F32), pltpu.VMEM((2, seq, DK_A), F32),
                   pltpu.VMEM((n_chunks, 2 * DK_A, DV_A), F32),
                   pltpu.VMEM((n_chunks, 1, 2 * DK_A), F32),
                   pltpu.VMEM((n_chunks, 2 * DK_A, DV_A), F32)]
    else:
        out_shape.append(jax.ShapeDtypeStruct((nb, 2, H_A, DK_A, DV_A), F32))
        out_specs.append(pl.BlockSpec((None, 2, H_A, DK_A, DV_A), im(lambda b, h: (b, 0, 0, 0, 0))))
    return pl.pallas_call(
        functools.partial(_gla_kernel, n_chunks=n_chunks, has_s0=has_s0, heads=heads),
        out_shape=out_shape,
        grid=grid,
        in_specs=in_specs,
        out_specs=out_specs,
        scratch_shapes=scratch,
        compiler_params=_cparams(len(grid)),
        name="gla",
    )(*args)


def kernel(x_prompt, x_sample, c, state_gla, cache_mla_ckv, cache_mla_krope, cache_gqa_k, cache_gqa_v,
           c_ctx, norm1_g, norm2_g, w_ada, b_ada, w_in, gla_gk_up, gla_gk_bias, gla_norm_g,
           mla_qa_norm_g, mla_w_qb, mla_kva_norm_g, mla_w_kvb, mla_q_norm_g, mla_k_norm_g,
           gqa_q_norm_g, gqa_k_norm_g, w_out_a, w_out_b, w_out_c, w_o,
           ffn_w_up, ffn_conv_w, ffn_conv_b, ffn_w_down):
    d = D_MODEL
    x = jnp.concatenate([x_prompt.reshape(T_P, d), x_sample.reshape(T_S, d)], axis=0)
    cvec = jnp.concatenate([c_ctx[None, :], c, jnp.zeros((MOD_ROWS - N_GROUPS, d), F32)], axis=0)
    mods = adaln_all(cvec, w_ada, b_ada)

    w_qb_pad = jnp.pad(mla_w_qb.reshape(DEPTH, Q_LORA_B, H_B, QK_B),
                       ((0, 0), (0, 0), (0, 0), (0, PAD_QK_B - QK_B))).reshape(DEPTH, Q_LORA_B, H_B * PAD_QK_B)
    q_g_pad = jnp.pad(mla_q_norm_g, ((0, 0), (0, PAD_QK_B - QK_B))).reshape(DEPTH, 1, PAD_QK_B)
    k_g_pad = jnp.pad(mla_k_norm_g, ((0, 0), (0, PAD_QK_B - QK_B))).reshape(DEPTH, 1, PAD_QK_B)
    up_pad = jnp.zeros((DEPTH, 2, 128, H_A * DK_A), F32)
    for dd in range(2):
        r0 = dd * GLA_GATE_RANK
        up_pad = up_pad.at[:, dd, r0:r0 + GLA_GATE_RANK, :].set(gla_gk_up[:, dd])
    tab_b = _rope_tables(DEC_SEQ, ROPE_B, 128)
    tab_c = _rope_tables(DEC_SEQ, HD_C, 128)
    tril, msb = _gla_constants()
    w_in_t = jnp.swapaxes(w_in, 1, 2)
    ck = cache_gqa_k.reshape(DEC_BATCH, DEPTH, PAST_LEN, HKV_C * HD_C)
    cv = cache_gqa_v.reshape(DEC_BATCH, DEPTH, PAST_LEN, HKV_C * HD_C)

    st_out, ckv_out, krope_out, kc_out, vc_out = [], [], [], [], []
    for l in range(DEPTH):
        h = norm_mod(x, norm1_g, mods, l, 0, 1)
        z = w_in_proj(h, w_in_t, l)

        ya_p, st = gla(z, up_pad, gla_gk_bias, gla_norm_g, tril, msb, l,
                       row0=0, nb=BATCH, seq=SEQ, s0=None)
        ya_s, = gla(z, up_pad, gla_gk_bias, gla_norm_g, tril, msb, l,
                    row0=T_P, nb=DEC_BATCH, seq=DEC_SEQ, s0=state_gla)

        bmk = 512
        qb_p = mla_q_proj(z, mla_qa_norm_g, w_qb_pad, q_g_pad, l, row0=0, rows=T_P, tab=None)
        qb_s = mla_q_proj(z, mla_qa_norm_g, w_qb_pad, q_g_pad, l, row0=T_P, rows=T_S, tab=tab_b)

        def tail_specs(row0):
            rb = row0 // bmk
            return (pl.BlockSpec((bmk, KV_LORA_B), lambda i: (rb + i, Z_CKV // KV_LORA_B)),
                    pl.BlockSpec((bmk, 128), lambda i: (rb + i, Z_KROPE // 128)))

        c_spec, kr_spec = tail_specs(0)
        kb_p, vb_p, ckv_p = mla_kv_proj(z, c_spec, z, kr_spec, mla_kva_norm_g, mla_w_kvb, k_g_pad, l,
                                        rows=T_P, bm=bmk, norm_in=True, tab=None)
        c_spec, kr_spec = tail_specs(T_P)
        kb_s, vb_s, _ = mla_kv_proj(z, c_spec, z, kr_spec, mla_kva_norm_g, mla_w_kvb, k_g_pad, l,
                                    rows=T_S, bm=bmk, norm_in=True, tab=tab_b)
        kb_c, vb_c = mla_kv_proj(
            cache_mla_ckv, pl.BlockSpec((None, None, PAST_LEN, KV_LORA_B), lambda i: (i, l, 0, 0)),
            cache_mla_krope, pl.BlockSpec((None, None, PAST_LEN, ROPE_B), lambda i: (i, l, 0, 0)),
            mla_kva_norm_g, mla_w_kvb, k_g_pad, l, rows=DEC_BATCH * PAST_LEN, bm=PAST_LEN, norm_in=False, tab=None)
        nkb, nvb = H_B * PAD_QK_B, H_B * V_B
        yb_p = attention(qb_p, [(kb_p, _rows_spec(SEQ, nkb), vb_p, _rows_spec(SEQ, nvb), SEQ)],
                         nb=BATCH, q_rows=SEQ, hq=H_B, hkv=H_B, dqk=PAD_QK_B, dv=V_B, name="mla_attn_p")
        yb_s = attention(qb_s, [(kb_s, _rows_spec(DEC_SEQ, nkb), vb_s, _rows_spec(DEC_SEQ, nvb), DEC_SEQ),
                                (kb_c, _rows_spec(PAST_LEN, nkb), vb_c, _rows_spec(PAST_LEN, nvb), PAST_LEN)],
                         nb=DEC_BATCH, q_rows=DEC_SEQ, hq=H_B, hkv=H_B, dqk=PAD_QK_B, dv=V_B, name="mla_attn_s")

        qc_p, kc_p, vc_p, kn_p = gqa_prep(z, gqa_q_norm_g, gqa_k_norm_g, l, row0=0, rows=T_P,
                                          tab=None, emit_kn=True)
        qc_s, kc_s, vc_s = gqa_prep(z, gqa_q_norm_g, gqa_k_norm_g, l, row0=T_P, rows=T_S,
                                    tab=tab_c, emit_kn=False)
        nkc = HKV_C * HD_C
        yc_p = attention(qc_p, [(kc_p, _rows_spec(SEQ, nkc), vc_p, _rows_spec(SEQ, nkc), SEQ)],
                         nb=BATCH, q_rows=SEQ, hq=HQ_C, hkv=HKV_C, dqk=HD_C, dv=HD_C, name="gqa_attn_p")
        cache_spec = pl.BlockSpec((None, None, PAST_LEN, nkc), lambda b: (b, l, 0, 0))
        yc_s = attention(qc_s, [(kc_s, _rows_spec(DEC_SEQ, nkc), vc_s, _rows_spec(DEC_SEQ, nkc), DEC_SEQ),
                                (ck, cache_spec, cv, cache_spec, PAST_LEN)],
                         nb=DEC_BATCH, q_rows=DEC_SEQ, hq=HQ_C, hkv=HKV_C, dqk=HD_C, dv=HD_C, name="gqa_attn_s")

        m = gated_merge((ya_p, ya_s, yb_p, yb_s, yc_p, yc_s), z, w_out_a, w_out_b, w_out_c, l)
        x, h2 = w_o_norm(m, w_o, l, x, mods, norm2_g)
        act = up_conv(h2, ffn_w_up, ffn_conv_w, ffn_conv_b, l)
        x = matmul_residual(act, ffn_w_down, l, x, mods, 5, bm=512, bn=512, name="ffn_down")

        st_out.append(st)
        ckv_out.append(ckv_p.reshape(BATCH, SEQ, KV_LORA_B))
        krope_out.append(z[:T_P, Z_KROPE:Z_KROPE + ROPE_B].astype(F32).reshape(BATCH, SEQ, ROPE_B))
        kc_out.append(kn_p.reshape(BATCH, SEQ, HKV_C, HD_C))
        vc_out.append(z[:T_P, Z_VC:Z_VC + nkc].astype(F32).reshape(BATCH, SEQ, HKV_C, HD_C))

    y_p = x[:T_P].reshape(BATCH, SEQ, d)
    y_s = x[T_P:].reshape(DEC_BATCH, DEC_SEQ, d)
    return (y_p, y_s, jnp.stack(st_out, axis=1), jnp.stack(ckv_out, axis=1), jnp.stack(krope_out, axis=1),
            jnp.stack(kc_out, axis=1), jnp.stack(vc_out, axis=1))
```

```python
import functools

import numpy as np
import jax
import jax.numpy as jnp
from jax import lax
from jax.experimental import pallas as pl
from jax.experimental.pallas import tpu as pltpu

F32 = jnp.float32
BF16 = jnp.bfloat16

D_MODEL = 2048
BATCH = 16
SEQ = 256
DEPTH = 4
DEC_BATCH = 2
DEC_SEQ = 1024
PAST_LEN = 512
GRID_W = 64
ROPE_THETA = 10000.0
EPS = 1e-6
H_A, DK_A, DV_A = 4, 128, 256
GLA_GATE_RANK = 16
GLA_GATE_NORM = 16.0
H_B, NOPE_B, ROPE_B, V_B = 8, 128, 64, 128
QK_B = NOPE_B + ROPE_B
Q_LORA_B = 512
KV_LORA_B = 256
HQ_C, HKV_C, HD_C = 8, 2, 128
D_FF = 5632

T_P = BATCH * SEQ
T_S = DEC_BATCH * DEC_SEQ
T_ALL = T_P + T_S
N_GROUPS = 1 + DEC_BATCH
SUBLANES = 8
MOD_ROWS = SUBLANES

_O_QA, _O_KA, _O_VA, _O_GA = 0, 512, 1024, 2048
_O_GKA = 3072
_O_QLORA = 3104
_O_KVA = 3616
_O_QC, _O_KC, _O_VC = 3936, 4960, 5216
_O_GATES = 5472
N_IN = 11616
W_IN_BN = 512
W_IN_RANGES = ((_O_GKA, 1), (_O_QC, 15), (_O_QA, 6), (_O_QLORA, 2))
N_Z = W_IN_BN * sum(nb for _, nb in W_IN_RANGES)
Z_GK = 0
Z_QC, Z_KC, Z_VC, Z_GATES = 512, 1536, 1792, 2048
Z_QA, Z_KA, Z_VA, Z_GA = 8192, 8704, 9216, 10240
Z_QLORA, Z_CKV, Z_KROPE = 11264, 11776, 12032
PAD_QK_B = 256

GLA_CHUNK = 256
GLA_LEVELS = 8

VMEM_LIMIT = 56 * 1024 * 1024
LOG2E = 1.4426950408889634


def _cparams(n_axes):
    return pltpu.CompilerParams(
        dimension_semantics=("arbitrary",) * n_axes, vmem_limit_bytes=VMEM_LIMIT)


def _group_of_block(i, bm):
    n_p = T_P // bm
    per = DEC_SEQ // bm
    return jnp.where(i < n_p, 0, 1 + (i - n_p) // per)


def _adaln_kernel(c_ref, w_ref, b_ref, o_ref):
    c = c_ref[...]
    a = (c * jax.nn.sigmoid(c)).astype(BF16)
    o_ref[...] = jnp.dot(a, w_ref[...].astype(BF16), preferred_element_type=F32) + b_ref[...]


def adaln_all(cvec, w_ada, b_ada):
    bn = 1024
    n = 6 * D_MODEL
    return pl.pallas_call(
        _adaln_kernel,
        out_shape=jax.ShapeDtypeStruct((DEPTH, MOD_ROWS, n), F32),
        grid=(DEPTH, n // bn),
        in_specs=[
            pl.BlockSpec((MOD_ROWS, D_MODEL), lambda l, j: (0, 0)),
            pl.BlockSpec((None, D_MODEL, bn), lambda l, j: (l, 0, j)),
            pl.BlockSpec((None, 1, bn), lambda l, j: (l, 0, j)),
        ],
        out_specs=pl.BlockSpec((None, MOD_ROWS, bn), lambda l, j: (l, 0, j)),
        compiler_params=_cparams(2),
        name="adaln",
    )(cvec, w_ada, b_ada.reshape(DEPTH, 1, n))


def _norm_mod_kernel(x_ref, g_ref, sh_ref, sc_ref, o_ref, *, bm):
    grp = _group_of_block(pl.program_id(0), bm)
    x = x_ref[...]
    y = x * lax.rsqrt(jnp.mean(x * x, axis=-1, keepdims=True) + EPS) * g_ref[...]
    sh = sh_ref[pl.ds(grp, 1), :]
    sc = sc_ref[pl.ds(grp, 1), :]
    o_ref[...] = (y * (1.0 + sc) + sh).astype(o_ref.dtype)


def norm_mod(x, g, mods, l, which_shift, which_scale):
    bm = 512
    nd = D_MODEL
    return pl.pallas_call(
        functools.partial(_norm_mod_kernel, bm=bm),
        out_shape=jax.ShapeDtypeStruct((T_ALL, nd), BF16),
        grid=(T_ALL // bm,),
        in_specs=[
            pl.BlockSpec((bm, nd), lambda i: (i, 0)),
            pl.BlockSpec((None, 1, nd), lambda i: (l, 0, 0)),
            pl.BlockSpec((None, MOD_ROWS, nd), lambda i: (l, 0, which_shift)),
            pl.BlockSpec((None, MOD_ROWS, nd), lambda i: (l, 0, which_scale)),
        ],
        out_specs=pl.BlockSpec((bm, nd), lambda i: (i, 0)),
        compiler_params=_cparams(1),
        name="norm_mod",
    )(x, g.reshape(DEPTH, 1, nd), mods, mods)


def _dot_nt(a, b):
    return lax.dot_general(a, b, (((1,), (1,)), ((), ())), preferred_element_type=F32)


ROW_BLK = 1024


def _w_in_kernel(h_ref, wt_ref, o_ref, wb_ref):
    wb_ref[...] = wt_ref[0].T.astype(BF16)

    def body(r, carry):
        rows = pl.ds(pl.multiple_of(r * ROW_BLK, ROW_BLK), ROW_BLK)
        o_ref[rows, :] = jnp.dot(h_ref[rows, :], wb_ref[...], preferred_element_type=F32).astype(o_ref.dtype)
        return carry

    lax.fori_loop(0, T_ALL // ROW_BLK, body, 0)


def _w_in_first_col(j):
    col8 = None
    start = sum(nb for _, nb in W_IN_RANGES)
    for c0, nb in reversed(W_IN_RANGES):
        assert c0 % 8 == 0
        start -= nb
        here = c0 // 8 + (j - start) * (W_IN_BN // 8)
        col8 = here if col8 is None else jnp.where(j < start + nb, here, col8)
    return col8 * 8


def w_in_proj(h, w_in_t, l):
    bn = W_IN_BN
    return pl.pallas_call(
        _w_in_kernel,
        out_shape=jax.ShapeDtypeStruct((T_ALL, N_Z), BF16),
        grid=(N_Z // bn,),
        in_specs=[
            pl.BlockSpec((T_ALL, D_MODEL), lambda j: (0, 0), pipeline_mode=pl.Buffered(1)),
            pl.BlockSpec((pl.Element(1), pl.Element(bn), pl.Element(D_MODEL)),
                         lambda j: (l, _w_in_first_col(j), 0)),
        ],
        out_specs=pl.BlockSpec((T_ALL, bn), lambda j: (0, j)),
        scratch_shapes=[pltpu.VMEM((D_MODEL, bn), BF16)],
        compiler_params=_cparams(1),
        name="w_in",
    )(h, w_in_t)


def _mm_res_kernel(a_ref, w_ref, x_ref, gate_ref, o_ref, wb_ref, *, bm):
    @pl.when(pl.program_id(1) == 0)
    def _():
        wb_ref[...] = w_ref[...].astype(BF16)

    grp = _group_of_block(pl.program_id(1), bm)
    acc = jnp.dot(a_ref[...], wb_ref[...], preferred_element_type=F32)
    o_ref[...] = x_ref[...] + gate_ref[pl.ds(grp, 1), :] * acc


def matmul_residual(a, w, l, x, mods, which_gate, *, bm, bn, name):
    m, k = a.shape
    n = w.shape[-1]
    gate_blk = which_gate * (D_MODEL // bn)
    return pl.pallas_call(
        functools.partial(_mm_res_kernel, bm=bm),
        out_shape=jax.ShapeDtypeStruct((m, n), F32),
        grid=(n // bn, m // bm),
        in_specs=[
            pl.BlockSpec((bm, k), lambda j, i: (i, 0)),
            pl.BlockSpec((None, k, bn), lambda j, i: (l, 0, j)),
            pl.BlockSpec((bm, bn), lambda j, i: (i, j)),
            pl.BlockSpec((None, MOD_ROWS, bn), lambda j, i: (l, 0, gate_blk + j)),
        ],
        out_specs=pl.BlockSpec((bm, bn), lambda j, i: (i, j)),
        scratch_shapes=[pltpu.VMEM((k, bn), BF16)],
        compiler_params=_cparams(2),
        name=name,
    )(a, w, x, mods)


def _w_o_norm_kernel(m_ref, w_ref, x_ref, gate_ref, g_ref, sh_ref, sc_ref, xo_ref, h_ref, wb_ref, *, bm):
    i = pl.program_id(0)

    @pl.when(i == 0)
    def _():
        wb_ref[...] = w_ref[...].astype(BF16)

    grp = _group_of_block(i, bm)
    acc = jnp.dot(m_ref[...], wb_ref[...], preferred_element_type=F32)
    x = x_ref[...] + gate_ref[pl.ds(grp, 1), :] * acc
    xo_ref[...] = x
    y = x * lax.rsqrt(jnp.mean(x * x, axis=-1, keepdims=True) + EPS) * g_ref[...]
    h_ref[...] = (y * (1.0 + sc_ref[pl.ds(grp, 1), :]) + sh_ref[pl.ds(grp, 1), :]).astype(h_ref.dtype)


def w_o_norm(m, w_o, l, x, mods, norm_g):
    bm = 512
    nd = D_MODEL

    def mod_spec(which):
        return pl.BlockSpec((None, MOD_ROWS, nd), lambda i: (l, 0, which))

    return pl.pallas_call(
        functools.partial(_w_o_norm_kernel, bm=bm),
        out_shape=[jax.ShapeDtypeStruct((T_ALL, nd), F32), jax.ShapeDtypeStruct((T_ALL, nd), BF16)],
        grid=(T_ALL // bm,),
        in_specs=[
            pl.BlockSpec((bm, nd), lambda i: (i, 0)),
            pl.BlockSpec((None, nd, nd), lambda i: (l, 0, 0), pipeline_mode=pl.Buffered(1)),
            pl.BlockSpec((bm, nd), lambda i: (i, 0)),
            mod_spec(2),
            pl.BlockSpec((None, 1, nd), lambda i: (l, 0, 0)),
            mod_spec(3), mod_spec(4),
        ],
        out_specs=[pl.BlockSpec((bm, nd), lambda i: (i, 0)), pl.BlockSpec((bm, nd), lambda i: (i, 0))],
        scratch_shapes=[pltpu.VMEM((nd, nd), BF16)],
        compiler_params=_cparams(1),
        name="w_o_norm",
    )(m, w_o, x, mods, norm_g.reshape(DEPTH, 1, nd), mods, mods)


def _merge_kernel(yap_ref, yas_ref, ybp_ref, ybs_ref, ycp_ref, ycs_ref, wa_ref, wb_ref, wc_ref,
                  ga_ref, gb_ref, gc_ref, o_ref, wab_ref, wbb_ref, wcb_ref, *, bm):
    i = pl.program_id(1)

    @pl.when(i == 0)
    def _():
        wab_ref[...] = wa_ref[...].astype(BF16)
        wbb_ref[...] = wb_ref[...].astype(BF16)
        wcb_ref[...] = wc_ref[...].astype(BF16)

    is_ctx = i < T_P // bm

    def branch(yp_ref, ys_ref, w_ref, g_ref):
        y = jnp.where(is_ctx, yp_ref[...], ys_ref[...])
        return jax.nn.sigmoid(g_ref[...].astype(F32)) * jnp.dot(y, w_ref[...], preferred_element_type=F32)

    m = (branch(yap_ref, yas_ref, wab_ref, ga_ref) + branch(ybp_ref, ybs_ref, wbb_ref, gb_ref)
         + branch(ycp_ref, ycs_ref, wcb_ref, gc_ref))
    o_ref[...] = m.astype(o_ref.dtype)


def gated_merge(ys, z, w_out_a, w_out_b, w_out_c, l):
    bm, bn = 512, 1024
    k = 1024
    gate0 = Z_GATES // bn
    per = D_MODEL // bn
    n_p = T_P // bm
    yp_spec = pl.BlockSpec((bm, k), lambda j, i: (jnp.minimum(i, n_p - 1), 0))
    ys_spec = pl.BlockSpec((bm, k), lambda j, i: (jnp.maximum(i - n_p, 0), 0))
    w_spec = pl.BlockSpec((None, k, bn), lambda j, i: (l, 0, j))

    def g_spec(br):
        return pl.BlockSpec((bm, bn), lambda j, i: (i, gate0 + br * per + j))

    return pl.pallas_call(
        functools.partial(_merge_kernel, bm=bm),
        out_shape=jax.ShapeDtypeStruct((T_ALL, D_MODEL), BF16),
        grid=(D_MODEL // bn, T_ALL // bm),
        in_specs=[yp_spec, ys_spec] * 3 + [w_spec] * 3 + [g_spec(0), g_spec(1), g_spec(2)],
        out_specs=pl.BlockSpec((bm, bn), lambda j, i: (i, j)),
        scratch_shapes=[pltpu.VMEM((k, bn), BF16)] * 3,
        compiler_params=_cparams(2),
        name="gated_merge",
    )(*ys, w_out_a, w_out_b, w_out_c, z, z, z)


def _up_conv_kernel(h_ref, wa_ref, wg_ref, cwa_ref, cwg_ref, cba_ref, cbg_ref, o_ref, wab_ref, wgb_ref,
                    *sc_refs):
    wab_ref[...] = wa_ref[...].astype(BF16)
    wgb_ref[...] = wg_ref[...].astype(BF16)
    bm = ROW_BLK
    nt = bm // SUBLANES
    bn = o_ref.shape[1]
    sub = lax.broadcasted_iota(jnp.int32, (1, SUBLANES, 1), 1)
    for sc_ref in sc_refs:
        sc_ref[0:SUBLANES, :] = jnp.zeros((SUBLANES, bn), F32)
        sc_ref[SUBLANES + bm:2 * SUBLANES + bm, :] = jnp.zeros((SUBLANES, bn), F32)

    def taps(cw_ref, cb_ref):
        return [jnp.broadcast_to(cw_ref[k:k + 1, :], (SUBLANES, bn))[None] for k in range(3)] + [
            jnp.broadcast_to(cb_ref[...], (SUBLANES, bn))[None]]

    taps_a = taps(cwa_ref, cba_ref)
    taps_g = taps(cwg_ref, cbg_ref)

    for r in range(T_ALL // bm):
        seq = SEQ if r < T_P // bm else DEC_SEQ
        tps = seq // SUBLANES
        n_seq = bm // seq
        h = h_ref[r * bm:(r + 1) * bm, :]

        def conv(w_ref, tp, sc_ref):
            u = jnp.dot(h, w_ref[...], preferred_element_type=F32)
            sc_ref[SUBLANES:SUBLANES + bm, :] = u
            prev = sc_ref[SUBLANES - 1:SUBLANES - 1 + bm, :].reshape(nt, SUBLANES, bn)
            nxt = sc_ref[SUBLANES + 1:SUBLANES + 1 + bm, :].reshape(nt, SUBLANES, bn)
            if n_seq > 1:
                prev = jnp.concatenate(
                    [p for s in range(n_seq) for p in (
                        jnp.where(sub == 0, 0.0, prev[s * tps:s * tps + 1]), prev[s * tps + 1:(s + 1) * tps])],
                    axis=0)
                nxt = jnp.concatenate(
                    [p for s in range(n_seq) for p in (
                        nxt[s * tps:(s + 1) * tps - 1],
                        jnp.where(sub == SUBLANES - 1, 0.0, nxt[(s + 1) * tps - 1:(s + 1) * tps]))],
                    axis=0)
            return prev * tp[0] + u.reshape(nt, SUBLANES, bn) * tp[1] + nxt * tp[2] + tp[3]

        a = conv(wab_ref, taps_a, sc_refs[0])
        g = conv(wgb_ref, taps_g, sc_refs[1])
        act = (g * jax.nn.sigmoid(g) * a).reshape(bm, bn)
        o_ref[r * bm:(r + 1) * bm, :] = act.astype(o_ref.dtype)


def up_conv(h2, w_up, conv_w, conv_b, l):
    bn = 256
    nb = D_FF // bn
    cb = conv_b.reshape(DEPTH, 1, 2 * D_FF)
    return pl.pallas_call(
        _up_conv_kernel,
        out_shape=jax.ShapeDtypeStruct((T_ALL, D_FF), BF16),
        grid=(nb,),
        in_specs=[
            pl.BlockSpec((T_ALL, D_MODEL), lambda j: (0, 0), pipeline_mode=pl.Buffered(1)),
            pl.BlockSpec((None, D_MODEL, bn), lambda j: (l, 0, j)),
            pl.BlockSpec((None, D_MODEL, bn), lambda j: (l, 0, nb + j)),
            pl.BlockSpec((None, 3, bn), lambda j: (l, 0, j)),
            pl.BlockSpec((None, 3, bn), lambda j: (l, 0, nb + j)),
            pl.BlockSpec((None, 1, bn), lambda j: (l, 0, j)),
            pl.BlockSpec((None, 1, bn), lambda j: (l, 0, nb + j)),
        ],
        out_specs=pl.BlockSpec((T_ALL, bn), lambda j: (0, j)),
        scratch_shapes=[pltpu.VMEM((D_MODEL, bn), BF16)] * 2
        + [pltpu.VMEM((ROW_BLK + 2 * SUBLANES, bn), F32)] * 2,
        compiler_params=_cparams(1),
        name="up_conv",
    )(h2, w_up, w_up, conv_w, conv_w, cb, cb)


def _rope_tables(n_tok, rot_dim, width):
    rows = n_tok // GRID_W
    row = jnp.repeat(jnp.arange(rows, dtype=F32), GRID_W)
    col = jnp.tile(jnp.arange(GRID_W, dtype=F32), rows)
    half = rot_dim // 2
    inv = jnp.power(ROPE_THETA, -jnp.arange(half // 2, dtype=F32) * (2.0 / half))
    ang = jnp.concatenate([row[:, None] * inv, col[:, None] * inv], axis=-1)
    cos, sin = jnp.cos(ang), jnp.sin(ang)
    zero = jnp.zeros_like(sin)
    c2 = jnp.stack([cos, cos], axis=-1).reshape(n_tok, rot_dim)
    s_even = jnp.stack([-sin, zero], axis=-1).reshape(n_tok, rot_dim)
    s_odd = jnp.stack([zero, sin], axis=-1).reshape(n_tok, rot_dim)
    pad = ((0, 0), (0, width - rot_dim))
    return jnp.stack([jnp.pad(c2, pad), jnp.pad(s_even, pad), jnp.pad(s_odd, pad)], axis=0)


def _rope(x, tab_ref):
    w = x.shape[-1]
    return (x * tab_ref[0] + pltpu.roll(x, w - 1, 1) * tab_ref[1] + pltpu.roll(x, 1, 1) * tab_ref[2])


def _mla_q_kernel(ql_ref, g1_ref, w_ref, g2_ref, *rest, rope):
    if rope:
        tab_ref, o_ref, wb_ref = rest
    else:
        o_ref, wb_ref = rest

    @pl.when(pl.program_id(0) == 0)
    def _():
        wb_ref[...] = w_ref[...].astype(BF16)

    x = ql_ref[...].astype(F32)
    xn = x * lax.rsqrt(jnp.mean(x * x, axis=-1, keepdims=True) + EPS) * g1_ref[...]
    q = jnp.dot(xn.astype(BF16), wb_ref[...], preferred_element_type=F32)
    scale = QK_B ** -0.5 * LOG2E
    for h in range(H_B):
        lo = h * PAD_QK_B
        qn = q[:, lo:lo + NOPE_B]
        qr = q[:, lo + NOPE_B:lo + PAD_QK_B]
        ms = (jnp.sum(qn * qn, axis=-1, keepdims=True) + jnp.sum(qr * qr, axis=-1, keepdims=True)) / QK_B
        r = lax.rsqrt(ms + EPS) * scale
        qn = qn * r * g2_ref[:, 0:NOPE_B]
        qr = qr * r * g2_ref[:, NOPE_B:PAD_QK_B]
        if rope:
            qr = _rope(qr, tab_ref)
        o_ref[:, lo:lo + NOPE_B] = qn.astype(o_ref.dtype)
        o_ref[:, lo + NOPE_B:lo + PAD_QK_B] = qr.astype(o_ref.dtype)


def mla_q_proj(z, qa_g, w_qb_pad, q_g_pad, l, *, row0, rows, tab):
    bm = 512
    rope = tab is not None
    kq = Q_LORA_B
    n = H_B * PAD_QK_B
    rb0 = row0 // bm
    in_specs = [
        pl.BlockSpec((bm, kq), lambda i: (rb0 + i, Z_QLORA // kq)),
        pl.BlockSpec((None, 1, kq), lambda i: (l, 0, 0)),
        pl.BlockSpec((None, kq, n), lambda i: (l, 0, 0)),
        pl.BlockSpec((None, 1, PAD_QK_B), lambda i: (l, 0, 0)),
    ]
    args = [z, qa_g.reshape(DEPTH, 1, kq), w_qb_pad, q_g_pad]
    if rope:
        per = DEC_SEQ // bm
        in_specs.append(pl.BlockSpec((3, bm, 128), lambda i: (0, i % per, 0)))
        args.append(tab)
    return pl.pallas_call(
        functools.partial(_mla_q_kernel, rope=rope),
        out_shape=jax.ShapeDtypeStruct((rows, n), BF16),
        grid=(rows // bm,),
        in_specs=in_specs,
        out_specs=pl.BlockSpec((bm, n), lambda i: (i, 0)),
        scratch_shapes=[pltpu.VMEM((kq, n), BF16)],
        compiler_params=_cparams(1),
        name="mla_q_proj",
    )(*args)


def _mla_kv_kernel(c_ref, kr_ref, g1_ref, w_ref, g2_ref, *rest, norm_in, rope):
    rest = list(rest)
    tab_ref = rest.pop(0) if rope else None
    if norm_in:
        k_ref, v_ref, ckv_ref, wb_ref = rest
    else:
        k_ref, v_ref, wb_ref = rest

    @pl.when(pl.program_id(0) == 0)
    def _():
        wb_ref[...] = w_ref[...].astype(BF16)

    c = c_ref[...].astype(F32)
    if norm_in:
        c = c * lax.rsqrt(jnp.mean(c * c, axis=-1, keepdims=True) + EPS) * g1_ref[...]
        ckv_ref[...] = c
    kv = jnp.dot(c.astype(BF16), wb_ref[...], preferred_element_type=F32)
    kr = kr_ref[...].astype(F32)
    if kr.shape[-1] == 128:
        lane = lax.broadcasted_iota(jnp.int32, kr.shape, 1)
        kr = jnp.where(lane < ROPE_B, kr, 0.0)
    else:
        kr = jnp.concatenate([kr, jnp.zeros_like(kr)], axis=-1)
    kr_ss = jnp.sum(kr * kr, axis=-1, keepdims=True)
    for h in range(H_B):
        kn = kv[:, h * 256:h * 256 + NOPE_B]
        v = kv[:, h * 256 + NOPE_B:(h + 1) * 256]
        r = lax.rsqrt((jnp.sum(kn * kn, axis=-1, keepdims=True) + kr_ss) / QK_B + EPS)
        krh = kr * r * g2_ref[:, NOPE_B:PAD_QK_B]
        if rope:
            krh = _rope(krh, tab_ref)
        k_ref[:, h * PAD_QK_B:h * PAD_QK_B + NOPE_B] = (kn * r * g2_ref[:, 0:NOPE_B]).astype(k_ref.dtype)
        k_ref[:, h * PAD_QK_B + NOPE_B:(h + 1) * PAD_QK_B] = krh.astype(k_ref.dtype)
        v_ref[:, h * V_B:(h + 1) * V_B] = v.astype(v_ref.dtype)


def mla_kv_proj(c_arr, c_spec, kr_arr, kr_spec, kva_g, w_kvb, k_g_pad, l, *, rows, bm, norm_in, tab):
    rope = tab is not None
    kc = KV_LORA_B
    n = H_B * (NOPE_B + V_B)
    in_specs = [
        c_spec, kr_spec,
        pl.BlockSpec((None, 1, kc), lambda i: (l, 0, 0)),
        pl.BlockSpec((None, kc, n), lambda i: (l, 0, 0)),
        pl.BlockSpec((None, 1, PAD_QK_B), lambda i: (l, 0, 0)),
    ]
    args = [c_arr, kr_arr, kva_g.reshape(DEPTH, 1, kc), w_kvb, k_g_pad]
    if rope:
        per = DEC_SEQ // bm
        in_specs.append(pl.BlockSpec((3, bm, 128), lambda i: (0, i % per, 0)))
        args.append(tab)
    out_shape = [jax.ShapeDtypeStruct((rows, H_B * PAD_QK_B), BF16),
                 jax.ShapeDtypeStruct((rows, H_B * V_B), BF16)]
    out_specs = [pl.BlockSpec((bm, H_B * PAD_QK_B), lambda i: (i, 0)),
                 pl.BlockSpec((bm, H_B * V_B), lambda i: (i, 0))]
    if norm_in:
        out_shape.append(jax.ShapeDtypeStruct((rows, kc), F32))
        out_specs.append(pl.BlockSpec((bm, kc), lambda i: (i, 0)))
    return pl.pallas_call(
        functools.partial(_mla_kv_kernel, norm_in=norm_in, rope=rope),
        out_shape=out_shape,
        grid=(rows // bm,),
        in_specs=in_specs,
        out_specs=out_specs,
        scratch_shapes=[pltpu.VMEM((kc, n), BF16)],
        compiler_params=_cparams(1),
        name="mla_kv_proj",
    )(*args)


def _gqa_prep_kernel(qlo_ref, qhi_ref, k_ref, v_ref, gq_ref, gk_ref, *rest, rope, emit_kn):
    rest = list(rest)
    tab_ref = rest.pop(0) if rope else None
    qo_ref, ko_ref, vo_ref = rest[:3]
    kn_ref = rest[3] if emit_kn else None
    scale = HD_C ** -0.5 * LOG2E
    half = HQ_C // 2

    def norm(x, g_ref):
        return x * lax.rsqrt(jnp.mean(x * x, axis=-1, keepdims=True) + EPS) * g_ref[...]

    for h in range(HQ_C):
        q_ref, hh = (qlo_ref, h) if h < half else (qhi_ref, h - half)
        x = norm(q_ref[:, hh * HD_C:(hh + 1) * HD_C].astype(F32), gq_ref)
        if rope:
            x = _rope(x, tab_ref)
        qo_ref[:, h * HD_C:(h + 1) * HD_C] = (x * scale).astype(qo_ref.dtype)
    for h in range(HKV_C):
        x = norm(k_ref[:, h * HD_C:(h + 1) * HD_C].astype(F32), gk_ref)
        if emit_kn:
            kn_ref[:, h * HD_C:(h + 1) * HD_C] = x
        if rope:
            x = _rope(x, tab_ref)
        ko_ref[:, h * HD_C:(h + 1) * HD_C] = x.astype(ko_ref.dtype)
    vo_ref[...] = v_ref[...].astype(vo_ref.dtype)


def gqa_prep(z, gq, gk, l, *, row0, rows, tab, emit_kn):
    bm = 512
    rope = tab is not None
    rb0 = row0 // bm
    nq, nk = HQ_C * HD_C, HKV_C * HD_C
    nqh = nq // 2
    in_specs = [
        pl.BlockSpec((bm, nqh), lambda i: (rb0 + i, Z_QC // nqh)),
        pl.BlockSpec((bm, nqh), lambda i: (rb0 + i, Z_QC // nqh + 1)),
        pl.BlockSpec((bm, nk), lambda i: (rb0 + i, Z_KC // nk)),
        pl.BlockSpec((bm, nk), lambda i: (rb0 + i, Z_VC // nk)),
        pl.BlockSpec((None, 1, HD_C), lambda i: (l, 0, 0)),
        pl.BlockSpec((None, 1, HD_C), lambda i: (l, 0, 0)),
    ]
    args = [z, z, z, z, gq.reshape(DEPTH, 1, HD_C), gk.reshape(DEPTH, 1, HD_C)]
    if rope:
        per = DEC_SEQ // bm
        in_specs.append(pl.BlockSpec((3, bm, 128), lambda i: (0, i % per, 0)))
        args.append(tab)
    out_shape = [jax.ShapeDtypeStruct((rows, nq), BF16),
                 jax.ShapeDtypeStruct((rows, nk), BF16),
                 jax.ShapeDtypeStruct((rows, nk), BF16)]
    out_specs = [pl.BlockSpec((bm, nq), lambda i: (i, 0)),
                 pl.BlockSpec((bm, nk), lambda i: (i, 0)),
                 pl.BlockSpec((bm, nk), lambda i: (i, 0))]
    if emit_kn:
        out_shape.append(jax.ShapeDtypeStruct((rows, nk), F32))
        out_specs.append(pl.BlockSpec((bm, nk), lambda i: (i, 0)))
    return pl.pallas_call(
        functools.partial(_gqa_prep_kernel, rope=rope, emit_kn=emit_kn),
        out_shape=out_shape,
        grid=(rows // bm,),
        in_specs=in_specs,
        out_specs=out_specs,
        compiler_params=_cparams(1),
        name="gqa_prep",
    )(*args)


def _attn_kernel(*refs, n_seg, seg_rows, hq, hkv, dqk, dv, tq):
    q_ref = refs[0]
    kv_refs = refs[1:1 + 2 * n_seg]
    o_ref = refs[1 + 2 * n_seg]
    k_sc, v_sc = refs[2 + 2 * n_seg:]
    g = hq // hkv
    nq = q_ref.shape[0] // tq
    for j in range(hkv):
        r = 0
        for s in range(n_seg):
            k_sc[j, r:r + seg_rows[s], :] = kv_refs[2 * s][:, j * dqk:(j + 1) * dqk].astype(BF16)
            v_sc[j, r:r + seg_rows[s], :] = kv_refs[2 * s + 1][:, j * dv:(j + 1) * dv].astype(BF16)
            r += seg_rows[s]

    def q_block(qb, carry):
        rows = pl.ds(pl.multiple_of(qb * tq, tq), tq)
        for h in range(hq):
            j = h // g
            q = q_ref[rows, h * dqk:(h + 1) * dqk]
            s_ = _dot_nt(q, k_sc[j])
            m = jnp.max(s_, axis=-1, keepdims=True)
            p = jnp.exp2(s_ - m)
            den = jnp.sum(p, axis=-1, keepdims=True)
            o = jnp.dot(p.astype(BF16), v_sc[j], preferred_element_type=F32) / den
            o_ref[rows, h * dv:(h + 1) * dv] = o.astype(o_ref.dtype)
        return carry

    if nq == 1:
        q_block(0, 0)
    else:
        lax.fori_loop(0, nq, q_block, 0)


def attention(q, segs, *, nb, q_rows, hq, hkv, dqk, dv, name):
    tq = 256
    in_specs = [pl.BlockSpec((q_rows, hq * dqk), lambda b: (b, 0))]
    args = [q]
    seg_rows = []
    for k_arr, k_spec, v_arr, v_spec, rows in segs:
        in_specs += [k_spec, v_spec]
        args += [k_arr, v_arr]
        seg_rows.append(rows)
    s_tot = sum(seg_rows)
    return pl.pallas_call(
        functools.partial(_attn_kernel, n_seg=len(segs), seg_rows=tuple(seg_rows), hq=hq, hkv=hkv,
                          dqk=dqk, dv=dv, tq=tq),
        out_shape=jax.ShapeDtypeStruct((nb * q_rows, hq * dv), BF16),
        grid=(nb,),
        in_specs=in_specs,
        out_specs=pl.BlockSpec((q_rows, hq * dv), lambda b: (b, 0)),
        scratch_shapes=[pltpu.VMEM((hkv, s_tot, dqk), BF16), pltpu.VMEM((hkv, s_tot, dv), BF16)],
        compiler_params=_cparams(1),
        name=name,
    )(*args)


def _rows_spec(rows, width, col_blk=0):
    return pl.BlockSpec((rows, width), lambda b: (b, col_blk))


def _gla_constants():
    c = GLA_CHUNK
    t = np.arange(c)[:, None]
    u = np.arange(c)[None, :]
    tril = (u <= t).astype(np.float32)
    hb = c // 2
    x = (t ^ u)[:hb, :hb]
    msb = np.where(x > 0, np.floor(np.log2(np.maximum(x, 1))), GLA_LEVELS - 1).astype(np.int32)
    return jnp.asarray(tril, BF16), jnp.asarray(msb)


def _split3(x):
    hi = x.astype(BF16)
    r = x - hi.astype(F32)
    mid = r.astype(BF16)
    lo = (r - mid.astype(F32)).astype(BF16)
    return [hi, mid, lo]


def _exp2_neg_abs(x):
    return jnp.exp2(-jnp.abs(x))


def _gla_kernel(q_ref, k_ref, v_ref, ga_ref, gk_ref, up_ref, bias_ref, ng_ref, tril_ref, msb_ref, *rest,
                n_chunks, has_s0, heads):
    rest = list(rest)
    s0_ref = rest.pop(0) if has_s0 else None
    y_ref = rest.pop(0)
    st_ref = None if has_s0 else rest.pop(0)
    c = GLA_CHUNK
    hb = c // 2
    nl = GLA_LEVELS
    dk = DK_A
    dv = DV_A
    row = lax.broadcasted_iota(jnp.int32, (c, 1), 0)

    def pass1(hd, rows):
        gk = gk_ref[rows, :].astype(BF16)
        las = []
        for d in range(2):
            zg = (jnp.dot(gk, up_ref[d, :, hd * dk:(hd + 1) * dk].astype(BF16), preferred_element_type=F32)
                  + bias_ref[d, :, hd * dk:(hd + 1) * dk])
            log_sig = jnp.minimum(zg, 0.0) - jnp.log1p(jnp.exp(-jnp.abs(zg)))
            las.append(log_sig * (LOG2E / GLA_GATE_NORM))
        parts = jnp.concatenate(_split3(las[0]) + _split3(las[1]), axis=1)
        cs = jnp.dot(tril_ref[...], parts, preferred_element_type=F32)
        cums = [cs[:, 0:dk] + cs[:, dk:2 * dk] + cs[:, 2 * dk:3 * dk],
                cs[:, 3 * dk:4 * dk] + cs[:, 4 * dk:5 * dk] + cs[:, 5 * dk:6 * dk]]
        tot_f = cums[0][c - 1:c, :]
        tot = jnp.concatenate([tot_f, cums[1][c - 1:c, :]], axis=1)
        k = k_ref[rows, hd * dk:(hd + 1) * dk].astype(F32)
        kd = jnp.concatenate([k * _exp2_neg_abs(tot_f - cums[0]), k * _exp2_neg_abs(cums[1] - las[1])], axis=1)
        kv = lax.dot_general(kd.astype(BF16), v_ref[rows, hd * dv:(hd + 1) * dv].astype(BF16),
                             (((0,), (0,)), ((), ())), preferred_element_type=F32)
        return las, cums, tot, kv

    def pass2(hd, rows, las, cums, tot, s_in):
        laf, lab = las
        cumf, cumb = cums
        cumbx = cumb - lab
        q = q_ref[rows, hd * dk:(hd + 1) * dk].astype(F32) * (DK_A ** -0.5)
        k = k_ref[rows, hd * dk:(hd + 1) * dk].astype(F32)
        v = v_ref[rows, hd * dv:(hd + 1) * dv].astype(BF16)
        msb = msb_ref[...]
        q2 = (2.0 * q).astype(BF16)
        kb = k.astype(BF16)
        scd = [_dot_nt(q2[0:hb], kb[0:hb]), _dot_nt(q2[hb:], kb[hb:])]
        a_lo = a_up = None
        for lvl in range(nl):
            b = 1 << lvl
            second = (row & b) != 0
            if lvl == 0:
                qq = q * jnp.exp2(jnp.where(second, laf, lab))
                kk = k
            elif lvl == 1:
                p = row & 3
                dq = jnp.where(p == 0, lab + pltpu.roll(lab, c - 1, 0),
                               jnp.where(p == 1, lab, jnp.where(p == 2, laf, laf + pltpu.roll(laf, 1, 0))))
                dkk = jnp.where(p == 0, pltpu.roll(laf, c - 1, 0), jnp.where(p == 3, pltpu.roll(lab, 1, 0), 0.0))
                qq = q * jnp.exp2(dq)
                kk = k * jnp.exp2(dkk)
            else:
                n = c // (2 * b)
                shp = (n, 2 * b, dk)
                cf3 = cumf.reshape(shp)
                cb3 = cumb.reshape(shp)
                ef = _exp2_neg_abs(cf3 - cf3[:, b - 1:b, :])
                eb = _exp2_neg_abs(cumbx.reshape(shp) - cb3[:, b - 1:b, :])
                if b >= SUBLANES:
                    eq = jnp.concatenate([eb[:, :b], ef[:, b:]], axis=1).reshape(c, dk)
                    ek = jnp.concatenate([ef[:, :b], eb[:, b:]], axis=1).reshape(c, dk)
                else:
                    ef = ef.reshape(c, dk)
                    eb = eb.reshape(c, dk)
                    eq = jnp.where(second, ef, eb)
                    ek = jnp.where(second, eb, ef)
                qq = q * eq
                kk = k * ek
            qq = qq.astype(BF16)
            kk = kk.astype(BF16)
            if lvl < nl - 1:
                for blk in range(2):
                    a = _dot_nt(qq[blk * hb:(blk + 1) * hb], kk[blk * hb:(blk + 1) * hb])
                    scd[blk] = jnp.where(msb == lvl, a, scd[blk])
            else:
                a_lo = _dot_nt(qq[hb:], kk[:hb])
                a_up = _dot_nt(qq[:hb], kk[hb:])
        sc = jnp.concatenate([jnp.concatenate([scd[0], a_up], axis=1),
                              jnp.concatenate([a_lo, scd[1]], axis=1)], axis=0)
        o = jnp.dot(sc.astype(BF16), v, preferred_element_type=F32)
        if s_in is not None:
            qd = jnp.concatenate([q * _exp2_neg_abs(cumf), q * _exp2_neg_abs(tot[:, dk:] - cumbx)], axis=1)
            o = o + jnp.dot(qd.astype(BF16), s_in.astype(BF16), preferred_element_type=F32)
        o = o * lax.rsqrt(jnp.mean(o * o, axis=-1, keepdims=True) + EPS) * ng_ref[...]
        ga = ga_ref[rows, hd * dv:(hd + 1) * dv].astype(F32)
        y_ref[rows, hd * dv:(hd + 1) * dv] = (o * (ga * jax.nn.sigmoid(ga))).astype(y_ref.dtype)

    if not has_s0:
        rows = slice(0, c)
        for hd in range(heads):
            las, cums, tot, kv = pass1(hd, rows)
            st_ref[0, hd] = kv[0:dk]
            st_ref[1, hd] = kv[dk:2 * dk]
            pass2(hd, rows, las, cums, tot, None)
        return

    la_sc, cum_sc, kv_sc, tot_sc, sin_sc = rest
    assert heads == 1

    def run_pass1(ci, carry):
        rows = pl.ds(pl.multiple_of(ci * c, c), c)
        las, cums, tot, kv = pass1(0, rows)
        for d in range(2):
            la_sc[d, rows, :] = las[d]
            cum_sc[d, rows, :] = cums[d]
        tot_sc[ci] = tot
        kv_sc[ci] = kv
        return carry

    lax.fori_loop(0, n_chunks, run_pass1, 0)

    eye = lax.broadcasted_iota(jnp.int32, (2 * dk, 2 * dk), 0) == lax.broadcasted_iota(
        jnp.int32, (2 * dk, 2 * dk), 1)

    def decay_col(ci):
        tot = jnp.broadcast_to(tot_sc[ci], (2 * dk, 2 * dk))
        return jnp.exp2(jnp.sum(jnp.where(eye, tot, 0.0), axis=1, keepdims=True))

    s = s0_ref[0]
    for ci in range(n_chunks):
        sin_sc[ci, 0:dk, :] = s
        if ci < n_chunks - 1:
            s = decay_col(ci)[0:dk] * s + kv_sc[ci, 0:dk, :]
    s = s0_ref[1]
    for ci in reversed(range(n_chunks)):
        sin_sc[ci, dk:2 * dk, :] = s
        if ci > 0:
            s = decay_col(ci)[dk:2 * dk] * s + kv_sc[ci, dk:2 * dk, :]

    def run_pass2(ci, carry):
        rows = pl.ds(pl.multiple_of(ci * c, c), c)
        pass2(0, rows, (la_sc[0, rows, :], la_sc[1, rows, :]), (cum_sc[0, rows, :], cum_sc[1, rows, :]),
              tot_sc[ci], sin_sc[ci])
        return carry

    lax.fori_loop(0, n_chunks, run_pass2, 0)


def gla(z, up_pad, bias, norm_g, tril, msb, l, *, row0, nb, seq, s0):
    rb0 = row0 // seq
    n_chunks = seq // GLA_CHUNK
    has_s0 = s0 is not None
    assert has_s0 or n_chunks == 1
    heads = 1 if has_s0 else H_A
    hb = GLA_CHUNK // 2
    if has_s0:
        grid = (nb, H_A)
        im = lambda f: (lambda b, h: f(b, h))
    else:
        grid = (nb,)
        im = lambda f: (lambda b: f(b, 0))
    nk, nv = heads * DK_A, heads * DV_A
    in_specs = [
        pl.BlockSpec((seq, nk), im(lambda b, h: (rb0 + b, Z_QA // nk + h))),
        pl.BlockSpec((seq, nk), im(lambda b, h: (rb0 + b, Z_KA // nk + h))),
        pl.BlockSpec((seq, nv), im(lambda b, h: (rb0 + b, Z_VA // nv + h))),
        pl.BlockSpec((seq, nv), im(lambda b, h: (rb0 + b, Z_GA // nv + h))),
        pl.BlockSpec((seq, 128), im(lambda b, h: (rb0 + b, Z_GK // 128))),
        pl.BlockSpec((None, 2, 128, nk), im(lambda b, h: (l, 0, 0, h))),
        pl.BlockSpec((None, 2, 1, nk), im(lambda b, h: (l, 0, 0, h))),
        pl.BlockSpec((None, 1, DV_A), im(lambda b, h: (l, 0, 0))),
        pl.BlockSpec((GLA_CHUNK, GLA_CHUNK), im(lambda b, h: (0, 0))),
        pl.BlockSpec((hb, hb), im(lambda b, h: (0, 0))),
    ]
    args = [z, z, z, z, z, up_pad, bias.reshape(DEPTH, 2, 1, H_A * DK_A),
            norm_g.reshape(DEPTH, 1, DV_A), tril, msb]
    out_shape = [jax.ShapeDtypeStruct((nb * seq, H_A * DV_A), BF16)]
    out_specs = [pl.BlockSpec((seq, nv), im(lambda b, h: (b, h)))]
    scratch = []
    if has_s0:
        in_specs.append(pl.BlockSpec((None, None, 2, None, DK_A, DV_A), im(lambda b, h: (b, l, 0, h, 0, 0))))
        args.append(s0)
        scratch = [pltpu.VMEM((2, seq, DK_A), F32), pltpu.VMEM((2, seq, DK_A), F32),
                   pltpu.VMEM((n_chunks, 2 * DK_A, DV_A), F32),
                   pltpu.VMEM((n_chunks, 1, 2 * DK_A), F32),
                   pltpu.VMEM((n_chunks, 2 * DK_A, DV_A), F32)]
    else:
        out_shape.append(jax.ShapeDtypeStruct((nb, 2, H_A, DK_A, DV_A), F32))
        out_specs.append(pl.BlockSpec((None, 2, H_A, DK_A, DV_A), im(lambda b, h: (b, 0, 0, 0, 0))))
    return pl.pallas_call(
        functools.partial(_gla_kernel, n_chunks=n_chunks, has_s0=has_s0, heads=heads),
        out_shape=out_shape,
        grid=grid,
        in_specs=in_specs,
        out_specs=out_specs,
        scratch_shapes=scratch,
        compiler_params=_cparams(len(grid)),
        name="gla",
    )(*args)


def kernel(x_prompt, x_sample, c, state_gla, cache_mla_ckv, cache_mla_krope, cache_gqa_k, cache_gqa_v,
           c_ctx, norm1_g, norm2_g, w_ada, b_ada, w_in, gla_gk_up, gla_gk_bias, gla_norm_g,
           mla_qa_norm_g, mla_w_qb, mla_kva_norm_g, mla_w_kvb, mla_q_norm_g, mla_k_norm_g,
           gqa_q_norm_g, gqa_k_norm_g, w_out_a, w_out_b, w_out_c, w_o,
           ffn_w_up, ffn_conv_w, ffn_conv_b, ffn_w_down):
    d = D_MODEL
    x = jnp.concatenate([x_prompt.reshape(T_P, d), x_sample.reshape(T_S, d)], axis=0)
    cvec = jnp.concatenate([c_ctx[None, :], c, jnp.zeros((MOD_ROWS - N_GROUPS, d), F32)], axis=0)
    mods = adaln_all(cvec, w_ada, b_ada)

    w_qb_pad = jnp.pad(mla_w_qb.reshape(DEPTH, Q_LORA_B, H_B, QK_B),
                       ((0, 0), (0, 0), (0, 0), (0, PAD_QK_B - QK_B))).reshape(DEPTH, Q_LORA_B, H_B * PAD_QK_B)
    q_g_pad = jnp.pad(mla_q_norm_g, ((0, 0), (0, PAD_QK_B - QK_B))).reshape(DEPTH, 1, PAD_QK_B)
    k_g_pad = jnp.pad(mla_k_norm_g, ((0, 0), (0, PAD_QK_B - QK_B))).reshape(DEPTH, 1, PAD_QK_B)
    up_pad = jnp.zeros((DEPTH, 2, 128, H_A * DK_A), F32)
    for dd in range(2):
        r0 = dd * GLA_GATE_RANK
        up_pad = up_pad.at[:, dd, r0:r0 + GLA_GATE_RANK, :].set(gla_gk_up[:, dd])
    tab_b = _rope_tables(DEC_SEQ, ROPE_B, 128)
    tab_c = _rope_tables(DEC_SEQ, HD_C, 128)
    tril, msb = _gla_constants()
    w_in_t = jnp.swapaxes(w_in, 1, 2)
    ck = cache_gqa_k.reshape(DEC_BATCH, DEPTH, PAST_LEN, HKV_C * HD_C)
    cv = cache_gqa_v.reshape(DEC_BATCH, DEPTH, PAST_LEN, HKV_C * HD_C)

    st_out, ckv_out, krope_out, kc_out, vc_out = [], [], [], [], []
    for l in range(DEPTH):
        h = norm_mod(x, norm1_g, mods, l, 0, 1)
        z = w_in_proj(h, w_in_t, l)

        ya_p, st = gla(z, up_pad, gla_gk_bias, gla_norm_g, tril, msb, l,
                       row0=0, nb=BATCH, seq=SEQ, s0=None)
        ya_s, = gla(z, up_pad, gla_gk_bias, gla_norm_g, tril, msb, l,
                    row0=T_P, nb=DEC_BATCH, seq=DEC_SEQ, s0=state_gla)

        bmk = 512
        qb_p = mla_q_proj(z, mla_qa_norm_g, w_qb_pad, q_g_pad, l, row0=0, rows=T_P, tab=None)
        qb_s = mla_q_proj(z, mla_qa_norm_g, w_qb_pad, q_g_pad, l, row0=T_P, rows=T_S, tab=tab_b)

        def tail_specs(row0):
            rb = row0 // bmk
            return (pl.BlockSpec((bmk, KV_LORA_B), lambda i: (rb + i, Z_CKV // KV_LORA_B)),
                    pl.BlockSpec((bmk, 128), lambda i: (rb + i, Z_KROPE // 128)))

        c_spec, kr_spec = tail_specs(0)
        kb_p, vb_p, ckv_p = mla_kv_proj(z, c_spec, z, kr_spec, mla_kva_norm_g, mla_w_kvb, k_g_pad, l,
                                        rows=T_P, bm=bmk, norm_in=True, tab=None)
        c_spec, kr_spec = tail_specs(T_P)
        kb_s, vb_s, _ = mla_kv_proj(z, c_spec, z, kr_spec, mla_kva_norm_g, mla_w_kvb, k_g_pad, l,
                                    rows=T_S, bm=bmk, norm_in=True, tab=tab_b)
        kb_c, vb_c = mla_kv_proj(
            cache_mla_ckv, pl.BlockSpec((None, None, PAST_LEN, KV_LORA_B), lambda i: (i, l, 0, 0)),
            cache_mla_krope, pl.BlockSpec((None, None, PAST_LEN, ROPE_B), lambda i: (i, l, 0, 0)),
            mla_kva_norm_g, mla_w_kvb, k_g_pad, l, rows=DEC_BATCH * PAST_LEN, bm=PAST_LEN, norm_in=False, tab=None)
        nkb, nvb = H_B * PAD_QK_B, H_B * V_B
        yb_p = attention(qb_p, [(kb_p, _rows_spec(SEQ, nkb), vb_p, _rows_spec(SEQ, nvb), SEQ)],
                         nb=BATCH, q_rows=SEQ, hq=H_B, hkv=H_B, dqk=PAD_QK_B, dv=V_B, name="mla_attn_p")
        yb_s = attention(qb_s, [(kb_s, _rows_spec(DEC_SEQ, nkb), vb_s, _rows_spec(DEC_SEQ, nvb), DEC_SEQ),
                                (kb_c, _rows_spec(PAST_LEN, nkb), vb_c, _rows_spec(PAST_LEN, nvb), PAST_LEN)],
                         nb=DEC_BATCH, q_rows=DEC_SEQ, hq=H_B, hkv=H_B, dqk=PAD_QK_B, dv=V_B, name="mla_attn_s")

        qc_p, kc_p, vc_p, kn_p = gqa_prep(z, gqa_q_norm_g, gqa_k_norm_g, l, row0=0, rows=T_P,
                                          tab=None, emit_kn=True)
        qc_s, kc_s, vc_s = gqa_prep(z, gqa_q_norm_g, gqa_k_norm_g, l, row0=T_P, rows=T_S,
                                    tab=tab_c, emit_kn=False)
        nkc = HKV_C * HD_C
        yc_p = attention(qc_p, [(kc_p, _rows_spec(SEQ, nkc), vc_p, _rows_spec(SEQ, nkc), SEQ)],
                         nb=BATCH, q_rows=SEQ, hq=HQ_C, hkv=HKV_C, dqk=HD_C, dv=HD_C, name="gqa_attn_p")
        cache_spec = pl.BlockSpec((None, None, PAST_LEN, nkc), lambda b: (b, l, 0, 0))
        yc_s = attention(qc_s, [(kc_s, _rows_spec(DEC_SEQ, nkc), vc_s, _rows_spec(DEC_SEQ, nkc), DEC_SEQ),
                                (ck, cache_spec, cv, cache_spec, PAST_LEN)],
                         nb=DEC_BATCH, q_rows=DEC_SEQ, hq=HQ_C, hkv=HKV_C, dqk=HD_C, dv=HD_C, name="gqa_attn_s")

        m = gated_merge((ya_p, ya_s, yb_p, yb_s, yc_p, yc_s), z, w_out_a, w_out_b, w_out_c, l)
        x, h2 = w_o_norm(m, w_o, l, x, mods, norm2_g)
        act = up_conv(h2, ffn_w_up, ffn_conv_w, ffn_conv_b, l)
        x = matmul_residual(act, ffn_w_down, l, x, mods, 5, bm=512, bn=512, name="ffn_down")

        st_out.append(st)
        ckv_out.append(ckv_p.reshape(BATCH, SEQ, KV_LORA_B))
        krope_out.append(z[:T_P, Z_KROPE:Z_KROPE + ROPE_B].astype(F32).reshape(BATCH, SEQ, ROPE_B))
        kc_out.append(kn_p.reshape(BATCH, SEQ, HKV_C, HD_C))
        vc_out.append(z[:T_P, Z_VC:Z_VC + nkc].astype(F32).reshape(BATCH, SEQ, HKV_C, HD_C))

    y_p = x[:T_P].reshape(BATCH, SEQ, d)
    y_s = x[T_P:].reshape(DEC_BATCH, DEC_SEQ, d)
    return (y_p, y_s, jnp.stack(st_out, axis=1), jnp.stack(ckv_out, axis=1), jnp.stack(krope_out, axis=1),
            jnp.stack(kc_out, axis=1), jnp.stack(vc_out, axis=1))
```

```python
import functools

import numpy as np
import jax
import jax.numpy as jnp
from jax import lax
from jax.experimental import pallas as pl
from jax.experimental.pallas import tpu as pltpu

F32 = jnp.float32
BF16 = jnp.bfloat16

D_MODEL = 2048
BATCH = 16
SEQ = 256
DEPTH = 4
DEC_BATCH = 2
DEC_SEQ = 1024
PAST_LEN = 512
GRID_W = 64
ROPE_THETA = 10000.0
EPS = 1e-6
H_A, DK_A, DV_A = 4, 128, 256
GLA_GATE_RANK = 16
GLA_GATE_NORM = 16.0
H_B, NOPE_B, ROPE_B, V_B = 8, 128, 64, 128
QK_B = NOPE_B + ROPE_B
Q_LORA_B = 512
KV_LORA_B = 256
HQ_C, HKV_C, HD_C = 8, 2, 128
D_FF = 5632

T_P = BATCH * SEQ
T_S = DEC_BATCH * DEC_SEQ
T_ALL = T_P + T_S
N_GROUPS = 1 + DEC_BATCH
SUBLANES = 8
MOD_ROWS = SUBLANES

_O_QA, _O_KA, _O_VA, _O_GA = 0, 512, 1024, 2048
_O_GKA = 3072
_O_QLORA = 3104
_O_KVA = 3616
_O_QC, _O_KC, _O_VC = 3936, 4960, 5216
_O_GATES = 5472
N_IN = 11616
W_IN_BN = 512
W_IN_RANGES = ((_O_GKA, 1), (_O_QC, 15), (_O_QA, 6), (_O_QLORA, 2))
N_Z = W_IN_BN * sum(nb for _, nb in W_IN_RANGES)
Z_GK = 0
Z_QC, Z_KC, Z_VC, Z_GATES = 512, 1536, 1792, 2048
Z_QA, Z_KA, Z_VA, Z_GA = 8192, 8704, 9216, 10240
Z_QLORA, Z_CKV, Z_KROPE = 11264, 11776, 12032
PAD_QK_B = 256

GLA_CHUNK = 256
GLA_LEVELS = 8

VMEM_LIMIT = 56 * 1024 * 1024
LOG2E = 1.4426950408889634


def _cparams(n_axes):
    return pltpu.CompilerParams(
        dimension_semantics=("arbitrary",) * n_axes, vmem_limit_bytes=VMEM_LIMIT)


def _group_of_block(i, bm):
    n_p = T_P // bm
    per = DEC_SEQ // bm
    return jnp.where(i < n_p, 0, 1 + (i - n_p) // per)


def _adaln_kernel(c_ref, w_ref, b_ref, o_ref):
    c = c_ref[...]
    a = (c * jax.nn.sigmoid(c)).astype(BF16)
    o_ref[...] = jnp.dot(a, w_ref[...].astype(BF16), preferred_element_type=F32) + b_ref[...]


def adaln_all(cvec, w_ada, b_ada):
    bn = 1024
    n = 6 * D_MODEL
    return pl.pallas_call(
        _adaln_kernel,
        out_shape=jax.ShapeDtypeStruct((DEPTH, MOD_ROWS, n), F32),
        grid=(DEPTH, n // bn),
        in_specs=[
            pl.BlockSpec((MOD_ROWS, D_MODEL), lambda l, j: (0, 0)),
            pl.BlockSpec((None, D_MODEL, bn), lambda l, j: (l, 0, j)),
            pl.BlockSpec((None, 1, bn), lambda l, j: (l, 0, j)),
        ],
        out_specs=pl.BlockSpec((None, MOD_ROWS, bn), lambda l, j: (l, 0, j)),
        compiler_params=_cparams(2),
        name="adaln",
    )(cvec, w_ada, b_ada.reshape(DEPTH, 1, n))


def _norm_mod_kernel(x_ref, g_ref, sh_ref, sc_ref, o_ref, *, bm):
    grp = _group_of_block(pl.program_id(0), bm)
    x = x_ref[...]
    y = x * lax.rsqrt(jnp.mean(x * x, axis=-1, keepdims=True) + EPS) * g_ref[...]
    sh = sh_ref[pl.ds(grp, 1), :]
    sc = sc_ref[pl.ds(grp, 1), :]
    o_ref[...] = (y * (1.0 + sc) + sh).astype(o_ref.dtype)


def norm_mod(x, g, mods, l, which_shift, which_scale):
    bm = 512
    nd = D_MODEL
    return pl.pallas_call(
        functools.partial(_norm_mod_kernel, bm=bm),
        out_shape=jax.ShapeDtypeStruct((T_ALL, nd), BF16),
        grid=(T_ALL // bm,),
        in_specs=[
            pl.BlockSpec((bm, nd), lambda i: (i, 0)),
            pl.BlockSpec((None, 1, nd), lambda i: (l, 0, 0)),
            pl.BlockSpec((None, MOD_ROWS, nd), lambda i: (l, 0, which_shift)),
            pl.BlockSpec((None, MOD_ROWS, nd), lambda i: (l, 0, which_scale)),
        ],
        out_specs=pl.BlockSpec((bm, nd), lambda i: (i, 0)),
        compiler_params=_cparams(1),
        name="norm_mod",
    )(x, g.reshape(DEPTH, 1, nd), mods, mods)


def _dot_nt(a, b):
    return lax.dot_general(a, b, (((1,), (1,)), ((), ())), preferred_element_type=F32)


ROW_BLK = 1024


def _w_in_kernel(h_ref, wt_ref, o_ref, wb_ref):
    wb_ref[...] = wt_ref[0].T.astype(BF16)

    for r in range(T_ALL // ROW_BLK):
        rows = slice(r * ROW_BLK, (r + 1) * ROW_BLK)
        o_ref[rows, :] = jnp.dot(h_ref[rows, :], wb_ref[...], preferred_element_type=F32).astype(o_ref.dtype)


def _w_in_first_col(j):
    col8 = None
    start = sum(nb for _, nb in W_IN_RANGES)
    for c0, nb in reversed(W_IN_RANGES):
        assert c0 % 8 == 0
        start -= nb
        here = c0 // 8 + (j - start) * (W_IN_BN // 8)
        col8 = here if col8 is None else jnp.where(j < start + nb, here, col8)
    return col8 * 8


def w_in_proj(h, w_in_t, l):
    bn = W_IN_BN
    return pl.pallas_call(
        _w_in_kernel,
        out_shape=jax.ShapeDtypeStruct((T_ALL, N_Z), BF16),
        grid=(N_Z // bn,),
        in_specs=[
            pl.BlockSpec((T_ALL, D_MODEL), lambda j: (0, 0), pipeline_mode=pl.Buffered(1)),
            pl.BlockSpec((pl.Element(1), pl.Element(bn), pl.Element(D_MODEL)),
                         lambda j: (l, _w_in_first_col(j), 0)),
        ],
        out_specs=pl.BlockSpec((T_ALL, bn), lambda j: (0, j)),
        scratch_shapes=[pltpu.VMEM((D_MODEL, bn), BF16)],
        compiler_params=_cparams(1),
        name="w_in",
    )(h, w_in_t)


def _mm_res_kernel(a_ref, w_ref, x_ref, gate_ref, o_ref, wb_ref, *, bm):
    @pl.when(pl.program_id(1) == 0)
    def _():
        wb_ref[...] = w_ref[...].astype(BF16)

    grp = _group_of_block(pl.program_id(1), bm)
    acc = jnp.dot(a_ref[...], wb_ref[...], preferred_element_type=F32)
    o_ref[...] = x_ref[...] + gate_ref[pl.ds(grp, 1), :] * acc


def matmul_residual(a, w, l, x, mods, which_gate, *, bm, bn, name):
    m, k = a.shape
    n = w.shape[-1]
    gate_blk = which_gate * (D_MODEL // bn)
    return pl.pallas_call(
        functools.partial(_mm_res_kernel, bm=bm),
        out_shape=jax.ShapeDtypeStruct((m, n), F32),
        grid=(n // bn, m // bm),
        in_specs=[
            pl.BlockSpec((bm, k), lambda j, i: (i, 0)),
            pl.BlockSpec((None, k, bn), lambda j, i: (l, 0, j)),
            pl.BlockSpec((bm, bn), lambda j, i: (i, j)),
            pl.BlockSpec((None, MOD_ROWS, bn), lambda j, i: (l, 0, gate_blk + j)),
        ],
        out_specs=pl.BlockSpec((bm, bn), lambda j, i: (i, j)),
        scratch_shapes=[pltpu.VMEM((k, bn), BF16)],
        compiler_params=_cparams(2),
        name=name,
    )(a, w, x, mods)


def _w_o_norm_kernel(m_ref, w_ref, x_ref, gate_ref, g_ref, sh_ref, sc_ref, xo_ref, h_ref, wb_ref, *, bm):
    i = pl.program_id(0)

    @pl.when(i == 0)
    def _():
        wb_ref[...] = w_ref[...].astype(BF16)

    grp = _group_of_block(i, bm)
    acc = jnp.dot(m_ref[...], wb_ref[...], preferred_element_type=F32)
    x = x_ref[...] + gate_ref[pl.ds(grp, 1), :] * acc
    xo_ref[...] = x
    y = x * lax.rsqrt(jnp.mean(x * x, axis=-1, keepdims=True) + EPS) * g_ref[...]
    h_ref[...] = (y * (1.0 + sc_ref[pl.ds(grp, 1), :]) + sh_ref[pl.ds(grp, 1), :]).astype(h_ref.dtype)


def w_o_norm(m, w_o, l, x, mods, norm_g):
    bm = 512
    nd = D_MODEL

    def mod_spec(which):
        return pl.BlockSpec((None, MOD_ROWS, nd), lambda i: (l, 0, which))

    return pl.pallas_call(
        functools.partial(_w_o_norm_kernel, bm=bm),
        out_shape=[jax.ShapeDtypeStruct((T_ALL, nd), F32), jax.ShapeDtypeStruct((T_ALL, nd), BF16)],
        grid=(T_ALL // bm,),
        in_specs=[
            pl.BlockSpec((bm, nd), lambda i: (i, 0)),
            pl.BlockSpec((None, nd, nd), lambda i: (l, 0, 0), pipeline_mode=pl.Buffered(1)),
            pl.BlockSpec((bm, nd), lambda i: (i, 0)),
            mod_spec(2),
            pl.BlockSpec((None, 1, nd), lambda i: (l, 0, 0)),
            mod_spec(3), mod_spec(4),
        ],
        out_specs=[pl.BlockSpec((bm, nd), lambda i: (i, 0)), pl.BlockSpec((bm, nd), lambda i: (i, 0))],
        scratch_shapes=[pltpu.VMEM((nd, nd), BF16)],
        compiler_params=_cparams(1),
        name="w_o_norm",
    )(m, w_o, x, mods, norm_g.reshape(DEPTH, 1, nd), mods, mods)


def _merge_kernel(yap_ref, yas_ref, ybp_ref, ybs_ref, ycp_ref, ycs_ref, wa_ref, wb_ref, wc_ref,
                  ga_ref, gb_ref, gc_ref, o_ref, wab_ref, wbb_ref, wcb_ref, *, bm):
    i = pl.program_id(1)

    @pl.when(i == 0)
    def _():
        wab_ref[...] = wa_ref[...].astype(BF16)
        wbb_ref[...] = wb_ref[...].astype(BF16)
        wcb_ref[...] = wc_ref[...].astype(BF16)

    is_ctx = i < T_P // bm

    def branch(yp_ref, ys_ref, w_ref, g_ref):
        y = jnp.where(is_ctx, yp_ref[...], ys_ref[...])
        return jax.nn.sigmoid(g_ref[...].astype(F32)) * jnp.dot(y, w_ref[...], preferred_element_type=F32)

    m = (branch(yap_ref, yas_ref, wab_ref, ga_ref) + branch(ybp_ref, ybs_ref, wbb_ref, gb_ref)
         + branch(ycp_ref, ycs_ref, wcb_ref, gc_ref))
    o_ref[...] = m.astype(o_ref.dtype)


def gated_merge(ys, z, w_out_a, w_out_b, w_out_c, l):
    bm, bn = 512, 1024
    k = 1024
    gate0 = Z_GATES // bn
    per = D_MODEL // bn
    n_p = T_P // bm
    yp_spec = pl.BlockSpec((bm, k), lambda j, i: (jnp.minimum(i, n_p - 1), 0))
    ys_spec = pl.BlockSpec((bm, k), lambda j, i: (jnp.maximum(i - n_p, 0), 0))
    w_spec = pl.BlockSpec((None, k, bn), lambda j, i: (l, 0, j))

    def g_spec(br):
        return pl.BlockSpec((bm, bn), lambda j, i: (i, gate0 + br * per + j))

    return pl.pallas_call(
        functools.partial(_merge_kernel, bm=bm),
        out_shape=jax.ShapeDtypeStruct((T_ALL, D_MODEL), BF16),
        grid=(D_MODEL // bn, T_ALL // bm),
        in_specs=[yp_spec, ys_spec] * 3 + [w_spec] * 3 + [g_spec(0), g_spec(1), g_spec(2)],
        out_specs=pl.BlockSpec((bm, bn), lambda j, i: (i, j)),
        scratch_shapes=[pltpu.VMEM((k, bn), BF16)] * 3,
        compiler_params=_cparams(2),
        name="gated_merge",
    )(*ys, w_out_a, w_out_b, w_out_c, z, z, z)


def _up_conv_kernel(h_ref, wa_ref, wg_ref, cwa_ref, cwg_ref, cba_ref, cbg_ref, o_ref, wab_ref, wgb_ref,
                    *sc_refs):
    wab_ref[...] = wa_ref[...].astype(BF16)
    wgb_ref[...] = wg_ref[...].astype(BF16)
    bm = ROW_BLK
    nt = bm // SUBLANES
    bn = o_ref.shape[1]
    sub = lax.broadcasted_iota(jnp.int32, (1, SUBLANES, 1), 1)
    for sc_ref in sc_refs:
        sc_ref[0:SUBLANES, :] = jnp.zeros((SUBLANES, bn), F32)
        sc_ref[SUBLANES + bm:2 * SUBLANES + bm, :] = jnp.zeros((SUBLANES, bn), F32)

    def taps(cw_ref, cb_ref):
        return [jnp.broadcast_to(cw_ref[k:k + 1, :], (SUBLANES, bn))[None] for k in range(3)] + [
            jnp.broadcast_to(cb_ref[...], (SUBLANES, bn))[None]]

    taps_a = taps(cwa_ref, cba_ref)
    taps_g = taps(cwg_ref, cbg_ref)

    for r in range(T_ALL // bm):
        seq = SEQ if r < T_P // bm else DEC_SEQ
        tps = seq // SUBLANES
        n_seq = bm // seq
        h = h_ref[r * bm:(r + 1) * bm, :]

        def conv(w_ref, tp, sc_ref):
            u = jnp.dot(h, w_ref[...], preferred_element_type=F32)
            sc_ref[SUBLANES:SUBLANES + bm, :] = u
            prev = sc_ref[SUBLANES - 1:SUBLANES - 1 + bm, :].reshape(nt, SUBLANES, bn)
            nxt = sc_ref[SUBLANES + 1:SUBLANES + 1 + bm, :].reshape(nt, SUBLANES, bn)
            if n_seq > 1:
                prev = jnp.concatenate(
                    [p for s in range(n_seq) for p in (
                        jnp.where(sub == 0, 0.0, prev[s * tps:s * tps + 1]), prev[s * tps + 1:(s + 1) * tps])],
                    axis=0)
                nxt = jnp.concatenate(
                    [p for s in range(n_seq) for p in (
                        nxt[s * tps:(s + 1) * tps - 1],
                        jnp.where(sub == SUBLANES - 1, 0.0, nxt[(s + 1) * tps - 1:(s + 1) * tps]))],
                    axis=0)
            return prev * tp[0] + u.reshape(nt, SUBLANES, bn) * tp[1] + nxt * tp[2] + tp[3]

        a = conv(wab_ref, taps_a, sc_refs[0])
        g = conv(wgb_ref, taps_g, sc_refs[1])
        act = (g * jax.nn.sigmoid(g) * a).reshape(bm, bn)
        o_ref[r * bm:(r + 1) * bm, :] = act.astype(o_ref.dtype)


def up_conv(h2, w_up, conv_w, conv_b, l):
    bn = 256
    nb = D_FF // bn
    cb = conv_b.reshape(DEPTH, 1, 2 * D_FF)
    return pl.pallas_call(
        _up_conv_kernel,
        out_shape=jax.ShapeDtypeStruct((T_ALL, D_FF), BF16),
        grid=(nb,),
        in_specs=[
            pl.BlockSpec((T_ALL, D_MODEL), lambda j: (0, 0), pipeline_mode=pl.Buffered(1)),
            pl.BlockSpec((None, D_MODEL, bn), lambda j: (l, 0, j)),
            pl.BlockSpec((None, D_MODEL, bn), lambda j: (l, 0, nb + j)),
            pl.BlockSpec((None, 3, bn), lambda j: (l, 0, j)),
            pl.BlockSpec((None, 3, bn), lambda j: (l, 0, nb + j)),
            pl.BlockSpec((None, 1, bn), lambda j: (l, 0, j)),
            pl.BlockSpec((None, 1, bn), lambda j: (l, 0, nb + j)),
        ],
        out_specs=pl.BlockSpec((T_ALL, bn), lambda j: (0, j)),
        scratch_shapes=[pltpu.VMEM((D_MODEL, bn), BF16)] * 2
        + [pltpu.VMEM((ROW_BLK + 2 * SUBLANES, bn), F32)] * 2,
        compiler_params=_cparams(1),
        name="up_conv",
    )(h2, w_up, w_up, conv_w, conv_w, cb, cb)


def _rope_tables(n_tok, rot_dim, width):
    rows = n_tok // GRID_W
    row = jnp.repeat(jnp.arange(rows, dtype=F32), GRID_W)
    col = jnp.tile(jnp.arange(GRID_W, dtype=F32), rows)
    half = rot_dim // 2
    inv = jnp.power(ROPE_THETA, -jnp.arange(half // 2, dtype=F32) * (2.0 / half))
    ang = jnp.concatenate([row[:, None] * inv, col[:, None] * inv], axis=-1)
    cos, sin = jnp.cos(ang), jnp.sin(ang)
    zero = jnp.zeros_like(sin)
    c2 = jnp.stack([cos, cos], axis=-1).reshape(n_tok, rot_dim)
    s_even = jnp.stack([-sin, zero], axis=-1).reshape(n_tok, rot_dim)
    s_odd = jnp.stack([zero, sin], axis=-1).reshape(n_tok, rot_dim)
    pad = ((0, 0), (0, width - rot_dim))
    return jnp.stack([jnp.pad(c2, pad), jnp.pad(s_even, pad), jnp.pad(s_odd, pad)], axis=0)


def _rope(x, tab_ref):
    w = x.shape[-1]
    return (x * tab_ref[0] + pltpu.roll(x, w - 1, 1) * tab_ref[1] + pltpu.roll(x, 1, 1) * tab_ref[2])


def _mla_q_kernel(ql_ref, g1_ref, w_ref, g2_ref, *rest, rope):
    if rope:
        tab_ref, o_ref, wb_ref = rest
    else:
        o_ref, wb_ref = rest

    @pl.when(pl.program_id(0) == 0)
    def _():
        wb_ref[...] = w_ref[...].astype(BF16)

    x = ql_ref[...].astype(F32)
    xn = x * lax.rsqrt(jnp.mean(x * x, axis=-1, keepdims=True) + EPS) * g1_ref[...]
    q = jnp.dot(xn.astype(BF16), wb_ref[...], preferred_element_type=F32)
    scale = QK_B ** -0.5 * LOG2E
    for h in range(H_B):
        lo = h * PAD_QK_B
        qn = q[:, lo:lo + NOPE_B]
        qr = q[:, lo + NOPE_B:lo + PAD_QK_B]
        ms = (jnp.sum(qn * qn, axis=-1, keepdims=True) + jnp.sum(qr * qr, axis=-1, keepdims=True)) / QK_B
        r = lax.rsqrt(ms + EPS) * scale
        qn = qn * r * g2_ref[:, 0:NOPE_B]
        qr = qr * r * g2_ref[:, NOPE_B:PAD_QK_B]
        if rope:
            qr = _rope(qr, tab_ref)
        o_ref[:, lo:lo + NOPE_B] = qn.astype(o_ref.dtype)
        o_ref[:, lo + NOPE_B:lo + PAD_QK_B] = qr.astype(o_ref.dtype)


def mla_q_proj(z, qa_g, w_qb_pad, q_g_pad, l, *, row0, rows, tab):
    bm = 512
    rope = tab is not None
    kq = Q_LORA_B
    n = H_B * PAD_QK_B
    rb0 = row0 // bm
    in_specs = [
        pl.BlockSpec((bm, kq), lambda i: (rb0 + i, Z_QLORA // kq)),
        pl.BlockSpec((None, 1, kq), lambda i: (l, 0, 0)),
        pl.BlockSpec((None, kq, n), lambda i: (l, 0, 0)),
        pl.BlockSpec((None, 1, PAD_QK_B), lambda i: (l, 0, 0)),
    ]
    args = [z, qa_g.reshape(DEPTH, 1, kq), w_qb_pad, q_g_pad]
    if rope:
        per = DEC_SEQ // bm
        in_specs.append(pl.BlockSpec((3, bm, 128), lambda i: (0, i % per, 0)))
        args.append(tab)
    return pl.pallas_call(
        functools.partial(_mla_q_kernel, rope=rope),
        out_shape=jax.ShapeDtypeStruct((rows, n), BF16),
        grid=(rows // bm,),
        in_specs=in_specs,
        out_specs=pl.BlockSpec((bm, n), lambda i: (i, 0)),
        scratch_shapes=[pltpu.VMEM((kq, n), BF16)],
        compiler_params=_cparams(1),
        name="mla_q_proj",
    )(*args)


def _mla_kv_kernel(c_ref, kr_ref, g1_ref, w_ref, g2_ref, *rest, norm_in, rope):
    rest = list(rest)
    tab_ref = rest.pop(0) if rope else None
    if norm_in:
        k_ref, v_ref, ckv_ref, wb_ref = rest
    else:
        k_ref, v_ref, wb_ref = rest

    @pl.when(pl.program_id(0) == 0)
    def _():
        wb_ref[...] = w_ref[...].astype(BF16)

    c = c_ref[...].astype(F32)
    if norm_in:
        c = c * lax.rsqrt(jnp.mean(c * c, axis=-1, keepdims=True) + EPS) * g1_ref[...]
        ckv_ref[...] = c
    kv = jnp.dot(c.astype(BF16), wb_ref[...], preferred_element_type=F32)
    kr = kr_ref[...].astype(F32)
    if kr.shape[-1] == 128:
        lane = lax.broadcasted_iota(jnp.int32, kr.shape, 1)
        kr = jnp.where(lane < ROPE_B, kr, 0.0)
    else:
        kr = jnp.concatenate([kr, jnp.zeros_like(kr)], axis=-1)
    kr_ss = jnp.sum(kr * kr, axis=-1, keepdims=True)
    for h in range(H_B):
        kn = kv[:, h * 256:h * 256 + NOPE_B]
        v = kv[:, h * 256 + NOPE_B:(h + 1) * 256]
        r = lax.rsqrt((jnp.sum(kn * kn, axis=-1, keepdims=True) + kr_ss) / QK_B + EPS)
        krh = kr * r * g2_ref[:, NOPE_B:PAD_QK_B]
        if rope:
            krh = _rope(krh, tab_ref)
        k_ref[:, h * PAD_QK_B:h * PAD_QK_B + NOPE_B] = (kn * r * g2_ref[:, 0:NOPE_B]).astype(k_ref.dtype)
        k_ref[:, h * PAD_QK_B + NOPE_B:(h + 1) * PAD_QK_B] = krh.astype(k_ref.dtype)
        v_ref[:, h * V_B:(h + 1) * V_B] = v.astype(v_ref.dtype)


def mla_kv_proj(c_arr, c_spec, kr_arr, kr_spec, kva_g, w_kvb, k_g_pad, l, *, rows, bm, norm_in, tab):
    rope = tab is not None
    kc = KV_LORA_B
    n = H_B * (NOPE_B + V_B)
    in_specs = [
        c_spec, kr_spec,
        pl.BlockSpec((None, 1, kc), lambda i: (l, 0, 0)),
        pl.BlockSpec((None, kc, n), lambda i: (l, 0, 0)),
        pl.BlockSpec((None, 1, PAD_QK_B), lambda i: (l, 0, 0)),
    ]
    args = [c_arr, kr_arr, kva_g.reshape(DEPTH, 1, kc), w_kvb, k_g_pad]
    if rope:
        per = DEC_SEQ // bm
        in_specs.append(pl.BlockSpec((3, bm, 128), lambda i: (0, i % per, 0)))
        args.append(tab)
    out_shape = [jax.ShapeDtypeStruct((rows, H_B * PAD_QK_B), BF16),
                 jax.ShapeDtypeStruct((rows, H_B * V_B), BF16)]
    out_specs = [pl.BlockSpec((bm, H_B * PAD_QK_B), lambda i: (i, 0)),
                 pl.BlockSpec((bm, H_B * V_B), lambda i: (i, 0))]
    if norm_in:
        out_shape.append(jax.ShapeDtypeStruct((rows, kc), F32))
        out_specs.append(pl.BlockSpec((bm, kc), lambda i: (i, 0)))
    return pl.pallas_call(
        functools.partial(_mla_kv_kernel, norm_in=norm_in, rope=rope),
        out_shape=out_shape,
        grid=(rows // bm,),
        in_specs=in_specs,
        out_specs=out_specs,
        scratch_shapes=[pltpu.VMEM((kc, n), BF16)],
        compiler_params=_cparams(1),
        name="mla_kv_proj",
    )(*args)


def _gqa_prep_kernel(qlo_ref, qhi_ref, k_ref, v_ref, gq_ref, gk_ref, *rest, rope, emit_kn):
    rest = list(rest)
    tab_ref = rest.pop(0) if rope else None
    qo_ref, ko_ref, vo_ref = rest[:3]
    kn_ref = rest[3] if emit_kn else None
    scale = HD_C ** -0.5 * LOG2E
    half = HQ_C // 2

    def norm(x, g_ref):
        return x * lax.rsqrt(jnp.mean(x * x, axis=-1, keepdims=True) + EPS) * g_ref[...]

    for h in range(HQ_C):
        q_ref, hh = (qlo_ref, h) if h < half else (qhi_ref, h - half)
        x = norm(q_ref[:, hh * HD_C:(hh + 1) * HD_C].astype(F32), gq_ref)
        if rope:
            x = _rope(x, tab_ref)
        qo_ref[:, h * HD_C:(h + 1) * HD_C] = (x * scale).astype(qo_ref.dtype)
    for h in range(HKV_C):
        x = norm(k_ref[:, h * HD_C:(h + 1) * HD_C].astype(F32), gk_ref)
        if emit_kn:
            kn_ref[:, h * HD_C:(h + 1) * HD_C] = x
        if rope:
            x = _rope(x, tab_ref)
        ko_ref[:, h * HD_C:(h + 1) * HD_C] = x.astype(ko_ref.dtype)
    vo_ref[...] = v_ref[...].astype(vo_ref.dtype)


def gqa_prep(z, gq, gk, l, *, row0, rows, tab, emit_kn):
    bm = 512
    rope = tab is not None
    rb0 = row0 // bm
    nq, nk = HQ_C * HD_C, HKV_C * HD_C
    nqh = nq // 2
    in_specs = [
        pl.BlockSpec((bm, nqh), lambda i: (rb0 + i, Z_QC // nqh)),
        pl.BlockSpec((bm, nqh), lambda i: (rb0 + i, Z_QC // nqh + 1)),
        pl.BlockSpec((bm, nk), lambda i: (rb0 + i, Z_KC // nk)),
        pl.BlockSpec((bm, nk), lambda i: (rb0 + i, Z_VC // nk)),
        pl.BlockSpec((None, 1, HD_C), lambda i: (l, 0, 0)),
        pl.BlockSpec((None, 1, HD_C), lambda i: (l, 0, 0)),
    ]
    args = [z, z, z, z, gq.reshape(DEPTH, 1, HD_C), gk.reshape(DEPTH, 1, HD_C)]
    if rope:
        per = DEC_SEQ // bm
        in_specs.append(pl.BlockSpec((3, bm, 128), lambda i: (0, i % per, 0)))
        args.append(tab)
    out_shape = [jax.ShapeDtypeStruct((rows, nq), BF16),
                 jax.ShapeDtypeStruct((rows, nk), BF16),
                 jax.ShapeDtypeStruct((rows, nk), BF16)]
    out_specs = [pl.BlockSpec((bm, nq), lambda i: (i, 0)),
                 pl.BlockSpec((bm, nk), lambda i: (i, 0)),
                 pl.BlockSpec((bm, nk), lambda i: (i, 0))]
    if emit_kn:
        out_shape.append(jax.ShapeDtypeStruct((rows, nk), F32))
        out_specs.append(pl.BlockSpec((bm, nk), lambda i: (i, 0)))
    return pl.pallas_call(
        functools.partial(_gqa_prep_kernel, rope=rope, emit_kn=emit_kn),
        out_shape=out_shape,
        grid=(rows // bm,),
        in_specs=in_specs,
        out_specs=out_specs,
        compiler_params=_cparams(1),
        name="gqa_prep",
    )(*args)


def _attn_kernel(*refs, n_seg, seg_rows, hq, hkv, dqk, dv, tq):
    q_ref = refs[0]
    kv_refs = refs[1:1 + 2 * n_seg]
    o_ref = refs[1 + 2 * n_seg]
    k_sc, v_sc = refs[2 + 2 * n_seg:]
    g = hq // hkv
    nq = q_ref.shape[0] // tq
    for j in range(hkv):
        r = 0
        for s in range(n_seg):
            k_sc[j, r:r + seg_rows[s], :] = kv_refs[2 * s][:, j * dqk:(j + 1) * dqk].astype(BF16)
            v_sc[j, r:r + seg_rows[s], :] = kv_refs[2 * s + 1][:, j * dv:(j + 1) * dv].astype(BF16)
            r += seg_rows[s]

    def q_block(qb, carry):
        rows = pl.ds(pl.multiple_of(qb * tq, tq), tq)
        for h in range(hq):
            j = h // g
            q = q_ref[rows, h * dqk:(h + 1) * dqk]
            s_ = _dot_nt(q, k_sc[j])
            m = jnp.max(s_, axis=-1, keepdims=True)
            p = jnp.exp2(s_ - m)
            den = jnp.sum(p, axis=-1, keepdims=True)
            o = jnp.dot(p.astype(BF16), v_sc[j], preferred_element_type=F32) / den
            o_ref[rows, h * dv:(h + 1) * dv] = o.astype(o_ref.dtype)
        return carry

    if nq == 1:
        q_block(0, 0)
    else:
        lax.fori_loop(0, nq, q_block, 0)


def attention(q, segs, *, nb, q_rows, hq, hkv, dqk, dv, name):
    tq = 256
    in_specs = [pl.BlockSpec((q_rows, hq * dqk), lambda b: (b, 0))]
    args = [q]
    seg_rows = []
    for k_arr, k_spec, v_arr, v_spec, rows in segs:
        in_specs += [k_spec, v_spec]
        args += [k_arr, v_arr]
        seg_rows.append(rows)
    s_tot = sum(seg_rows)
    return pl.pallas_call(
        functools.partial(_attn_kernel, n_seg=len(segs), seg_rows=tuple(seg_rows), hq=hq, hkv=hkv,
                          dqk=dqk, dv=dv, tq=tq),
        out_shape=jax.ShapeDtypeStruct((nb * q_rows, hq * dv), BF16),
        grid=(nb,),
        in_specs=in_specs,
        out_specs=pl.BlockSpec((q_rows, hq * dv), lambda b: (b, 0)),
        scratch_shapes=[pltpu.VMEM((hkv, s_tot, dqk), BF16), pltpu.VMEM((hkv, s_tot, dv), BF16)],
        compiler_params=_cparams(1),
        name=name,
    )(*args)


def _rows_spec(rows, width, col_blk=0):
    return pl.BlockSpec((rows, width), lambda b: (b, col_blk))


def _gla_constants():
    c = GLA_CHUNK
    t = np.arange(c)[:, None]
    u = np.arange(c)[None, :]
    tril = (u <= t).astype(np.float32)
    hb = c // 2
    x = (t ^ u)[:hb, :hb]
    msb = np.where(x > 0, np.floor(np.log2(np.maximum(x, 1))), GLA_LEVELS - 1).astype(np.int32)
    return jnp.asarray(tril, BF16), jnp.asarray(msb)


def _split3(x):
    hi = x.astype(BF16)
    r = x - hi.astype(F32)
    mid = r.astype(BF16)
    lo = (r - mid.astype(F32)).astype(BF16)
    return [hi, mid, lo]


def _exp2_neg_abs(x):
    return jnp.exp2(-jnp.abs(x))


def _gla_kernel(q_ref, k_ref, v_ref, ga_ref, gk_ref, up_ref, bias_ref, ng_ref, tril_ref, msb_ref, *rest,
                n_chunks, has_s0, heads):
    rest = list(rest)
    s0_ref = rest.pop(0) if has_s0 else None
    y_ref = rest.pop(0)
    st_ref = None if has_s0 else rest.pop(0)
    c = GLA_CHUNK
    hb = c // 2
    nl = GLA_LEVELS
    dk = DK_A
    dv = DV_A
    row = lax.broadcasted_iota(jnp.int32, (c, 1), 0)

    def pass1(hd, rows):
        gk = gk_ref[rows, :].astype(BF16)
        las = []
        for d in range(2):
            zg = (jnp.dot(gk, up_ref[d, :, hd * dk:(hd + 1) * dk].astype(BF16), preferred_element_type=F32)
                  + bias_ref[d, :, hd * dk:(hd + 1) * dk])
            log_sig = jnp.minimum(zg, 0.0) - jnp.log1p(jnp.exp(-jnp.abs(zg)))
            las.append(log_sig * (LOG2E / GLA_GATE_NORM))
        parts = jnp.concatenate(_split3(las[0]) + _split3(las[1]), axis=1)
        cs = jnp.dot(tril_ref[...], parts, preferred_element_type=F32)
        cums = [cs[:, 0:dk] + cs[:, dk:2 * dk] + cs[:, 2 * dk:3 * dk],
                cs[:, 3 * dk:4 * dk] + cs[:, 4 * dk:5 * dk] + cs[:, 5 * dk:6 * dk]]
        tot_f = cums[0][c - 1:c, :]
        tot = jnp.concatenate([tot_f, cums[1][c - 1:c, :]], axis=1)
        k = k_ref[rows, hd * dk:(hd + 1) * dk].astype(F32)
        kd = jnp.concatenate([k * _exp2_neg_abs(tot_f - cums[0]), k * _exp2_neg_abs(cums[1] - las[1])], axis=1)
        kv = lax.dot_general(kd.astype(BF16), v_ref[rows, hd * dv:(hd + 1) * dv].astype(BF16),
                             (((0,), (0,)), ((), ())), preferred_element_type=F32)
        return las, cums, tot, kv

    def pass2(hd, rows, las, cums, tot, s_in):
        laf, lab = las
        cumf, cumb = cums
        cumbx = cumb - lab
        q = q_ref[rows, hd * dk:(hd + 1) * dk].astype(F32) * (DK_A ** -0.5)
        k = k_ref[rows, hd * dk:(hd + 1) * dk].astype(F32)
        v = v_ref[rows, hd * dv:(hd + 1) * dv].astype(BF16)
        msb = msb_ref[...]
        q2 = (2.0 * q).astype(BF16)
        kb = k.astype(BF16)
        scd = [_dot_nt(q2[0:hb], kb[0:hb]), _dot_nt(q2[hb:], kb[hb:])]
        a_lo = a_up = None
        for lvl in range(nl):
            b = 1 << lvl
            second = (row & b) != 0
            if lvl == 0:
                qq = q * jnp.exp2(jnp.where(second, laf, lab))
                kk = k
            elif lvl == 1:
                p = row & 3
                dq = jnp.where(p == 0, lab + pltpu.roll(lab, c - 1, 0),
                               jnp.where(p == 1, lab, jnp.where(p == 2, laf, laf + pltpu.roll(laf, 1, 0))))
                dkk = jnp.where(p == 0, pltpu.roll(laf, c - 1, 0), jnp.where(p == 3, pltpu.roll(lab, 1, 0), 0.0))
                qq = q * jnp.exp2(dq)
                kk = k * jnp.exp2(dkk)
            else:
                n = c // (2 * b)
                shp = (n, 2 * b, dk)
                cf3 = cumf.reshape(shp)
                cb3 = cumb.reshape(shp)
                ef = _exp2_neg_abs(cf3 - cf3[:, b - 1:b, :])
                eb = _exp2_neg_abs(cumbx.reshape(shp) - cb3[:, b - 1:b, :])
                if b >= SUBLANES:
                    eq = jnp.concatenate([eb[:, :b], ef[:, b:]], axis=1).reshape(c, dk)
                    ek = jnp.concatenate([ef[:, :b], eb[:, b:]], axis=1).reshape(c, dk)
                else:
                    ef = ef.reshape(c, dk)
                    eb = eb.reshape(c, dk)
                    eq = jnp.where(second, ef, eb)
                    ek = jnp.where(second, eb, ef)
                qq = q * eq
                kk = k * ek
            qq = qq.astype(BF16)
            kk = kk.astype(BF16)
            if lvl < nl - 1:
                for blk in range(2):
                    a = _dot_nt(qq[blk * hb:(blk + 1) * hb], kk[blk * hb:(blk + 1) * hb])
                    scd[blk] = jnp.where(msb == lvl, a, scd[blk])
            else:
                a_lo = _dot_nt(qq[hb:], kk[:hb])
                a_up = _dot_nt(qq[:hb], kk[hb:])
        sc = jnp.concatenate([jnp.concatenate([scd[0], a_up], axis=1),
                              jnp.concatenate([a_lo, scd[1]], axis=1)], axis=0)
        o = jnp.dot(sc.astype(BF16), v, preferred_element_type=F32)
        if s_in is not None:
            qd = jnp.concatenate([q * _exp2_neg_abs(cumf), q * _exp2_neg_abs(tot[:, dk:] - cumbx)], axis=1)
            o = o + jnp.dot(qd.astype(BF16), s_in.astype(BF16), preferred_element_type=F32)
        o = o * lax.rsqrt(jnp.mean(o * o, axis=-1, keepdims=True) + EPS) * ng_ref[...]
        ga = ga_ref[rows, hd * dv:(hd + 1) * dv].astype(F32)
        y_ref[rows, hd * dv:(hd + 1) * dv] = (o * (ga * jax.nn.sigmoid(ga))).astype(y_ref.dtype)

    if not has_s0:
        rows = slice(0, c)
        for hd in range(heads):
            las, cums, tot, kv = pass1(hd, rows)
            st_ref[0, hd] = kv[0:dk]
            st_ref[1, hd] = kv[dk:2 * dk]
            pass2(hd, rows, las, cums, tot, None)
        return

    la_sc, cum_sc, kv_sc, tot_sc, sin_sc = rest
    assert heads == 1

    def run_pass1(ci, carry):
        rows = pl.ds(pl.multiple_of(ci * c, c), c)
        las, cums, tot, kv = pass1(0, rows)
        for d in range(2):
            la_sc[d, rows, :] = las[d]
            cum_sc[d, rows, :] = cums[d]
        tot_sc[ci] = tot
        kv_sc[ci] = kv
        return carry

    lax.fori_loop(0, n_chunks, run_pass1, 0)

    eye = lax.broadcasted_iota(jnp.int32, (2 * dk, 2 * dk), 0) == lax.broadcasted_iota(
        jnp.int32, (2 * dk, 2 * dk), 1)

    def decay_col(ci):
        tot = jnp.broadcast_to(tot_sc[ci], (2 * dk, 2 * dk))
        return jnp.exp2(jnp.sum(jnp.where(eye, tot, 0.0), axis=1, keepdims=True))

    s = s0_ref[0]
    for ci in range(n_chunks):
        sin_sc[ci, 0:dk, :] = s
        if ci < n_chunks - 1:
            s = decay_col(ci)[0:dk] * s + kv_sc[ci, 0:dk, :]
    s = s0_ref[1]
    for ci in reversed(range(n_chunks)):
        sin_sc[ci, dk:2 * dk, :] = s
        if ci > 0:
            s = decay_col(ci)[dk:2 * dk] * s + kv_sc[ci, dk:2 * dk, :]

    def run_pass2(ci, carry):
        rows = pl.ds(pl.multiple_of(ci * c, c), c)
        pass2(0, rows, (la_sc[0, rows, :], la_sc[1, rows, :]), (cum_sc[0, rows, :], cum_sc[1, rows, :]),
              tot_sc[ci], sin_sc[ci])
        return carry

    lax.fori_loop(0, n_chunks, run_pass2, 0)


def gla(z, up_pad, bias, norm_g, tril, msb, l, *, row0, nb, seq, s0):
    rb0 = row0 // seq
    n_chunks = seq // GLA_CHUNK
    has_s0 = s0 is not None
    assert has_s0 or n_chunks == 1
    heads = 1 if has_s0 else H_A
    hb = GLA_CHUNK // 2
    if has_s0:
        grid = (nb, H_A)
        im = lambda f: (lambda b, h: f(b, h))
    else:
        grid = (nb,)
        im = lambda f: (lambda b: f(b, 0))
    nk, nv = heads * DK_A, heads * DV_A
    in_specs = [
        pl.BlockSpec((seq, nk), im(lambda b, h: (rb0 + b, Z_QA // nk + h))),
        pl.BlockSpec((seq, nk), im(lambda b, h: (rb0 + b, Z_KA // nk + h))),
        pl.BlockSpec((seq, nv), im(lambda b, h: (rb0 + b, Z_VA // nv + h))),
        pl.BlockSpec((seq, nv), im(lambda b, h: (rb0 + b, Z_GA // nv + h))),
        pl.BlockSpec((seq, 128), im(lambda b, h: (rb0 + b, Z_GK // 128))),
        pl.BlockSpec((None, 2, 128, nk), im(lambda b, h: (l, 0, 0, h))),
        pl.BlockSpec((None, 2, 1, nk), im(lambda b, h: (l, 0, 0, h))),
        pl.BlockSpec((None, 1, DV_A), im(lambda b, h: (l, 0, 0))),
        pl.BlockSpec((GLA_CHUNK, GLA_CHUNK), im(lambda b, h: (0, 0))),
        pl.BlockSpec((hb, hb), im(lambda b, h: (0, 0))),
    ]
    args = [z, z, z, z, z, up_pad, bias.reshape(DEPTH, 2, 1, H_A * DK_A),
            norm_g.reshape(DEPTH, 1, DV_A), tril, msb]
    out_shape = [jax.ShapeDtypeStruct((nb * seq, H_A * DV_A), BF16)]
    out_specs = [pl.BlockSpec((seq, nv), im(lambda b, h: (b, h)))]
    scratch = []
    if has_s0:
        in_specs.append(pl.BlockSpec((None, None, 2, None, DK_A, DV_A), im(lambda b, h: (b, l, 0, h, 0, 0))))
        args.append(s0)
        scratch = [pltpu.VMEM((2, seq, DK_A), F32), pltpu.VMEM((2, seq, DK_A), F32),
                   pltpu.VMEM((n_chunks, 2 * DK_A, DV_A), F32),
                   pltpu.VMEM((n_chunks, 1, 2 * DK_A), F32),
                   pltpu.VMEM((n_chunks, 2 * DK_A, DV_A), F32)]
    else:
        out_shape.append(jax.ShapeDtypeStruct((nb, 2, H_A, DK_A, DV_A), F32))
        out_specs.append(pl.BlockSpec((None, 2, H_A, DK_A, DV_A), im(lambda b, h: (b, 0, 0, 0, 0))))
    return pl.pallas_call(
        functools.partial(_gla_kernel, n_chunks=n_chunks, has_s0=has_s0, heads=heads),
        out_shape=out_shape,
        grid=grid,
        in_specs=in_specs,
        out_specs=out_specs,
        scratch_shapes=scratch,
        compiler_params=_cparams(len(grid)),
        name="gla",
    )(*args)


def kernel(x_prompt, x_sample, c, state_gla, cache_mla_ckv, cache_mla_krope, cache_gqa_k, cache_gqa_v,
           c_ctx, norm1_g, norm2_g, w_ada, b_ada, w_in, gla_gk_up, gla_gk_bias, gla_norm_g,
           mla_qa_norm_g, mla_w_qb, mla_kva_norm_g, mla_w_kvb, mla_q_norm_g, mla_k_norm_g,
           gqa_q_norm_g, gqa_k_norm_g, w_out_a, w_out_b, w_out_c, w_o,
           ffn_w_up, ffn_conv_w, ffn_conv_b, ffn_w_down):
    d = D_MODEL
    x = jnp.concatenate([x_prompt.reshape(T_P, d), x_sample.reshape(T_S, d)], axis=0)
    cvec = jnp.concatenate([c_ctx[None, :], c, jnp.zeros((MOD_ROWS - N_GROUPS, d), F32)], axis=0)
    mods = adaln_all(cvec, w_ada, b_ada)

    w_qb_pad = jnp.pad(mla_w_qb.reshape(DEPTH, Q_LORA_B, H_B, QK_B),
                       ((0, 0), (0, 0), (0, 0), (0, PAD_QK_B - QK_B))).reshape(DEPTH, Q_LORA_B, H_B * PAD_QK_B)
    q_g_pad = jnp.pad(mla_q_norm_g, ((0, 0), (0, PAD_QK_B - QK_B))).reshape(DEPTH, 1, PAD_QK_B)
    k_g_pad = jnp.pad(mla_k_norm_g, ((0, 0), (0, PAD_QK_B - QK_B))).reshape(DEPTH, 1, PAD_QK_B)
    up_pad = jnp.zeros((DEPTH, 2, 128, H_A * DK_A), F32)
    for dd in range(2):
        r0 = dd * GLA_GATE_RANK
        up_pad = up_pad.at[:, dd, r0:r0 + GLA_GATE_RANK, :].set(gla_gk_up[:, dd])
    tab_b = _rope_tables(DEC_SEQ, ROPE_B, 128)
    tab_c = _rope_tables(DEC_SEQ, HD_C, 128)
    tril, msb = _gla_constants()
    w_in_t = jnp.swapaxes(w_in, 1, 2)
    ck = cache_gqa_k.reshape(DEC_BATCH, DEPTH, PAST_LEN, HKV_C * HD_C)
    cv = cache_gqa_v.reshape(DEC_BATCH, DEPTH, PAST_LEN, HKV_C * HD_C)

    st_out, ckv_out, krope_out, kc_out, vc_out = [], [], [], [], []
    for l in range(DEPTH):
        h = norm_mod(x, norm1_g, mods, l, 0, 1)
        z = w_in_proj(h, w_in_t, l)

        ya_p, st = gla(z, up_pad, gla_gk_bias, gla_norm_g, tril, msb, l,
                       row0=0, nb=BATCH, seq=SEQ, s0=None)
        ya_s, = gla(z, up_pad, gla_gk_bias, gla_norm_g, tril, msb, l,
                    row0=T_P, nb=DEC_BATCH, seq=DEC_SEQ, s0=state_gla)

        bmk = 512
        qb_p = mla_q_proj(z, mla_qa_norm_g, w_qb_pad, q_g_pad, l, row0=0, rows=T_P, tab=None)
        qb_s = mla_q_proj(z, mla_qa_norm_g, w_qb_pad, q_g_pad, l, row0=T_P, rows=T_S, tab=tab_b)

        def tail_specs(row0):
            rb = row0 // bmk
            return (pl.BlockSpec((bmk, KV_LORA_B), lambda i: (rb + i, Z_CKV // KV_LORA_B)),
                    pl.BlockSpec((bmk, 128), lambda i: (rb + i, Z_KROPE // 128)))

        c_spec, kr_spec = tail_specs(0)
        kb_p, vb_p, ckv_p = mla_kv_proj(z, c_spec, z, kr_spec, mla_kva_norm_g, mla_w_kvb, k_g_pad, l,
                                        rows=T_P, bm=bmk, norm_in=True, tab=None)
        c_spec, kr_spec = tail_specs(T_P)
        kb_s, vb_s, _ = mla_kv_proj(z, c_spec, z, kr_spec, mla_kva_norm_g, mla_w_kvb, k_g_pad, l,
                                    rows=T_S, bm=bmk, norm_in=True, tab=tab_b)
        kb_c, vb_c = mla_kv_proj(
            cache_mla_ckv, pl.BlockSpec((None, None, PAST_LEN, KV_LORA_B), lambda i: (i, l, 0, 0)),
            cache_mla_krope, pl.BlockSpec((None, None, PAST_LEN, ROPE_B), lambda i: (i, l, 0, 0)),
            mla_kva_norm_g, mla_w_kvb, k_g_pad, l, rows=DEC_BATCH * PAST_LEN, bm=PAST_LEN, norm_in=False, tab=None)
        nkb, nvb = H_B * PAD_QK_B, H_B * V_B
        yb_p = attention(qb_p, [(kb_p, _rows_spec(SEQ, nkb), vb_p, _rows_spec(SEQ, nvb), SEQ)],
                         nb=BATCH, q_rows=SEQ, hq=H_B, hkv=H_B, dqk=PAD_QK_B, dv=V_B, name="mla_attn_p")
        yb_s = attention(qb_s, [(kb_s, _rows_spec(DEC_SEQ, nkb), vb_s, _rows_spec(DEC_SEQ, nvb), DEC_SEQ),
                                (kb_c, _rows_spec(PAST_LEN, nkb), vb_c, _rows_spec(PAST_LEN, nvb), PAST_LEN)],
                         nb=DEC_BATCH, q_rows=DEC_SEQ, hq=H_B, hkv=H_B, dqk=PAD_QK_B, dv=V_B, name="mla_attn_s")

        qc_p, kc_p, vc_p, kn_p = gqa_prep(z, gqa_q_norm_g, gqa_k_norm_g, l, row0=0, rows=T_P,
                                          tab=None, emit_kn=True)
        qc_s, kc_s, vc_s = gqa_prep(z, gqa_q_norm_g, gqa_k_norm_g, l, row0=T_P, rows=T_S,
                                    tab=tab_c, emit_kn=False)
        nkc = HKV_C * HD_C
        yc_p = attention(qc_p, [(kc_p, _rows_spec(SEQ, nkc), vc_p, _rows_spec(SEQ, nkc), SEQ)],
                         nb=BATCH, q_rows=SEQ, hq=HQ_C, hkv=HKV_C, dqk=HD_C, dv=HD_C, name="gqa_attn_p")
        cache_spec = pl.BlockSpec((None, None, PAST_LEN, nkc), lambda b: (b, l, 0, 0))
        yc_s = attention(qc_s, [(kc_s, _rows_spec(DEC_SEQ, nkc), vc_s, _rows_spec(DEC_SEQ, nkc), DEC_SEQ),
                                (ck, cache_spec, cv, cache_spec, PAST_LEN)],
                         nb=DEC_BATCH, q_rows=DEC_SEQ, hq=HQ_C, hkv=HKV_C, dqk=HD_C, dv=HD_C, name="gqa_attn_s")

        m = gated_merge((ya_p, ya_s, yb_p, yb_s, yc_p, yc_s), z, w_out_a, w_out_b, w_out_c, l)
        x, h2 = w_o_norm(m, w_o, l, x, mods, norm2_g)
        act = up_conv(h2, ffn_w_up, ffn_conv_w, ffn_conv_b, l)
        x = matmul_residual(act, ffn_w_down, l, x, mods, 5, bm=512, bn=512, name="ffn_down")

        st_out.append(st)
        ckv_out.append(ckv_p.reshape(BATCH, SEQ, KV_LORA_B))
        krope_out.append(z[:T_P, Z_KROPE:Z_KROPE + ROPE_B].astype(F32).reshape(BATCH, SEQ, ROPE_B))
        kc_out.append(kn_p.reshape(BATCH, SEQ, HKV_C, HD_C))
        vc_out.append(z[:T_P, Z_VC:Z_VC + nkc].astype(F32).reshape(BATCH, SEQ, HKV_C, HD_C))

    y_p = x[:T_P].reshape(BATCH, SEQ, d)
    y_s = x[T_P:].reshape(DEC_BATCH, DEC_SEQ, d)
    return (y_p, y_s, jnp.stack(st_out, axis=1), jnp.stack(ckv_out, axis=1), jnp.stack(krope_out, axis=1),
            jnp.stack(kc_out, axis=1), jnp.stack(vc_out, axis=1))
```

```python
import functools

import numpy as np
import jax
import jax.numpy as jnp
from jax import lax
from jax.experimental import pallas as pl
from jax.experimental.pallas import tpu as pltpu

F32 = jnp.float32
BF16 = jnp.bfloat16

D_MODEL = 2048
BATCH = 16
SEQ = 256
DEPTH = 4
DEC_BATCH = 2
DEC_SEQ = 1024
PAST_LEN = 512
GRID_W = 64
ROPE_THETA = 10000.0
EPS = 1e-6
H_A, DK_A, DV_A = 4, 128, 256
GLA_GATE_RANK = 16
GLA_GATE_NORM = 16.0
H_B, NOPE_B, ROPE_B, V_B = 8, 128, 64, 128
QK_B = NOPE_B + ROPE_B
Q_LORA_B = 512
KV_LORA_B = 256
HQ_C, HKV_C, HD_C = 8, 2, 128
D_FF = 5632

T_P = BATCH * SEQ
T_S = DEC_BATCH * DEC_SEQ
T_ALL = T_P + T_S
N_GROUPS = 1 + DEC_BATCH
SUBLANES = 8
MOD_ROWS = SUBLANES

_O_QA, _O_KA, _O_VA, _O_GA = 0, 512, 1024, 2048
_O_GKA = 3072
_O_QLORA = 3104
_O_KVA = 3616
_O_QC, _O_KC, _O_VC = 3936, 4960, 5216
_O_GATES = 5472
N_IN = 11616
W_IN_BN = 512
W_IN_RANGES = ((_O_GKA, 1), (_O_QC, 15), (_O_QA, 6), (_O_QLORA, 2))
N_Z = W_IN_BN * sum(nb for _, nb in W_IN_RANGES)
Z_GK = 0
Z_QC, Z_KC, Z_VC, Z_GATES = 512, 1536, 1792, 2048
Z_QA, Z_KA, Z_VA, Z_GA = 8192, 8704, 9216, 10240
Z_QLORA, Z_CKV, Z_KROPE = 11264, 11776, 12032
PAD_QK_B = 256

GLA_CHUNK = 256
GLA_LEVELS = 8

VMEM_LIMIT = 56 * 1024 * 1024
LOG2E = 1.4426950408889634


def _cparams(n_axes):
    return pltpu.CompilerParams(
        dimension_semantics=("arbitrary",) * n_axes, vmem_limit_bytes=VMEM_LIMIT)


def _group_of_block(i, bm):
    n_p = T_P // bm
    per = DEC_SEQ // bm
    return jnp.where(i < n_p, 0, 1 + (i - n_p) // per)


def _adaln_kernel(c_ref, w_ref, b_ref, o_ref):
    c = c_ref[...]
    a = (c * jax.nn.sigmoid(c)).astype(BF16)
    o_ref[...] = jnp.dot(a, w_ref[...].astype(BF16), preferred_element_type=F32) + b_ref[...]


def adaln_all(cvec, w_ada, b_ada):
    bn = 1024
    n = 6 * D_MODEL
    return pl.pallas_call(
        _adaln_kernel,
        out_shape=jax.ShapeDtypeStruct((DEPTH, MOD_ROWS, n), F32),
        grid=(DEPTH, n // bn),
        in_specs=[
            pl.BlockSpec((MOD_ROWS, D_MODEL), lambda l, j: (0, 0)),
            pl.BlockSpec((None, D_MODEL, bn), lambda l, j: (l, 0, j)),
            pl.BlockSpec((None, 1, bn), lambda l, j: (l, 0, j)),
        ],
        out_specs=pl.BlockSpec((None, MOD_ROWS, bn), lambda l, j: (l, 0, j)),
        compiler_params=_cparams(2),
        name="adaln",
    )(cvec, w_ada, b_ada.reshape(DEPTH, 1, n))


def _norm_mod_kernel(x_ref, g_ref, sh_ref, sc_ref, o_ref, *, bm):
    grp = _group_of_block(pl.program_id(0), bm)
    x = x_ref[...]
    y = x * lax.rsqrt(jnp.mean(x * x, axis=-1, keepdims=True) + EPS) * g_ref[...]
    sh = sh_ref[pl.ds(grp, 1), :]
    sc = sc_ref[pl.ds(grp, 1), :]
    o_ref[...] = (y * (1.0 + sc) + sh).astype(o_ref.dtype)


def norm_mod(x, g, mods, l, which_shift, which_scale):
    bm = 512
    nd = D_MODEL
    return pl.pallas_call(
        functools.partial(_norm_mod_kernel, bm=bm),
        out_shape=jax.ShapeDtypeStruct((T_ALL, nd), BF16),
        grid=(T_ALL // bm,),
        in_specs=[
            pl.BlockSpec((bm, nd), lambda i: (i, 0)),
            pl.BlockSpec((None, 1, nd), lambda i: (l, 0, 0)),
            pl.BlockSpec((None, MOD_ROWS, nd), lambda i: (l, 0, which_shift)),
            pl.BlockSpec((None, MOD_ROWS, nd), lambda i: (l, 0, which_scale)),
        ],
        out_specs=pl.BlockSpec((bm, nd), lambda i: (i, 0)),
        compiler_params=_cparams(1),
        name="norm_mod",
    )(x, g.reshape(DEPTH, 1, nd), mods, mods)


def _dot_nt(a, b):
    return lax.dot_general(a, b, (((1,), (1,)), ((), ())), preferred_element_type=F32)


ROW_BLK = 1024


def _w_in_kernel(h_ref, wt_ref, o_ref, wb_ref):
    wb_ref[...] = wt_ref[0].T.astype(BF16)

    for r in range(T_ALL // ROW_BLK):
        rows = slice(r * ROW_BLK, (r + 1) * ROW_BLK)
        o_ref[rows, :] = jnp.dot(h_ref[rows, :], wb_ref[...], preferred_element_type=F32).astype(o_ref.dtype)


def _w_in_first_col(j):
    col8 = None
    start = sum(nb for _, nb in W_IN_RANGES)
    for c0, nb in reversed(W_IN_RANGES):
        assert c0 % 8 == 0
        start -= nb
        here = c0 // 8 + (j - start) * (W_IN_BN // 8)
        col8 = here if col8 is None else jnp.where(j < start + nb, here, col8)
    return col8 * 8


def w_in_proj(h, w_in_t, l):
    bn = W_IN_BN
    return pl.pallas_call(
        _w_in_kernel,
        out_shape=jax.ShapeDtypeStruct((T_ALL, N_Z), BF16),
        grid=(N_Z // bn,),
        in_specs=[
            pl.BlockSpec((T_ALL, D_MODEL), lambda j: (0, 0), pipeline_mode=pl.Buffered(1)),
            pl.BlockSpec((pl.Element(1), pl.Element(bn), pl.Element(D_MODEL)),
                         lambda j: (l, _w_in_first_col(j), 0)),
        ],
        out_specs=pl.BlockSpec((T_ALL, bn), lambda j: (0, j)),
        scratch_shapes=[pltpu.VMEM((D_MODEL, bn), BF16)],
        compiler_params=_cparams(1),
        name="w_in",
    )(h, w_in_t)


def _mm_res_kernel(a_ref, w_ref, x_ref, gate_ref, o_ref, wb_ref, *, bm):
    @pl.when(pl.program_id(1) == 0)
    def _():
        wb_ref[...] = w_ref[...].astype(BF16)

    grp = _group_of_block(pl.program_id(1), bm)
    acc = jnp.dot(a_ref[...], wb_ref[...], preferred_element_type=F32)
    o_ref[...] = x_ref[...] + gate_ref[pl.ds(grp, 1), :] * acc


def matmul_residual(a, w, l, x, mods, which_gate, *, bm, bn, name):
    m, k = a.shape
    n = w.shape[-1]
    gate_blk = which_gate * (D_MODEL // bn)
    return pl.pallas_call(
        functools.partial(_mm_res_kernel, bm=bm),
        out_shape=jax.ShapeDtypeStruct((m, n), F32),
        grid=(n // bn, m // bm),
        in_specs=[
            pl.BlockSpec((bm, k), lambda j, i: (i, 0)),
            pl.BlockSpec((None, k, bn), lambda j, i: (l, 0, j)),
            pl.BlockSpec((bm, bn), lambda j, i: (i, j)),
            pl.BlockSpec((None, MOD_ROWS, bn), lambda j, i: (l, 0, gate_blk + j)),
        ],
        out_specs=pl.BlockSpec((bm, bn), lambda j, i: (i, j)),
        scratch_shapes=[pltpu.VMEM((k, bn), BF16)],
        compiler_params=_cparams(2),
        name=name,
    )(a, w, x, mods)


def _w_o_norm_kernel(m_ref, w_ref, x_ref, gate_ref, g_ref, sh_ref, sc_ref, xo_ref, h_ref, wb_ref, *, bm):
    i = pl.program_id(0)

    @pl.when(i == 0)
    def _():
        wb_ref[...] = w_ref[...].astype(BF16)

    grp = _group_of_block(i, bm)
    acc = jnp.dot(m_ref[...], wb_ref[...], preferred_element_type=F32)
    x = x_ref[...] + gate_ref[pl.ds(grp, 1), :] * acc
    xo_ref[...] = x
    y = x * lax.rsqrt(jnp.mean(x * x, axis=-1, keepdims=True) + EPS) * g_ref[...]
    h_ref[...] = (y * (1.0 + sc_ref[pl.ds(grp, 1), :]) + sh_ref[pl.ds(grp, 1), :]).astype(h_ref.dtype)


def w_o_norm(m, w_o, l, x, mods, norm_g):
    bm = 512
    nd = D_MODEL

    def mod_spec(which):
        return pl.BlockSpec((None, MOD_ROWS, nd), lambda i: (l, 0, which))

    return pl.pallas_call(
        functools.partial(_w_o_norm_kernel, bm=bm),
        out_shape=[jax.ShapeDtypeStruct((T_ALL, nd), F32), jax.ShapeDtypeStruct((T_ALL, nd), BF16)],
        grid=(T_ALL // bm,),
        in_specs=[
            pl.BlockSpec((bm, nd), lambda i: (i, 0)),
            pl.BlockSpec((None, nd, nd), lambda i: (l, 0, 0), pipeline_mode=pl.Buffered(1)),
            pl.BlockSpec((bm, nd), lambda i: (i, 0)),
            mod_spec(2),
            pl.BlockSpec((None, 1, nd), lambda i: (l, 0, 0)),
            mod_spec(3), mod_spec(4),
        ],
        out_specs=[pl.BlockSpec((bm, nd), lambda i: (i, 0)), pl.BlockSpec((bm, nd), lambda i: (i, 0))],
        scratch_shapes=[pltpu.VMEM((nd, nd), BF16)],
        compiler_params=_cparams(1),
        name="w_o_norm",
    )(m, w_o, x, mods, norm_g.reshape(DEPTH, 1, nd), mods, mods)


def _merge_kernel(yap_ref, yas_ref, ybp_ref, ybs_ref, ycp_ref, ycs_ref, wa_ref, wb_ref, wc_ref,
                  ga_ref, gb_ref, gc_ref, o_ref, wab_ref, wbb_ref, wcb_ref, *, bm):
    i = pl.program_id(1)

    @pl.when(i == 0)
    def _():
        wab_ref[...] = wa_ref[...].astype(BF16)
        wbb_ref[...] = wb_ref[...].astype(BF16)
        wcb_ref[...] = wc_ref[...].astype(BF16)

    is_ctx = i < T_P // bm

    def branch(yp_ref, ys_ref, w_ref, g_ref):
        y = jnp.where(is_ctx, yp_ref[...], ys_ref[...])
        return jax.nn.sigmoid(g_ref[...].astype(F32)) * jnp.dot(y, w_ref[...], preferred_element_type=F32)

    m = (branch(yap_ref, yas_ref, wab_ref, ga_ref) + branch(ybp_ref, ybs_ref, wbb_ref, gb_ref)
         + branch(ycp_ref, ycs_ref, wcb_ref, gc_ref))
    o_ref[...] = m.astype(o_ref.dtype)


def gated_merge(ys, z, w_out_a, w_out_b, w_out_c, l):
    bm, bn = 512, 1024
    k = 1024
    gate0 = Z_GATES // bn
    per = D_MODEL // bn
    n_p = T_P // bm
    yp_spec = pl.BlockSpec((bm, k), lambda j, i: (jnp.minimum(i, n_p - 1), 0))
    ys_spec = pl.BlockSpec((bm, k), lambda j, i: (jnp.maximum(i - n_p, 0), 0))
    w_spec = pl.BlockSpec((None, k, bn), lambda j, i: (l, 0, j))

    def g_spec(br):
        return pl.BlockSpec((bm, bn), lambda j, i: (i, gate0 + br * per + j))

    return pl.pallas_call(
        functools.partial(_merge_kernel, bm=bm),
        out_shape=jax.ShapeDtypeStruct((T_ALL, D_MODEL), BF16),
        grid=(D_MODEL // bn, T_ALL // bm),
        in_specs=[yp_spec, ys_spec] * 3 + [w_spec] * 3 + [g_spec(0), g_spec(1), g_spec(2)],
        out_specs=pl.BlockSpec((bm, bn), lambda j, i: (i, j)),
        scratch_shapes=[pltpu.VMEM((k, bn), BF16)] * 3,
        compiler_params=_cparams(2),
        name="gated_merge",
    )(*ys, w_out_a, w_out_b, w_out_c, z, z, z)


def _up_conv_kernel(h_ref, wa_ref, wg_ref, cwa_ref, cwg_ref, cba_ref, cbg_ref, o_ref, wab_ref, wgb_ref,
                    *sc_refs):
    wab_ref[...] = wa_ref[...].astype(BF16)
    wgb_ref[...] = wg_ref[...].astype(BF16)
    bm = ROW_BLK
    nt = bm // SUBLANES
    bn = o_ref.shape[1]
    sub = lax.broadcasted_iota(jnp.int32, (1, SUBLANES, 1), 1)
    for sc_ref in sc_refs:
        sc_ref[0:SUBLANES, :] = jnp.zeros((SUBLANES, bn), F32)
        sc_ref[SUBLANES + bm:2 * SUBLANES + bm, :] = jnp.zeros((SUBLANES, bn), F32)

    def taps(cw_ref, cb_ref):
        return [jnp.broadcast_to(cw_ref[k:k + 1, :], (SUBLANES, bn))[None] for k in range(3)] + [
            jnp.broadcast_to(cb_ref[...], (SUBLANES, bn))[None]]

    taps_a = taps(cwa_ref, cba_ref)
    taps_g = taps(cwg_ref, cbg_ref)

    for r in range(T_ALL // bm):
        seq = SEQ if r < T_P // bm else DEC_SEQ
        tps = seq // SUBLANES
        n_seq = bm // seq
        h = h_ref[r * bm:(r + 1) * bm, :]

        def conv(w_ref, tp, sc_ref):
            u = jnp.dot(h, w_ref[...], preferred_element_type=F32)
            sc_ref[SUBLANES:SUBLANES + bm, :] = u
            prev = sc_ref[SUBLANES - 1:SUBLANES - 1 + bm, :].reshape(nt, SUBLANES, bn)
            nxt = sc_ref[SUBLANES + 1:SUBLANES + 1 + bm, :].reshape(nt, SUBLANES, bn)
            if n_seq > 1:
                prev = jnp.concatenate(
                    [p for s in range(n_seq) for p in (
                        jnp.where(sub == 0, 0.0, prev[s * tps:s * tps + 1]), prev[s * tps + 1:(s + 1) * tps])],
                    axis=0)
                nxt = jnp.concatenate(
                    [p for s in range(n_seq) for p in (
                        nxt[s * tps:(s + 1) * tps - 1],
                        jnp.where(sub == SUBLANES - 1, 0.0, nxt[(s + 1) * tps - 1:(s + 1) * tps]))],
                    axis=0)
            return prev * tp[0] + u.reshape(nt, SUBLANES, bn) * tp[1] + nxt * tp[2] + tp[3]

        a = conv(wab_ref, taps_a, sc_refs[0])
        g = conv(wgb_ref, taps_g, sc_refs[1])
        act = (g * jax.nn.sigmoid(g) * a).reshape(bm, bn)
        o_ref[r * bm:(r + 1) * bm, :] = act.astype(o_ref.dtype)


def up_conv(h2, w_up, conv_w, conv_b, l):
    bn = 256
    nb = D_FF // bn
    cb = conv_b.reshape(DEPTH, 1, 2 * D_FF)
    return pl.pallas_call(
        _up_conv_kernel,
        out_shape=jax.ShapeDtypeStruct((T_ALL, D_FF), BF16),
        grid=(nb,),
        in_specs=[
            pl.BlockSpec((T_ALL, D_MODEL), lambda j: (0, 0), pipeline_mode=pl.Buffered(1)),
            pl.BlockSpec((None, D_MODEL, bn), lambda j: (l, 0, j)),
            pl.BlockSpec((None, D_MODEL, bn), lambda j: (l, 0, nb + j)),
            pl.BlockSpec((None, 3, bn), lambda j: (l, 0, j)),
            pl.BlockSpec((None, 3, bn), lambda j: (l, 0, nb + j)),
            pl.BlockSpec((None, 1, bn), lambda j: (l, 0, j)),
            pl.BlockSpec((None, 1, bn), lambda j: (l, 0, nb + j)),
        ],
        out_specs=pl.BlockSpec((T_ALL, bn), lambda j: (0, j)),
        scratch_shapes=[pltpu.VMEM((D_MODEL, bn), BF16)] * 2
        + [pltpu.VMEM((ROW_BLK + 2 * SUBLANES, bn), F32)] * 2,
        compiler_params=_cparams(1),
        name="up_conv",
    )(h2, w_up, w_up, conv_w, conv_w, cb, cb)


def _rope_tables(n_tok, rot_dim, width):
    rows = n_tok // GRID_W
    row = jnp.repeat(jnp.arange(rows, dtype=F32), GRID_W)
    col = jnp.tile(jnp.arange(GRID_W, dtype=F32), rows)
    half = rot_dim // 2
    inv = jnp.power(ROPE_THETA, -jnp.arange(half // 2, dtype=F32) * (2.0 / half))
    ang = jnp.concatenate([row[:, None] * inv, col[:, None] * inv], axis=-1)
    cos, sin = jnp.cos(ang), jnp.sin(ang)
    zero = jnp.zeros_like(sin)
    c2 = jnp.stack([cos, cos], axis=-1).reshape(n_tok, rot_dim)
    s_even = jnp.stack([-sin, zero], axis=-1).reshape(n_tok, rot_dim)
    s_odd = jnp.stack([zero, sin], axis=-1).reshape(n_tok, rot_dim)
    pad = ((0, 0), (0, width - rot_dim))
    return jnp.stack([jnp.pad(c2, pad), jnp.pad(s_even, pad), jnp.pad(s_odd, pad)], axis=0)


def _rope(x, tab_ref):
    w = x.shape[-1]
    return (x * tab_ref[0] + pltpu.roll(x, w - 1, 1) * tab_ref[1] + pltpu.roll(x, 1, 1) * tab_ref[2])


def _mla_q_kernel(ql_ref, g1_ref, w_ref, g2_ref, *rest, rope):
    if rope:
        tab_ref, o_ref, wb_ref = rest
    else:
        o_ref, wb_ref = rest

    @pl.when(pl.program_id(0) == 0)
    def _():
        wb_ref[...] = w_ref[...].astype(BF16)

    x = ql_ref[...].astype(F32)
    xn = x * lax.rsqrt(jnp.mean(x * x, axis=-1, keepdims=True) + EPS) * g1_ref[...]
    q = jnp.dot(xn.astype(BF16), wb_ref[...], preferred_element_type=F32)
    scale = QK_B ** -0.5 * LOG2E
    for h in range(H_B):
        lo = h * PAD_QK_B
        qn = q[:, lo:lo + NOPE_B]
        qr = q[:, lo + NOPE_B:lo + PAD_QK_B]
        ms = (jnp.sum(qn * qn, axis=-1, keepdims=True) + jnp.sum(qr * qr, axis=-1, keepdims=True)) / QK_B
        r = lax.rsqrt(ms + EPS) * scale
        qn = qn * r * g2_ref[:, 0:NOPE_B]
        qr = qr * r * g2_ref[:, NOPE_B:PAD_QK_B]
        if rope:
            qr = _rope(qr, tab_ref)
        o_ref[:, lo:lo + NOPE_B] = qn.astype(o_ref.dtype)
        o_ref[:, lo + NOPE_B:lo + PAD_QK_B] = qr.astype(o_ref.dtype)


def mla_q_proj(z, qa_g, w_qb_pad, q_g_pad, l, *, row0, rows, tab):
    bm = 512
    rope = tab is not None
    kq = Q_LORA_B
    n = H_B * PAD_QK_B
    rb0 = row0 // bm
    in_specs = [
        pl.BlockSpec((bm, kq), lambda i: (rb0 + i, Z_QLORA // kq)),
        pl.BlockSpec((None, 1, kq), lambda i: (l, 0, 0)),
        pl.BlockSpec((None, kq, n), lambda i: (l, 0, 0)),
        pl.BlockSpec((None, 1, PAD_QK_B), lambda i: (l, 0, 0)),
    ]
    args = [z, qa_g.reshape(DEPTH, 1, kq), w_qb_pad, q_g_pad]
    if rope:
        per = DEC_SEQ // bm
        in_specs.append(pl.BlockSpec((3, bm, 128), lambda i: (0, i % per, 0)))
        args.append(tab)
    return pl.pallas_call(
        functools.partial(_mla_q_kernel, rope=rope),
        out_shape=jax.ShapeDtypeStruct((rows, n), BF16),
        grid=(rows // bm,),
        in_specs=in_specs,
        out_specs=pl.BlockSpec((bm, n), lambda i: (i, 0)),
        scratch_shapes=[pltpu.VMEM((kq, n), BF16)],
        compiler_params=_cparams(1),
        name="mla_q_proj",
    )(*args)


def _mla_kv_kernel(c_ref, kr_ref, g1_ref, w_ref, g2_ref, *rest, norm_in, rope):
    rest = list(rest)
    tab_ref = rest.pop(0) if rope else None
    if norm_in:
        k_ref, v_ref, ckv_ref, wb_ref = rest
    else:
        k_ref, v_ref, wb_ref = rest

    @pl.when(pl.program_id(0) == 0)
    def _():
        wb_ref[...] = w_ref[...].astype(BF16)

    c = c_ref[...].astype(F32)
    if norm_in:
        c = c * lax.rsqrt(jnp.mean(c * c, axis=-1, keepdims=True) + EPS) * g1_ref[...]
        ckv_ref[...] = c
    kv = jnp.dot(c.astype(BF16), wb_ref[...], preferred_element_type=F32)
    kr = kr_ref[...].astype(F32)
    if kr.shape[-1] == 128:
        lane = lax.broadcasted_iota(jnp.int32, kr.shape, 1)
        kr = jnp.where(lane < ROPE_B, kr, 0.0)
    else:
        kr = jnp.concatenate([kr, jnp.zeros_like(kr)], axis=-1)
    kr_ss = jnp.sum(kr * kr, axis=-1, keepdims=True)
    for h in range(H_B):
        kn = kv[:, h * 256:h * 256 + NOPE_B]
        v = kv[:, h * 256 + NOPE_B:(h + 1) * 256]
        r = lax.rsqrt((jnp.sum(kn * kn, axis=-1, keepdims=True) + kr_ss) / QK_B + EPS)
        krh = kr * r * g2_ref[:, NOPE_B:PAD_QK_B]
        if rope:
            krh = _rope(krh, tab_ref)
        k_ref[:, h * PAD_QK_B:h * PAD_QK_B + NOPE_B] = (kn * r * g2_ref[:, 0:NOPE_B]).astype(k_ref.dtype)
        k_ref[:, h * PAD_QK_B + NOPE_B:(h + 1) * PAD_QK_B] = krh.astype(k_ref.dtype)
        v_ref[:, h * V_B:(h + 1) * V_B] = v.astype(v_ref.dtype)


def mla_kv_proj(c_arr, c_spec, kr_arr, kr_spec, kva_g, w_kvb, k_g_pad, l, *, rows, bm, norm_in, tab):
    rope = tab is not None
    kc = KV_LORA_B
    n = H_B * (NOPE_B + V_B)
    in_specs = [
        c_spec, kr_spec,
        pl.BlockSpec((None, 1, kc), lambda i: (l, 0, 0)),
        pl.BlockSpec((None, kc, n), lambda i: (l, 0, 0)),
        pl.BlockSpec((None, 1, PAD_QK_B), lambda i: (l, 0, 0)),
    ]
    args = [c_arr, kr_arr, kva_g.reshape(DEPTH, 1, kc), w_kvb, k_g_pad]
    if rope:
        per = DEC_SEQ // bm
        in_specs.append(pl.BlockSpec((3, bm, 128), lambda i: (0, i % per, 0)))
        args.append(tab)
    out_shape = [jax.ShapeDtypeStruct((rows, H_B * PAD_QK_B), BF16),
                 jax.ShapeDtypeStruct((rows, H_B * V_B), BF16)]
    out_specs = [pl.BlockSpec((bm, H_B * PAD_QK_B), lambda i: (i, 0)),
                 pl.BlockSpec((bm, H_B * V_B), lambda i: (i, 0))]
    if norm_in:
        out_shape.append(jax.ShapeDtypeStruct((rows, kc), F32))
        out_specs.append(pl.BlockSpec((bm, kc), lambda i: (i, 0)))
    return pl.pallas_call(
        functools.partial(_mla_kv_kernel, norm_in=norm_in, rope=rope),
        out_shape=out_shape,
        grid=(rows // bm,),
        in_specs=in_specs,
        out_specs=out_specs,
        scratch_shapes=[pltpu.VMEM((kc, n), BF16)],
        compiler_params=_cparams(1),
        name="mla_kv_proj",
    )(*args)


def _gqa_prep_kernel(qlo_ref, qhi_ref, k_ref, v_ref, gq_ref, gk_ref, *rest, rope, emit_kn):
    rest = list(rest)
    tab_ref = rest.pop(0) if rope else None
    qo_ref, ko_ref, vo_ref = rest[:3]
    kn_ref = rest[3] if emit_kn else None
    scale = HD_C ** -0.5 * LOG2E
    half = HQ_C // 2

    def norm(x, g_ref):
        return x * lax.rsqrt(jnp.mean(x * x, axis=-1, keepdims=True) + EPS) * g_ref[...]

    for h in range(HQ_C):
        q_ref, hh = (qlo_ref, h) if h < half else (qhi_ref, h - half)
        x = norm(q_ref[:, hh * HD_C:(hh + 1) * HD_C].astype(F32), gq_ref)
        if rope:
            x = _rope(x, tab_ref)
        qo_ref[:, h * HD_C:(h + 1) * HD_C] = (x * scale).astype(qo_ref.dtype)
    for h in range(HKV_C):
        x = norm(k_ref[:, h * HD_C:(h + 1) * HD_C].astype(F32), gk_ref)
        if emit_kn:
            kn_ref[:, h * HD_C:(h + 1) * HD_C] = x
        if rope:
            x = _rope(x, tab_ref)
        ko_ref[:, h * HD_C:(h + 1) * HD_C] = x.astype(ko_ref.dtype)
    vo_ref[...] = v_ref[...].astype(vo_ref.dtype)


def gqa_prep(z, gq, gk, l, *, row0, rows, tab, emit_kn):
    bm = 512
    rope = tab is not None
    rb0 = row0 // bm
    nq, nk = HQ_C * HD_C, HKV_C * HD_C
    nqh = nq // 2
    in_specs = [
        pl.BlockSpec((bm, nqh), lambda i: (rb0 + i, Z_QC // nqh)),
        pl.BlockSpec((bm, nqh), lambda i: (rb0 + i, Z_QC // nqh + 1)),
        pl.BlockSpec((bm, nk), lambda i: (rb0 + i, Z_KC // nk)),
        pl.BlockSpec((bm, nk), lambda i: (rb0 + i, Z_VC // nk)),
        pl.BlockSpec((None, 1, HD_C), lambda i: (l, 0, 0)),
        pl.BlockSpec((None, 1, HD_C), lambda i: (l, 0, 0)),
    ]
    args = [z, z, z, z, gq.reshape(DEPTH, 1, HD_C), gk.reshape(DEPTH, 1, HD_C)]
    if rope:
        per = DEC_SEQ // bm
        in_specs.append(pl.BlockSpec((3, bm, 128), lambda i: (0, i % per, 0)))
        args.append(tab)
    out_shape = [jax.ShapeDtypeStruct((rows, nq), BF16),
                 jax.ShapeDtypeStruct((rows, nk), BF16),
                 jax.ShapeDtypeStruct((rows, nk), BF16)]
    out_specs = [pl.BlockSpec((bm, nq), lambda i: (i, 0)),
                 pl.BlockSpec((bm, nk), lambda i: (i, 0)),
                 pl.BlockSpec((bm, nk), lambda i: (i, 0))]
    if emit_kn:
        out_shape.append(jax.ShapeDtypeStruct((rows, nk), F32))
        out_specs.append(pl.BlockSpec((bm, nk), lambda i: (i, 0)))
    return pl.pallas_call(
        functools.partial(_gqa_prep_kernel, rope=rope, emit_kn=emit_kn),
        out_shape=out_shape,
        grid=(rows // bm,),
        in_specs=in_specs,
        out_specs=out_specs,
        compiler_params=_cparams(1),
        name="gqa_prep",
    )(*args)


def _attn_kernel(*refs, n_seg, seg_rows, hq, hkv, dqk, dv, tq):
    q_ref = refs[0]
    kv_refs = refs[1:1 + 2 * n_seg]
    o_ref = refs[1 + 2 * n_seg]
    k_sc, v_sc = refs[2 + 2 * n_seg:]
    g = hq // hkv
    nq = q_ref.shape[0] // tq
    for j in range(hkv):
        r = 0
        for s in range(n_seg):
            k_sc[j, r:r + seg_rows[s], :] = kv_refs[2 * s][:, j * dqk:(j + 1) * dqk].astype(BF16)
            v_sc[j, r:r + seg_rows[s], :] = kv_refs[2 * s + 1][:, j * dv:(j + 1) * dv].astype(BF16)
            r += seg_rows[s]

    def q_block(qb, carry):
        rows = pl.ds(pl.multiple_of(qb * tq, tq), tq)
        for h in range(hq):
            j = h // g
            q = q_ref[rows, h * dqk:(h + 1) * dqk]
            s_ = _dot_nt(q, k_sc[j])
            m = jnp.max(s_, axis=-1, keepdims=True)
            p = jnp.exp2(s_ - m)
            den = jnp.sum(p, axis=-1, keepdims=True)
            o = jnp.dot(p.astype(BF16), v_sc[j], preferred_element_type=F32) / den
            o_ref[rows, h * dv:(h + 1) * dv] = o.astype(o_ref.dtype)
        return carry

    if nq == 1:
        q_block(0, 0)
    else:
        lax.fori_loop(0, nq, q_block, 0)


def attention(q, segs, *, nb, q_rows, hq, hkv, dqk, dv, name):
    tq = 256
    in_specs = [pl.BlockSpec((q_rows, hq * dqk), lambda b: (b, 0))]
    args = [q]
    seg_rows = []
    for k_arr, k_spec, v_arr, v_spec, rows in segs:
        in_specs += [k_spec, v_spec]
        args += [k_arr, v_arr]
        seg_rows.append(rows)
    s_tot = sum(seg_rows)
    return pl.pallas_call(
        functools.partial(_attn_kernel, n_seg=len(segs), seg_rows=tuple(seg_rows), hq=hq, hkv=hkv,
                          dqk=dqk, dv=dv, tq=tq),
        out_shape=jax.ShapeDtypeStruct((nb * q_rows, hq * dv), BF16),
        grid=(nb,),
        in_specs=in_specs,
        out_specs=pl.BlockSpec((q_rows, hq * dv), lambda b: (b, 0)),
        scratch_shapes=[pltpu.VMEM((hkv, s_tot, dqk), BF16), pltpu.VMEM((hkv, s_tot, dv), BF16)],
        compiler_params=_cparams(1),
        name=name,
    )(*args)


def _rows_spec(rows, width, col_blk=0):
    return pl.BlockSpec((rows, width), lambda b: (b, col_blk))


def _gla_constants():
    c = GLA_CHUNK
    t = np.arange(c)[:, None]
    u = np.arange(c)[None, :]
    tril = (u <= t).astype(np.float32)
    hb = c // 2
    x = (t ^ u)[:hb, :hb]
    msb = np.where(x > 0, np.floor(np.log2(np.maximum(x, 1))), GLA_LEVELS - 1).astype(np.int32)
    return jnp.asarray(tril, BF16), jnp.asarray(msb)


def _split3(x):
    hi = x.astype(BF16)
    r = x - hi.astype(F32)
    mid = r.astype(BF16)
    lo = (r - mid.astype(F32)).astype(BF16)
    return [hi, mid, lo]


def _exp2_neg_abs(x):
    return jnp.exp2(-jnp.abs(x))


def _gla_kernel(q_ref, k_ref, v_ref, ga_ref, gk_ref, up_ref, bias_ref, ng_ref, tril_ref, msb_ref, *rest,
                n_chunks, has_s0, heads):
    rest = list(rest)
    s0_ref = rest.pop(0) if has_s0 else None
    y_ref = rest.pop(0)
    st_ref = None if has_s0 else rest.pop(0)
    c = GLA_CHUNK
    hb = c // 2
    nl = GLA_LEVELS
    dk = DK_A
    dv = DV_A
    row = lax.broadcasted_iota(jnp.int32, (c, 1), 0)

    def pass1(hd, rows):
        gk = gk_ref[rows, :].astype(BF16)
        las = []
        for d in range(2):
            zg = (jnp.dot(gk, up_ref[d, :, hd * dk:(hd + 1) * dk].astype(BF16), preferred_element_type=F32)
                  + bias_ref[d, :, hd * dk:(hd + 1) * dk])
            log_sig = jnp.minimum(zg, 0.0) - jnp.log1p(jnp.exp(-jnp.abs(zg)))
            las.append(log_sig * (LOG2E / GLA_GATE_NORM))
        parts = jnp.concatenate(_split3(las[0]) + _split3(las[1]), axis=1)
        cs = jnp.dot(tril_ref[...], parts, preferred_element_type=F32)
        cums = [cs[:, 0:dk] + cs[:, dk:2 * dk] + cs[:, 2 * dk:3 * dk],
                cs[:, 3 * dk:4 * dk] + cs[:, 4 * dk:5 * dk] + cs[:, 5 * dk:6 * dk]]
        tot_f = cums[0][c - 1:c, :]
        tot = jnp.concatenate([tot_f, cums[1][c - 1:c, :]], axis=1)
        k = k_ref[rows, hd * dk:(hd + 1) * dk].astype(F32)
        kd = jnp.concatenate([k * _exp2_neg_abs(tot_f - cums[0]), k * _exp2_neg_abs(cums[1] - las[1])], axis=1)
        kv = lax.dot_general(kd.astype(BF16), v_ref[rows, hd * dv:(hd + 1) * dv].astype(BF16),
                             (((0,), (0,)), ((), ())), preferred_element_type=F32)
        return las, cums, tot, kv

    def pass2(hd, rows, las, cums, tot, s_in):
        laf, lab = las
        cumf, cumb = cums
        cumbx = cumb - lab
        q = q_ref[rows, hd * dk:(hd + 1) * dk].astype(F32) * (DK_A ** -0.5)
        k = k_ref[rows, hd * dk:(hd + 1) * dk].astype(F32)
        v = v_ref[rows, hd * dv:(hd + 1) * dv].astype(BF16)
        msb = msb_ref[...]
        q2 = (2.0 * q).astype(BF16)
        kb = k.astype(BF16)
        scd = [_dot_nt(q2[0:hb], kb[0:hb]), _dot_nt(q2[hb:], kb[hb:])]
        a_lo = a_up = None
        for lvl in range(nl):
            b = 1 << lvl
            second = (row & b) != 0
            if lvl == 0:
                qq = q * jnp.exp2(jnp.where(second, laf, lab))
                kk = k
            elif lvl == 1:
                p = row & 3
                dq = jnp.where(p == 0, lab + pltpu.roll(lab, c - 1, 0),
                               jnp.where(p == 1, lab, jnp.where(p == 2, laf, laf + pltpu.roll(laf, 1, 0))))
                dkk = jnp.where(p == 0, pltpu.roll(laf, c - 1, 0), jnp.where(p == 3, pltpu.roll(lab, 1, 0), 0.0))
                qq = q * jnp.exp2(dq)
                kk = k * jnp.exp2(dkk)
            else:
                n = c // (2 * b)
                shp = (n, 2 * b, dk)
                cf3 = cumf.reshape(shp)
                cb3 = cumb.reshape(shp)
                ef = _exp2_neg_abs(cf3 - cf3[:, b - 1:b, :])
                eb = _exp2_neg_abs(cumbx.reshape(shp) - cb3[:, b - 1:b, :])
                if b >= SUBLANES:
                    eq = jnp.concatenate([eb[:, :b], ef[:, b:]], axis=1).reshape(c, dk)
                    ek = jnp.concatenate([ef[:, :b], eb[:, b:]], axis=1).reshape(c, dk)
                else:
                    ef = ef.reshape(c, dk)
                    eb = eb.reshape(c, dk)
                    eq = jnp.where(second, ef, eb)
                    ek = jnp.where(second, eb, ef)
                qq = q * eq
                kk = k * ek
            qq = qq.astype(BF16)
            kk = kk.astype(BF16)
            if lvl < nl - 1:
                for blk in range(2):
                    a = _dot_nt(qq[blk * hb:(blk + 1) * hb], kk[blk * hb:(blk + 1) * hb])
                    scd[blk] = jnp.where(msb == lvl, a, scd[blk])
            else:
                a_lo = _dot_nt(qq[hb:], kk[:hb])
                a_up = _dot_nt(qq[:hb], kk[hb:])
        sc = jnp.concatenate([jnp.concatenate([scd[0], a_up], axis=1),
                              jnp.concatenate([a_lo, scd[1]], axis=1)], axis=0)
        o = jnp.dot(sc.astype(BF16), v, preferred_element_type=F32)
        if s_in is not None:
            qd = jnp.concatenate([q * _exp2_neg_abs(cumf), q * _exp2_neg_abs(tot[:, dk:] - cumbx)], axis=1)
            o = o + jnp.dot(qd.astype(BF16), s_in.astype(BF16), preferred_element_type=F32)
        o = o * lax.rsqrt(jnp.mean(o * o, axis=-1, keepdims=True) + EPS) * ng_ref[...]
        ga = ga_ref[rows, hd * dv:(hd + 1) * dv].astype(F32)
        y_ref[rows, hd * dv:(hd + 1) * dv] = (o * (ga * jax.nn.sigmoid(ga))).astype(y_ref.dtype)

    if not has_s0:
        rows = slice(0, c)
        for hd in range(heads):
            las, cums, tot, kv = pass1(hd, rows)
            st_ref[0, hd] = kv[0:dk]
            st_ref[1, hd] = kv[dk:2 * dk]
            pass2(hd, rows, las, cums, tot, None)
        return

    la_sc, cum_sc, kv_sc, tot_sc, sin_sc = rest
    assert heads == 1

    for ci in range(n_chunks):
        rows = slice(ci * c, (ci + 1) * c)
        las, cums, tot, kv = pass1(0, rows)
        for d in range(2):
            la_sc[d, rows, :] = las[d]
            cum_sc[d, rows, :] = cums[d]
        tot_sc[ci] = tot
        kv_sc[ci] = kv

    eye = lax.broadcasted_iota(jnp.int32, (2 * dk, 2 * dk), 0) == lax.broadcasted_iota(
        jnp.int32, (2 * dk, 2 * dk), 1)

    def decay_col(ci):
        tot = jnp.broadcast_to(tot_sc[ci], (2 * dk, 2 * dk))
        return jnp.exp2(jnp.sum(jnp.where(eye, tot, 0.0), axis=1, keepdims=True))

    s = s0_ref[0]
    for ci in range(n_chunks):
        sin_sc[ci, 0:dk, :] = s
        if ci < n_chunks - 1:
            s = decay_col(ci)[0:dk] * s + kv_sc[ci, 0:dk, :]
    s = s0_ref[1]
    for ci in reversed(range(n_chunks)):
        sin_sc[ci, dk:2 * dk, :] = s
        if ci > 0:
            s = decay_col(ci)[dk:2 * dk] * s + kv_sc[ci, dk:2 * dk, :]

    for ci in range(n_chunks):
        rows = slice(ci * c, (ci + 1) * c)
        pass2(0, rows, (la_sc[0, rows, :], la_sc[1, rows, :]), (cum_sc[0, rows, :], cum_sc[1, rows, :]),
              tot_sc[ci], sin_sc[ci])


def gla(z, up_pad, bias, norm_g, tril, msb, l, *, row0, nb, seq, s0):
    rb0 = row0 // seq
    n_chunks = seq // GLA_CHUNK
    has_s0 = s0 is not None
    assert has_s0 or n_chunks == 1
    heads = 1 if has_s0 else H_A
    hb = GLA_CHUNK // 2
    if has_s0:
        grid = (nb, H_A)
        im = lambda f: (lambda b, h: f(b, h))
    else:
        grid = (nb,)
        im = lambda f: (lambda b: f(b, 0))
    nk, nv = heads * DK_A, heads * DV_A
    in_specs = [
        pl.BlockSpec((seq, nk), im(lambda b, h: (rb0 + b, Z_QA // nk + h))),
        pl.BlockSpec((seq, nk), im(lambda b, h: (rb0 + b, Z_KA // nk + h))),
        pl.BlockSpec((seq, nv), im(lambda b, h: (rb0 + b, Z_VA // nv + h))),
        pl.BlockSpec((seq, nv), im(lambda b, h: (rb0 + b, Z_GA // nv + h))),
        pl.BlockSpec((seq, 128), im(lambda b, h: (rb0 + b, Z_GK // 128))),
        pl.BlockSpec((None, 2, 128, nk), im(lambda b, h: (l, 0, 0, h))),
        pl.BlockSpec((None, 2, 1, nk), im(lambda b, h: (l, 0, 0, h))),
        pl.BlockSpec((None, 1, DV_A), im(lambda b, h: (l, 0, 0))),
        pl.BlockSpec((GLA_CHUNK, GLA_CHUNK), im(lambda b, h: (0, 0))),
        pl.BlockSpec((hb, hb), im(lambda b, h: (0, 0))),
    ]
    args = [z, z, z, z, z, up_pad, bias.reshape(DEPTH, 2, 1, H_A * DK_A),
            norm_g.reshape(DEPTH, 1, DV_A), tril, msb]
    out_shape = [jax.ShapeDtypeStruct((nb * seq, H_A * DV_A), BF16)]
    out_specs = [pl.BlockSpec((seq, nv), im(lambda b, h: (b, h)))]
    scratch = []
    if has_s0:
        in_specs.append(pl.BlockSpec((None, None, 2, None, DK_A, DV_A), im(lambda b, h: (b, l, 0, h, 0, 0))))
        args.append(s0)
        scratch = [pltpu.VMEM((2, seq, DK_A), F32), pltpu.VMEM((2, seq, DK_A), F32),
                   pltpu.VMEM((n_chunks, 2 * DK_A, DV_A), F32),
                   pltpu.VMEM((n_chunks, 1, 2 * DK_A), F32),
                   pltpu.VMEM((n_chunks, 2 * DK_A, DV_A), F32)]
    else:
        out_shape.append(jax.ShapeDtypeStruct((nb, 2, H_A, DK_A, DV_A), F32))
        out_specs.append(pl.BlockSpec((None, 2, H_A, DK_A, DV_A), im(lambda b, h: (b, 0, 0, 0, 0))))
    return pl.pallas_call(
        functools.partial(_gla_kernel, n_chunks=n_chunks, has_s0=has_s0, heads=heads),
        out_shape=out_shape,
        grid=grid,
        in_specs=in_specs,
        out_specs=out_specs,
        scratch_shapes=scratch,
        compiler_params=_cparams(len(grid)),
        name="gla",
    )(*args)


def kernel(x_prompt, x_sample, c, state_gla, cache_mla_ckv, cache_mla_krope, cache_gqa_k, cache_gqa_v,
           c_ctx, norm1_g, norm2_g, w_ada, b_ada, w_in, gla_gk_up, gla_gk_bias, gla_norm_g,
           mla_qa_norm_g, mla_w_qb, mla_kva_norm_g, mla_w_kvb, mla_q_norm_g, mla_k_norm_g,
           gqa_q_norm_g, gqa_k_norm_g, w_out_a, w_out_b, w_out_c, w_o,
           ffn_w_up, ffn_conv_w, ffn_conv_b, ffn_w_down):
    d = D_MODEL
    x = jnp.concatenate([x_prompt.reshape(T_P, d), x_sample.reshape(T_S, d)], axis=0)
    cvec = jnp.concatenate([c_ctx[None, :], c, jnp.zeros((MOD_ROWS - N_GROUPS, d), F32)], axis=0)
    mods = adaln_all(cvec, w_ada, b_ada)

    w_qb_pad = jnp.pad(mla_w_qb.reshape(DEPTH, Q_LORA_B, H_B, QK_B),
                       ((0, 0), (0, 0), (0, 0), (0, PAD_QK_B - QK_B))).reshape(DEPTH, Q_LORA_B, H_B * PAD_QK_B)
    q_g_pad = jnp.pad(mla_q_norm_g, ((0, 0), (0, PAD_QK_B - QK_B))).reshape(DEPTH, 1, PAD_QK_B)
    k_g_pad = jnp.pad(mla_k_norm_g, ((0, 0), (0, PAD_QK_B - QK_B))).reshape(DEPTH, 1, PAD_QK_B)
    up_pad = jnp.zeros((DEPTH, 2, 128, H_A * DK_A), F32)
    for dd in range(2):
        r0 = dd * GLA_GATE_RANK
        up_pad = up_pad.at[:, dd, r0:r0 + GLA_GATE_RANK, :].set(gla_gk_up[:, dd])
    tab_b = _rope_tables(DEC_SEQ, ROPE_B, 128)
    tab_c = _rope_tables(DEC_SEQ, HD_C, 128)
    tril, msb = _gla_constants()
    w_in_t = jnp.swapaxes(w_in, 1, 2)
    ck = cache_gqa_k.reshape(DEC_BATCH, DEPTH, PAST_LEN, HKV_C * HD_C)
    cv = cache_gqa_v.reshape(DEC_BATCH, DEPTH, PAST_LEN, HKV_C * HD_C)

    st_out, ckv_out, krope_out, kc_out, vc_out = [], [], [], [], []
    for l in range(DEPTH):
        h = norm_mod(x, norm1_g, mods, l, 0, 1)
        z = w_in_proj(h, w_in_t, l)

        ya_p, st = gla(z, up_pad, gla_gk_bias, gla_norm_g, tril, msb, l,
                       row0=0, nb=BATCH, seq=SEQ, s0=None)
        ya_s, = gla(z, up_pad, gla_gk_bias, gla_norm_g, tril, msb, l,
                    row0=T_P, nb=DEC_BATCH, seq=DEC_SEQ, s0=state_gla)

        bmk = 512
        qb_p = mla_q_proj(z, mla_qa_norm_g, w_qb_pad, q_g_pad, l, row0=0, rows=T_P, tab=None)
        qb_s = mla_q_proj(z, mla_qa_norm_g, w_qb_pad, q_g_pad, l, row0=T_P, rows=T_S, tab=tab_b)

        def tail_specs(row0):
            rb = row0 // bmk
            return (pl.BlockSpec((bmk, KV_LORA_B), lambda i: (rb + i, Z_CKV // KV_LORA_B)),
                    pl.BlockSpec((bmk, 128), lambda i: (rb + i, Z_KROPE // 128)))

        c_spec, kr_spec = tail_specs(0)
        kb_p, vb_p, ckv_p = mla_kv_proj(z, c_spec, z, kr_spec, mla_kva_norm_g, mla_w_kvb, k_g_pad, l,
                                        rows=T_P, bm=bmk, norm_in=True, tab=None)
        c_spec, kr_spec = tail_specs(T_P)
        kb_s, vb_s, _ = mla_kv_proj(z, c_spec, z, kr_spec, mla_kva_norm_g, mla_w_kvb, k_g_pad, l,
                                    rows=T_S, bm=bmk, norm_in=True, tab=tab_b)
        kb_c, vb_c = mla_kv_proj(
            cache_mla_ckv, pl.BlockSpec((None, None, PAST_LEN, KV_LORA_B), lambda i: (i, l, 0, 0)),
            cache_mla_krope, pl.BlockSpec((None, None, PAST_LEN, ROPE_B), lambda i: (i, l, 0, 0)),
            mla_kva_norm_g, mla_w_kvb, k_g_pad, l, rows=DEC_BATCH * PAST_LEN, bm=PAST_LEN, norm_in=False, tab=None)
        nkb, nvb = H_B * PAD_QK_B, H_B * V_B
        yb_p = attention(qb_p, [(kb_p, _rows_spec(SEQ, nkb), vb_p, _rows_spec(SEQ, nvb), SEQ)],
                         nb=BATCH, q_rows=SEQ, hq=H_B, hkv=H_B, dqk=PAD_QK_B, dv=V_B, name="mla_attn_p")
        yb_s = attention(qb_s, [(kb_s, _rows_spec(DEC_SEQ, nkb), vb_s, _rows_spec(DEC_SEQ, nvb), DEC_SEQ),
                                (kb_c, _rows_spec(PAST_LEN, nkb), vb_c, _rows_spec(PAST_LEN, nvb), PAST_LEN)],
                         nb=DEC_BATCH, q_rows=DEC_SEQ, hq=H_B, hkv=H_B, dqk=PAD_QK_B, dv=V_B, name="mla_attn_s")

        qc_p, kc_p, vc_p, kn_p = gqa_prep(z, gqa_q_norm_g, gqa_k_norm_g, l, row0=0, rows=T_P,
                                          tab=None, emit_kn=True)
        qc_s, kc_s, vc_s = gqa_prep(z, gqa_q_norm_g, gqa_k_norm_g, l, row0=T_P, rows=T_S,
                                    tab=tab_c, emit_kn=False)
        nkc = HKV_C * HD_C
        yc_p = attention(qc_p, [(kc_p, _rows_spec(SEQ, nkc), vc_p, _rows_spec(SEQ, nkc), SEQ)],
                         nb=BATCH, q_rows=SEQ, hq=HQ_C, hkv=HKV_C, dqk=HD_C, dv=HD_C, name="gqa_attn_p")
        cache_spec = pl.BlockSpec((None, None, PAST_LEN, nkc), lambda b: (b, l, 0, 0))
        yc_s = attention(qc_s, [(kc_s, _rows_spec(DEC_SEQ, nkc), vc_s, _rows_spec(DEC_SEQ, nkc), DEC_SEQ),
                                (ck, cache_spec, cv, cache_spec, PAST_LEN)],
                         nb=DEC_BATCH, q_rows=DEC_SEQ, hq=HQ_C, hkv=HKV_C, dqk=HD_C, dv=HD_C, name="gqa_attn_s")

        m = gated_merge((ya_p, ya_s, yb_p, yb_s, yc_p, yc_s), z, w_out_a, w_out_b, w_out_c, l)
        x, h2 = w_o_norm(m, w_o, l, x, mods, norm2_g)
        act = up_conv(h2, ffn_w_up, ffn_conv_w, ffn_conv_b, l)
        x = matmul_residual(act, ffn_w_down, l, x, mods, 5, bm=512, bn=512, name="ffn_down")

        st_out.append(st)
        ckv_out.append(ckv_p.reshape(BATCH, SEQ, KV_LORA_B))
        krope_out.append(z[:T_P, Z_KROPE:Z_KROPE + ROPE_B].astype(F32).reshape(BATCH, SEQ, ROPE_B))
        kc_out.append(kn_p.reshape(BATCH, SEQ, HKV_C, HD_C))
        vc_out.append(z[:T_P, Z_VC:Z_VC + nkc].astype(F32).reshape(BATCH, SEQ, HKV_C, HD_C))

    y_p = x[:T_P].reshape(BATCH, SEQ, d)
    y_s = x[T_P:].reshape(DEC_BATCH, DEC_SEQ, d)
    return (y_p, y_s, jnp.stack(st_out, axis=1), jnp.stack(ckv_out, axis=1), jnp.stack(krope_out, axis=1),
            jnp.stack(kc_out, axis=1), jnp.stack(vc_out, axis=1))
```

```python
import functools

import numpy as np
import jax
import jax.numpy as jnp
from jax import lax
from jax.experimental import pallas as pl
from jax.experimental.pallas import tpu as pltpu

F32 = jnp.float32
BF16 = jnp.bfloat16

D_MODEL = 2048
BATCH = 16
SEQ = 256
DEPTH = 4
DEC_BATCH = 2
DEC_SEQ = 1024
PAST_LEN = 512
GRID_W = 64
ROPE_THETA = 10000.0
EPS = 1e-6
H_A, DK_A, DV_A = 4, 128, 256
GLA_GATE_RANK = 16
GLA_GATE_NORM = 16.0
H_B, NOPE_B, ROPE_B, V_B = 8, 128, 64, 128
QK_B = NOPE_B + ROPE_B
Q_LORA_B = 512
KV_LORA_B = 256
HQ_C, HKV_C, HD_C = 8, 2, 128
D_FF = 5632

T_P = BATCH * SEQ
T_S = DEC_BATCH * DEC_SEQ
T_ALL = T_P + T_S
N_GROUPS = 1 + DEC_BATCH
SUBLANES = 8
MOD_ROWS = SUBLANES

_O_QA, _O_KA, _O_VA, _O_GA = 0, 512, 1024, 2048
_O_GKA = 3072
_O_QLORA = 3104
_O_KVA = 3616
_O_QC, _O_KC, _O_VC = 3936, 4960, 5216
_O_GATES = 5472
N_IN = 11616
W_IN_BN = 512
W_IN_RANGES = ((_O_GKA, 1), (_O_QC, 15), (_O_QA, 6), (_O_QLORA, 2))
N_Z = W_IN_BN * sum(nb for _, nb in W_IN_RANGES)
Z_GK = 0
Z_QC, Z_KC, Z_VC, Z_GATES = 512, 1536, 1792, 2048
Z_QA, Z_KA, Z_VA, Z_GA = 8192, 8704, 9216, 10240
Z_QLORA, Z_CKV, Z_KROPE = 11264, 11776, 12032
PAD_QK_B = 256

GLA_CHUNK = 256
GLA_LEVELS = 8

VMEM_LIMIT = 56 * 1024 * 1024
LOG2E = 1.4426950408889634


def _cparams(n_axes):
    return pltpu.CompilerParams(
        dimension_semantics=("arbitrary",) * n_axes, vmem_limit_bytes=VMEM_LIMIT)


def _group_of_block(i, bm):
    n_p = T_P // bm
    per = DEC_SEQ // bm
    return jnp.where(i < n_p, 0, 1 + (i - n_p) // per)


def _adaln_kernel(c_ref, w_ref, b_ref, o_ref):
    c = c_ref[...]
    a = (c * jax.nn.sigmoid(c)).astype(BF16)
    o_ref[...] = jnp.dot(a, w_ref[...].astype(BF16), preferred_element_type=F32) + b_ref[...]


def adaln_all(cvec, w_ada, b_ada):
    bn = 1024
    n = 6 * D_MODEL
    return pl.pallas_call(
        _adaln_kernel,
        out_shape=jax.ShapeDtypeStruct((DEPTH, MOD_ROWS, n), F32),
        grid=(DEPTH, n // bn),
        in_specs=[
            pl.BlockSpec((MOD_ROWS, D_MODEL), lambda l, j: (0, 0)),
            pl.BlockSpec((None, D_MODEL, bn), lambda l, j: (l, 0, j)),
            pl.BlockSpec((None, 1, bn), lambda l, j: (l, 0, j)),
        ],
        out_specs=pl.BlockSpec((None, MOD_ROWS, bn), lambda l, j: (l, 0, j)),
        compiler_params=_cparams(2),
        name="adaln",
    )(cvec, w_ada, b_ada.reshape(DEPTH, 1, n))


def _norm_mod_kernel(x_ref, g_ref, sh_ref, sc_ref, o_ref, *, bm):
    grp = _group_of_block(pl.program_id(0), bm)
    x = x_ref[...]
    y = x * lax.rsqrt(jnp.mean(x * x, axis=-1, keepdims=True) + EPS) * g_ref[...]
    sh = sh_ref[pl.ds(grp, 1), :]
    sc = sc_ref[pl.ds(grp, 1), :]
    o_ref[...] = (y * (1.0 + sc) + sh).astype(o_ref.dtype)


def norm_mod(x, g, mods, l, which_shift, which_scale):
    bm = 1024
    nd = D_MODEL
    return pl.pallas_call(
        functools.partial(_norm_mod_kernel, bm=bm),
        out_shape=jax.ShapeDtypeStruct((T_ALL, nd), BF16),
        grid=(T_ALL // bm,),
        in_specs=[
            pl.BlockSpec((bm, nd), lambda i: (i, 0)),
            pl.BlockSpec((None, 1, nd), lambda i: (l, 0, 0)),
            pl.BlockSpec((None, MOD_ROWS, nd), lambda i: (l, 0, which_shift)),
            pl.BlockSpec((None, MOD_ROWS, nd), lambda i: (l, 0, which_scale)),
        ],
        out_specs=pl.BlockSpec((bm, nd), lambda i: (i, 0)),
        compiler_params=_cparams(1),
        name="norm_mod",
    )(x, g.reshape(DEPTH, 1, nd), mods, mods)


def _dot_nt(a, b):
    return lax.dot_general(a, b, (((1,), (1,)), ((), ())), preferred_element_type=F32)


ROW_BLK = 1024


def _w_in_kernel(h_ref, wt_ref, o_ref, wb_ref):
    wb_ref[...] = wt_ref[0].T.astype(BF16)

    for r in range(T_ALL // ROW_BLK):
        rows = slice(r * ROW_BLK, (r + 1) * ROW_BLK)
        o_ref[rows, :] = jnp.dot(h_ref[rows, :], wb_ref[...], preferred_element_type=F32).astype(o_ref.dtype)


def _w_in_first_col(j):
    col8 = None
    start = sum(nb for _, nb in W_IN_RANGES)
    for c0, nb in reversed(W_IN_RANGES):
        assert c0 % 8 == 0
        start -= nb
        here = c0 // 8 + (j - start) * (W_IN_BN // 8)
        col8 = here if col8 is None else jnp.where(j < start + nb, here, col8)
    return col8 * 8


def w_in_proj(h, w_in_t, l):
    bn = W_IN_BN
    return pl.pallas_call(
        _w_in_kernel,
        out_shape=jax.ShapeDtypeStruct((T_ALL, N_Z), BF16),
        grid=(N_Z // bn,),
        in_specs=[
            pl.BlockSpec((T_ALL, D_MODEL), lambda j: (0, 0), pipeline_mode=pl.Buffered(1)),
            pl.BlockSpec((pl.Element(1), pl.Element(bn), pl.Element(D_MODEL)),
                         lambda j: (l, _w_in_first_col(j), 0)),
        ],
        out_specs=pl.BlockSpec((T_ALL, bn), lambda j: (0, j)),
        scratch_shapes=[pltpu.VMEM((D_MODEL, bn), BF16)],
        compiler_params=_cparams(1),
        name="w_in",
    )(h, w_in_t)


def _mm_res_kernel(a_ref, w_ref, x_ref, gate_ref, o_ref, wb_ref, *, bm):
    @pl.when(pl.program_id(1) == 0)
    def _():
        wb_ref[...] = w_ref[...].astype(BF16)

    grp = _group_of_block(pl.program_id(1), bm)
    acc = jnp.dot(a_ref[...], wb_ref[...], preferred_element_type=F32)
    o_ref[...] = x_ref[...] + gate_ref[pl.ds(grp, 1), :] * acc


def matmul_residual(a, w, l, x, mods, which_gate, *, bm, bn, name):
    m, k = a.shape
    n = w.shape[-1]
    gate_blk = which_gate * (D_MODEL // bn)
    return pl.pallas_call(
        functools.partial(_mm_res_kernel, bm=bm),
        out_shape=jax.ShapeDtypeStruct((m, n), F32),
        grid=(n // bn, m // bm),
        in_specs=[
            pl.BlockSpec((bm, k), lambda j, i: (i, 0)),
            pl.BlockSpec((None, k, bn), lambda j, i: (l, 0, j)),
            pl.BlockSpec((bm, bn), lambda j, i: (i, j)),
            pl.BlockSpec((None, MOD_ROWS, bn), lambda j, i: (l, 0, gate_blk + j)),
        ],
        out_specs=pl.BlockSpec((bm, bn), lambda j, i: (i, j)),
        scratch_shapes=[pltpu.VMEM((k, bn), BF16)],
        compiler_params=_cparams(2),
        name=name,
    )(a, w, x, mods)


def _w_o_norm_kernel(m_ref, w_ref, x_ref, gate_ref, g_ref, sh_ref, sc_ref, xo_ref, h_ref, wb_ref, *, bm):
    i = pl.program_id(0)

    @pl.when(i == 0)
    def _():
        wb_ref[...] = w_ref[...].astype(BF16)

    grp = _group_of_block(i, bm)
    acc = jnp.dot(m_ref[...], wb_ref[...], preferred_element_type=F32)
    x = x_ref[...] + gate_ref[pl.ds(grp, 1), :] * acc
    xo_ref[...] = x
    y = x * lax.rsqrt(jnp.mean(x * x, axis=-1, keepdims=True) + EPS) * g_ref[...]
    h_ref[...] = (y * (1.0 + sc_ref[pl.ds(grp, 1), :]) + sh_ref[pl.ds(grp, 1), :]).astype(h_ref.dtype)


def w_o_norm(m, w_o, l, x, mods, norm_g):
    bm = 512
    nd = D_MODEL

    def mod_spec(which):
        return pl.BlockSpec((None, MOD_ROWS, nd), lambda i: (l, 0, which))

    return pl.pallas_call(
        functools.partial(_w_o_norm_kernel, bm=bm),
        out_shape=[jax.ShapeDtypeStruct((T_ALL, nd), F32), jax.ShapeDtypeStruct((T_ALL, nd), BF16)],
        grid=(T_ALL // bm,),
        in_specs=[
            pl.BlockSpec((bm, nd), lambda i: (i, 0)),
            pl.BlockSpec((None, nd, nd), lambda i: (l, 0, 0), pipeline_mode=pl.Buffered(1)),
            pl.BlockSpec((bm, nd), lambda i: (i, 0)),
            mod_spec(2),
            pl.BlockSpec((None, 1, nd), lambda i: (l, 0, 0)),
            mod_spec(3), mod_spec(4),
        ],
        out_specs=[pl.BlockSpec((bm, nd), lambda i: (i, 0)), pl.BlockSpec((bm, nd), lambda i: (i, 0))],
        scratch_shapes=[pltpu.VMEM((nd, nd), BF16)],
        compiler_params=_cparams(1),
        name="w_o_norm",
    )(m, w_o, x, mods, norm_g.reshape(DEPTH, 1, nd), mods, mods)


def _merge_kernel(yap_ref, yas_ref, ybp_ref, ybs_ref, ycp_ref, ycs_ref, wa_ref, wb_ref, wc_ref,
                  ga_ref, gb_ref, gc_ref, o_ref, wab_ref, wbb_ref, wcb_ref, *, bm):
    i = pl.program_id(1)

    @pl.when(i == 0)
    def _():
        wab_ref[...] = wa_ref[...].astype(BF16)
        wbb_ref[...] = wb_ref[...].astype(BF16)
        wcb_ref[...] = wc_ref[...].astype(BF16)

    is_ctx = i < T_P // bm

    def branch(yp_ref, ys_ref, w_ref, g_ref):
        y = jnp.where(is_ctx, yp_ref[...], ys_ref[...])
        return jax.nn.sigmoid(g_ref[...].astype(F32)) * jnp.dot(y, w_ref[...], preferred_element_type=F32)

    m = (branch(yap_ref, yas_ref, wab_ref, ga_ref) + branch(ybp_ref, ybs_ref, wbb_ref, gb_ref)
         + branch(ycp_ref, ycs_ref, wcb_ref, gc_ref))
    o_ref[...] = m.astype(o_ref.dtype)


def gated_merge(ys, z, w_out_a, w_out_b, w_out_c, l):
    bm, bn = 512, 1024
    k = 1024
    gate0 = Z_GATES // bn
    per = D_MODEL // bn
    n_p = T_P // bm
    yp_spec = pl.BlockSpec((bm, k), lambda j, i: (jnp.minimum(i, n_p - 1), 0))
    ys_spec = pl.BlockSpec((bm, k), lambda j, i: (jnp.maximum(i - n_p, 0), 0))
    w_spec = pl.BlockSpec((None, k, bn), lambda j, i: (l, 0, j))

    def g_spec(br):
        return pl.BlockSpec((bm, bn), lambda j, i: (i, gate0 + br * per + j))

    return pl.pallas_call(
        functools.partial(_merge_kernel, bm=bm),
        out_shape=jax.ShapeDtypeStruct((T_ALL, D_MODEL), BF16),
        grid=(D_MODEL // bn, T_ALL // bm),
        in_specs=[yp_spec, ys_spec] * 3 + [w_spec] * 3 + [g_spec(0), g_spec(1), g_spec(2)],
        out_specs=pl.BlockSpec((bm, bn), lambda j, i: (i, j)),
        scratch_shapes=[pltpu.VMEM((k, bn), BF16)] * 3,
        compiler_params=_cparams(2),
        name="gated_merge",
    )(*ys, w_out_a, w_out_b, w_out_c, z, z, z)


def _up_conv_kernel(h_ref, wa_ref, wg_ref, cwa_ref, cwg_ref, cba_ref, cbg_ref, o_ref, wab_ref, wgb_ref,
                    *sc_refs):
    wab_ref[...] = wa_ref[...].astype(BF16)
    wgb_ref[...] = wg_ref[...].astype(BF16)
    bm = ROW_BLK
    nt = bm // SUBLANES
    bn = o_ref.shape[1]
    sub = lax.broadcasted_iota(jnp.int32, (1, SUBLANES, 1), 1)
    for sc_ref in sc_refs:
        sc_ref[0:SUBLANES, :] = jnp.zeros((SUBLANES, bn), F32)
        sc_ref[SUBLANES + bm:2 * SUBLANES + bm, :] = jnp.zeros((SUBLANES, bn), F32)

    def taps(cw_ref, cb_ref):
        return [jnp.broadcast_to(cw_ref[k:k + 1, :], (SUBLANES, bn))[None] for k in range(3)] + [
            jnp.broadcast_to(cb_ref[...], (SUBLANES, bn))[None]]

    taps_a = taps(cwa_ref, cba_ref)
    taps_g = taps(cwg_ref, cbg_ref)

    for r in range(T_ALL // bm):
        seq = SEQ if r < T_P // bm else DEC_SEQ
        tps = seq // SUBLANES
        n_seq = bm // seq
        h = h_ref[r * bm:(r + 1) * bm, :]

        def conv(w_ref, tp, sc_ref):
            u = jnp.dot(h, w_ref[...], preferred_element_type=F32)
            sc_ref[SUBLANES:SUBLANES + bm, :] = u
            prev = sc_ref[SUBLANES - 1:SUBLANES - 1 + bm, :].reshape(nt, SUBLANES, bn)
            nxt = sc_ref[SUBLANES + 1:SUBLANES + 1 + bm, :].reshape(nt, SUBLANES, bn)
            if n_seq > 1:
                prev = jnp.concatenate(
                    [p for s in range(n_seq) for p in (
                        jnp.where(sub == 0, 0.0, prev[s * tps:s * tps + 1]), prev[s * tps + 1:(s + 1) * tps])],
                    axis=0)
                nxt = jnp.concatenate(
                    [p for s in range(n_seq) for p in (
                        nxt[s * tps:(s + 1) * tps - 1],
                        jnp.where(sub == SUBLANES - 1, 0.0, nxt[(s + 1) * tps - 1:(s + 1) * tps]))],
                    axis=0)
            return prev * tp[0] + u.reshape(nt, SUBLANES, bn) * tp[1] + nxt * tp[2] + tp[3]

        a = conv(wab_ref, taps_a, sc_refs[0])
        g = conv(wgb_ref, taps_g, sc_refs[1])
        act = (g * jax.nn.sigmoid(g) * a).reshape(bm, bn)
        o_ref[r * bm:(r + 1) * bm, :] = act.astype(o_ref.dtype)


def up_conv(h2, w_up, conv_w, conv_b, l):
    bn = 256
    nb = D_FF // bn
    cb = conv_b.reshape(DEPTH, 1, 2 * D_FF)
    return pl.pallas_call(
        _up_conv_kernel,
        out_shape=jax.ShapeDtypeStruct((T_ALL, D_FF), BF16),
        grid=(nb,),
        in_specs=[
            pl.BlockSpec((T_ALL, D_MODEL), lambda j: (0, 0), pipeline_mode=pl.Buffered(1)),
            pl.BlockSpec((None, D_MODEL, bn), lambda j: (l, 0, j)),
            pl.BlockSpec((None, D_MODEL, bn), lambda j: (l, 0, nb + j)),
            pl.BlockSpec((None, 3, bn), lambda j: (l, 0, j)),
            pl.BlockSpec((None, 3, bn), lambda j: (l, 0, nb + j)),
            pl.BlockSpec((None, 1, bn), lambda j: (l, 0, j)),
            pl.BlockSpec((None, 1, bn), lambda j: (l, 0, nb + j)),
        ],
        out_specs=pl.BlockSpec((T_ALL, bn), lambda j: (0, j)),
        scratch_shapes=[pltpu.VMEM((D_MODEL, bn), BF16)] * 2
        + [pltpu.VMEM((ROW_BLK + 2 * SUBLANES, bn), F32)] * 2,
        compiler_params=_cparams(1),
        name="up_conv",
    )(h2, w_up, w_up, conv_w, conv_w, cb, cb)


def _rope_tables(n_tok, rot_dim, width):
    rows = n_tok // GRID_W
    row = jnp.repeat(jnp.arange(rows, dtype=F32), GRID_W)
    col = jnp.tile(jnp.arange(GRID_W, dtype=F32), rows)
    half = rot_dim // 2
    inv = jnp.power(ROPE_THETA, -jnp.arange(half // 2, dtype=F32) * (2.0 / half))
    ang = jnp.concatenate([row[:, None] * inv, col[:, None] * inv], axis=-1)
    cos, sin = jnp.cos(ang), jnp.sin(ang)
    zero = jnp.zeros_like(sin)
    c2 = jnp.stack([cos, cos], axis=-1).reshape(n_tok, rot_dim)
    s_even = jnp.stack([-sin, zero], axis=-1).reshape(n_tok, rot_dim)
    s_odd = jnp.stack([zero, sin], axis=-1).reshape(n_tok, rot_dim)
    pad = ((0, 0), (0, width - rot_dim))
    return jnp.stack([jnp.pad(c2, pad), jnp.pad(s_even, pad), jnp.pad(s_odd, pad)], axis=0)


def _rope(x, tab_ref):
    w = x.shape[-1]
    return (x * tab_ref[0] + pltpu.roll(x, w - 1, 1) * tab_ref[1] + pltpu.roll(x, 1, 1) * tab_ref[2])


def _mla_q_kernel(ql_ref, g1_ref, w_ref, g2_ref, *rest, rope):
    if rope:
        tab_ref, o_ref, wb_ref = rest
    else:
        o_ref, wb_ref = rest

    @pl.when(pl.program_id(0) == 0)
    def _():
        wb_ref[...] = w_ref[...].astype(BF16)

    x = ql_ref[...].astype(F32)
    xn = x * lax.rsqrt(jnp.mean(x * x, axis=-1, keepdims=True) + EPS) * g1_ref[...]
    q = jnp.dot(xn.astype(BF16), wb_ref[...], preferred_element_type=F32)
    scale = QK_B ** -0.5 * LOG2E
    for h in range(H_B):
        lo = h * PAD_QK_B
        qn = q[:, lo:lo + NOPE_B]
        qr = q[:, lo + NOPE_B:lo + PAD_QK_B]
        ms = (jnp.sum(qn * qn, axis=-1, keepdims=True) + jnp.sum(qr * qr, axis=-1, keepdims=True)) / QK_B
        r = lax.rsqrt(ms + EPS) * scale
        qn = qn * r * g2_ref[:, 0:NOPE_B]
        qr = qr * r * g2_ref[:, NOPE_B:PAD_QK_B]
        if rope:
            qr = _rope(qr, tab_ref)
        o_ref[:, lo:lo + NOPE_B] = qn.astype(o_ref.dtype)
        o_ref[:, lo + NOPE_B:lo + PAD_QK_B] = qr.astype(o_ref.dtype)


def mla_q_proj(z, qa_g, w_qb_pad, q_g_pad, l, *, row0, rows, tab):
    bm = 1024
    rope = tab is not None
    kq = Q_LORA_B
    n = H_B * PAD_QK_B
    rb0 = row0 // bm
    in_specs = [
        pl.BlockSpec((bm, kq), lambda i: (rb0 + i, Z_QLORA // kq)),
        pl.BlockSpec((None, 1, kq), lambda i: (l, 0, 0)),
        pl.BlockSpec((None, kq, n), lambda i: (l, 0, 0)),
        pl.BlockSpec((None, 1, PAD_QK_B), lambda i: (l, 0, 0)),
    ]
    args = [z, qa_g.reshape(DEPTH, 1, kq), w_qb_pad, q_g_pad]
    if rope:
        per = DEC_SEQ // bm
        in_specs.append(pl.BlockSpec((3, bm, 128), lambda i: (0, i % per, 0)))
        args.append(tab)
    return pl.pallas_call(
        functools.partial(_mla_q_kernel, rope=rope),
        out_shape=jax.ShapeDtypeStruct((rows, n), BF16),
        grid=(rows // bm,),
        in_specs=in_specs,
        out_specs=pl.BlockSpec((bm, n), lambda i: (i, 0)),
        scratch_shapes=[pltpu.VMEM((kq, n), BF16)],
        compiler_params=_cparams(1),
        name="mla_q_proj",
    )(*args)


def _mla_kv_kernel(c_ref, kr_ref, g1_ref, w_ref, g2_ref, *rest, norm_in, rope):
    rest = list(rest)
    tab_ref = rest.pop(0) if rope else None
    if norm_in:
        k_ref, v_ref, ckv_ref, wb_ref = rest
    else:
        k_ref, v_ref, wb_ref = rest

    @pl.when(pl.program_id(0) == 0)
    def _():
        wb_ref[...] = w_ref[...].astype(BF16)

    c = c_ref[...].astype(F32)
    if norm_in:
        c = c * lax.rsqrt(jnp.mean(c * c, axis=-1, keepdims=True) + EPS) * g1_ref[...]
        ckv_ref[...] = c
    kv = jnp.dot(c.astype(BF16), wb_ref[...], preferred_element_type=F32)
    kr = kr_ref[...].astype(F32)
    if kr.shape[-1] == 128:
        lane = lax.broadcasted_iota(jnp.int32, kr.shape, 1)
        kr = jnp.where(lane < ROPE_B, kr, 0.0)
    else:
        kr = jnp.concatenate([kr, jnp.zeros_like(kr)], axis=-1)
    kr_ss = jnp.sum(kr * kr, axis=-1, keepdims=True)
    for h in range(H_B):
        kn = kv[:, h * 256:h * 256 + NOPE_B]
        v = kv[:, h * 256 + NOPE_B:(h + 1) * 256]
        r = lax.rsqrt((jnp.sum(kn * kn, axis=-1, keepdims=True) + kr_ss) / QK_B + EPS)
        krh = kr * r * g2_ref[:, NOPE_B:PAD_QK_B]
        if rope:
            krh = _rope(krh, tab_ref)
        k_ref[:, h * PAD_QK_B:h * PAD_QK_B + NOPE_B] = (kn * r * g2_ref[:, 0:NOPE_B]).astype(k_ref.dtype)
        k_ref[:, h * PAD_QK_B + NOPE_B:(h + 1) * PAD_QK_B] = krh.astype(k_ref.dtype)
        v_ref[:, h * V_B:(h + 1) * V_B] = v.astype(v_ref.dtype)


def mla_kv_proj(c_arr, c_spec, kr_arr, kr_spec, kva_g, w_kvb, k_g_pad, l, *, rows, bm, norm_in, tab):
    rope = tab is not None
    kc = KV_LORA_B
    n = H_B * (NOPE_B + V_B)
    in_specs = [
        c_spec, kr_spec,
        pl.BlockSpec((None, 1, kc), lambda i: (l, 0, 0)),
        pl.BlockSpec((None, kc, n), lambda i: (l, 0, 0)),
        pl.BlockSpec((None, 1, PAD_QK_B), lambda i: (l, 0, 0)),
    ]
    args = [c_arr, kr_arr, kva_g.reshape(DEPTH, 1, kc), w_kvb, k_g_pad]
    if rope:
        per = DEC_SEQ // bm
        in_specs.append(pl.BlockSpec((3, bm, 128), lambda i: (0, i % per, 0)))
        args.append(tab)
    out_shape = [jax.ShapeDtypeStruct((rows, H_B * PAD_QK_B), BF16),
                 jax.ShapeDtypeStruct((rows, H_B * V_B), BF16)]
    out_specs = [pl.BlockSpec((bm, H_B * PAD_QK_B), lambda i: (i, 0)),
                 pl.BlockSpec((bm, H_B * V_B), lambda i: (i, 0))]
    if norm_in:
        out_shape.append(jax.ShapeDtypeStruct((rows, kc), F32))
        out_specs.append(pl.BlockSpec((bm, kc), lambda i: (i, 0)))
    return pl.pallas_call(
        functools.partial(_mla_kv_kernel, norm_in=norm_in, rope=rope),
        out_shape=out_shape,
        grid=(rows // bm,),
        in_specs=in_specs,
        out_specs=out_specs,
        scratch_shapes=[pltpu.VMEM((kc, n), BF16)],
        compiler_params=_cparams(1),
        name="mla_kv_proj",
    )(*args)


def _gqa_prep_kernel(qlo_ref, qhi_ref, k_ref, v_ref, gq_ref, gk_ref, *rest, rope, emit_kn):
    rest = list(rest)
    tab_ref = rest.pop(0) if rope else None
    qo_ref, ko_ref, vo_ref = rest[:3]
    kn_ref = rest[3] if emit_kn else None
    scale = HD_C ** -0.5 * LOG2E
    half = HQ_C // 2

    def norm(x, g_ref):
        return x * lax.rsqrt(jnp.mean(x * x, axis=-1, keepdims=True) + EPS) * g_ref[...]

    for h in range(HQ_C):
        q_ref, hh = (qlo_ref, h) if h < half else (qhi_ref, h - half)
        x = norm(q_ref[:, hh * HD_C:(hh + 1) * HD_C].astype(F32), gq_ref)
        if rope:
            x = _rope(x, tab_ref)
        qo_ref[:, h * HD_C:(h + 1) * HD_C] = (x * scale).astype(qo_ref.dtype)
    for h in range(HKV_C):
        x = norm(k_ref[:, h * HD_C:(h + 1) * HD_C].astype(F32), gk_ref)
        if emit_kn:
            kn_ref[:, h * HD_C:(h + 1) * HD_C] = x
        if rope:
            x = _rope(x, tab_ref)
        ko_ref[:, h * HD_C:(h + 1) * HD_C] = x.astype(ko_ref.dtype)
    vo_ref[...] = v_ref[...].astype(vo_ref.dtype)


def gqa_prep(z, gq, gk, l, *, row0, rows, tab, emit_kn):
    bm = 1024
    rope = tab is not None
    rb0 = row0 // bm
    nq, nk = HQ_C * HD_C, HKV_C * HD_C
    nqh = nq // 2
    in_specs = [
        pl.BlockSpec((bm, nqh), lambda i: (rb0 + i, Z_QC // nqh)),
        pl.BlockSpec((bm, nqh), lambda i: (rb0 + i, Z_QC // nqh + 1)),
        pl.BlockSpec((bm, nk), lambda i: (rb0 + i, Z_KC // nk)),
        pl.BlockSpec((bm, nk), lambda i: (rb0 + i, Z_VC // nk)),
        pl.BlockSpec((None, 1, HD_C), lambda i: (l, 0, 0)),
        pl.BlockSpec((None, 1, HD_C), lambda i: (l, 0, 0)),
    ]
    args = [z, z, z, z, gq.reshape(DEPTH, 1, HD_C), gk.reshape(DEPTH, 1, HD_C)]
    if rope:
        per = DEC_SEQ // bm
        in_specs.append(pl.BlockSpec((3, bm, 128), lambda i: (0, i % per, 0)))
        args.append(tab)
    out_shape = [jax.ShapeDtypeStruct((rows, nq), BF16),
                 jax.ShapeDtypeStruct((rows, nk), BF16),
                 jax.ShapeDtypeStruct((rows, nk), BF16)]
    out_specs = [pl.BlockSpec((bm, nq), lambda i: (i, 0)),
                 pl.BlockSpec((bm, nk), lambda i: (i, 0)),
                 pl.BlockSpec((bm, nk), lambda i: (i, 0))]
    if emit_kn:
        out_shape.append(jax.ShapeDtypeStruct((rows, nk), F32))
        out_specs.append(pl.BlockSpec((bm, nk), lambda i: (i, 0)))
    return pl.pallas_call(
        functools.partial(_gqa_prep_kernel, rope=rope, emit_kn=emit_kn),
        out_shape=out_shape,
        grid=(rows // bm,),
        in_specs=in_specs,
        out_specs=out_specs,
        compiler_params=_cparams(1),
        name="gqa_prep",
    )(*args)


def _attn_kernel(*refs, n_seg, seg_rows, hq, hkv, dqk, dv, tq):
    q_ref = refs[0]
    kv_refs = refs[1:1 + 2 * n_seg]
    o_ref = refs[1 + 2 * n_seg]
    k_sc, v_sc = refs[2 + 2 * n_seg:]
    g = hq // hkv
    nq = q_ref.shape[0] // tq
    for j in range(hkv):
        r = 0
        for s in range(n_seg):
            k_sc[j, r:r + seg_rows[s], :] = kv_refs[2 * s][:, j * dqk:(j + 1) * dqk].astype(BF16)
            v_sc[j, r:r + seg_rows[s], :] = kv_refs[2 * s + 1][:, j * dv:(j + 1) * dv].astype(BF16)
            r += seg_rows[s]

    def q_block(qb, carry):
        rows = pl.ds(pl.multiple_of(qb * tq, tq), tq)
        for h in range(hq):
            j = h // g
            q = q_ref[rows, h * dqk:(h + 1) * dqk]
            s_ = _dot_nt(q, k_sc[j])
            m = jnp.max(s_, axis=-1, keepdims=True)
            p = jnp.exp2(s_ - m)
            den = jnp.sum(p, axis=-1, keepdims=True)
            o = jnp.dot(p.astype(BF16), v_sc[j], preferred_element_type=F32) / den
            o_ref[rows, h * dv:(h + 1) * dv] = o.astype(o_ref.dtype)
        return carry

    if nq == 1:
        q_block(0, 0)
    else:
        lax.fori_loop(0, nq, q_block, 0)


def attention(q, segs, *, nb, q_rows, hq, hkv, dqk, dv, name):
    tq = 256
    in_specs = [pl.BlockSpec((q_rows, hq * dqk), lambda b: (b, 0))]
    args = [q]
    seg_rows = []
    for k_arr, k_spec, v_arr, v_spec, rows in segs:
        in_specs += [k_spec, v_spec]
        args += [k_arr, v_arr]
        seg_rows.append(rows)
    s_tot = sum(seg_rows)
    return pl.pallas_call(
        functools.partial(_attn_kernel, n_seg=len(segs), seg_rows=tuple(seg_rows), hq=hq, hkv=hkv,
                          dqk=dqk, dv=dv, tq=tq),
        out_shape=jax.ShapeDtypeStruct((nb * q_rows, hq * dv), BF16),
        grid=(nb,),
        in_specs=in_specs,
        out_specs=pl.BlockSpec((q_rows, hq * dv), lambda b: (b, 0)),
        scratch_shapes=[pltpu.VMEM((hkv, s_tot, dqk), BF16), pltpu.VMEM((hkv, s_tot, dv), BF16)],
        compiler_params=_cparams(1),
        name=name,
    )(*args)


def _rows_spec(rows, width, col_blk=0):
    return pl.BlockSpec((rows, width), lambda b: (b, col_blk))


def _gla_constants():
    c = GLA_CHUNK
    t = np.arange(c)[:, None]
    u = np.arange(c)[None, :]
    tril = (u <= t).astype(np.float32)
    hb = c // 2
    x = (t ^ u)[:hb, :hb]
    msb = np.where(x > 0, np.floor(np.log2(np.maximum(x, 1))), GLA_LEVELS - 1).astype(np.int32)
    return jnp.asarray(tril, BF16), jnp.asarray(msb)


def _split3(x):
    hi = x.astype(BF16)
    r = x - hi.astype(F32)
    mid = r.astype(BF16)
    lo = (r - mid.astype(F32)).astype(BF16)
    return [hi, mid, lo]


def _exp2_neg_abs(x):
    return jnp.exp2(-jnp.abs(x))


def _gla_kernel(q_ref, k_ref, v_ref, ga_ref, gk_ref, up_ref, bias_ref, ng_ref, tril_ref, msb_ref, *rest,
                n_chunks, has_s0, heads):
    rest = list(rest)
    s0_ref = rest.pop(0) if has_s0 else None
    y_ref = rest.pop(0)
    st_ref = None if has_s0 else rest.pop(0)
    c = GLA_CHUNK
    hb = c // 2
    nl = GLA_LEVELS
    dk = DK_A
    dv = DV_A
    row = lax.broadcasted_iota(jnp.int32, (c, 1), 0)

    def pass1(hd, rows):
        gk = gk_ref[rows, :].astype(BF16)
        las = []
        for d in range(2):
            zg = (jnp.dot(gk, up_ref[d, :, hd * dk:(hd + 1) * dk].astype(BF16), preferred_element_type=F32)
                  + bias_ref[d, :, hd * dk:(hd + 1) * dk])
            log_sig = jnp.minimum(zg, 0.0) - jnp.log1p(jnp.exp(-jnp.abs(zg)))
            las.append(log_sig * (LOG2E / GLA_GATE_NORM))
        parts = jnp.concatenate(_split3(las[0]) + _split3(las[1]), axis=1)
        cs = jnp.dot(tril_ref[...], parts, preferred_element_type=F32)
        cums = [cs[:, 0:dk] + cs[:, dk:2 * dk] + cs[:, 2 * dk:3 * dk],
                cs[:, 3 * dk:4 * dk] + cs[:, 4 * dk:5 * dk] + cs[:, 5 * dk:6 * dk]]
        tot_f = cums[0][c - 1:c, :]
        tot = jnp.concatenate([tot_f, cums[1][c - 1:c, :]], axis=1)
        k = k_ref[rows, hd * dk:(hd + 1) * dk].astype(F32)
        kd = jnp.concatenate([k * _exp2_neg_abs(tot_f - cums[0]), k * _exp2_neg_abs(cums[1] - las[1])], axis=1)
        kv = lax.dot_general(kd.astype(BF16), v_ref[rows, hd * dv:(hd + 1) * dv].astype(BF16),
                             (((0,), (0,)), ((), ())), preferred_element_type=F32)
        return las, cums, tot, kv

    def pass2(hd, rows, las, cums, tot, s_in):
        laf, lab = las
        cumf, cumb = cums
        cumbx = cumb - lab
        q = q_ref[rows, hd * dk:(hd + 1) * dk].astype(F32) * (DK_A ** -0.5)
        k = k_ref[rows, hd * dk:(hd + 1) * dk].astype(F32)
        v = v_ref[rows, hd * dv:(hd + 1) * dv].astype(BF16)
        msb = msb_ref[...]
        q2 = (2.0 * q).astype(BF16)
        kb = k.astype(BF16)
        scd = [_dot_nt(q2[0:hb], kb[0:hb]), _dot_nt(q2[hb:], kb[hb:])]
        a_lo = a_up = None
        for lvl in range(nl):
            b = 1 << lvl
            second = (row & b) != 0
            if lvl == 0:
                qq = q * jnp.exp2(jnp.where(second, laf, lab))
                kk = k
            elif lvl == 1:
                p = row & 3
                dq = jnp.where(p == 0, lab + pltpu.roll(lab, c - 1, 0),
                               jnp.where(p == 1, lab, jnp.where(p == 2, laf, laf + pltpu.roll(laf, 1, 0))))
                dkk = jnp.where(p == 0, pltpu.roll(laf, c - 1, 0), jnp.where(p == 3, pltpu.roll(lab, 1, 0), 0.0))
                qq = q * jnp.exp2(dq)
                kk = k * jnp.exp2(dkk)
            else:
                n = c // (2 * b)
                shp = (n, 2 * b, dk)
                cf3 = cumf.reshape(shp)
                cb3 = cumb.reshape(shp)
                ef = _exp2_neg_abs(cf3 - cf3[:, b - 1:b, :])
                eb = _exp2_neg_abs(cumbx.reshape(shp) - cb3[:, b - 1:b, :])
                if b >= SUBLANES:
                    eq = jnp.concatenate([eb[:, :b], ef[:, b:]], axis=1).reshape(c, dk)
                    ek = jnp.concatenate([ef[:, :b], eb[:, b:]], axis=1).reshape(c, dk)
                else:
                    ef = ef.reshape(c, dk)
                    eb = eb.reshape(c, dk)
                    eq = jnp.where(second, ef, eb)
                    ek = jnp.where(second, eb, ef)
                qq = q * eq
                kk = k * ek
            qq = qq.astype(BF16)
            kk = kk.astype(BF16)
            if lvl < nl - 1:
                for blk in range(2):
                    a = _dot_nt(qq[blk * hb:(blk + 1) * hb], kk[blk * hb:(blk + 1) * hb])
                    scd[blk] = jnp.where(msb == lvl, a, scd[blk])
            else:
                a_lo = _dot_nt(qq[hb:], kk[:hb])
                a_up = _dot_nt(qq[:hb], kk[hb:])
        sc = jnp.concatenate([jnp.concatenate([scd[0], a_up], axis=1),
                              jnp.concatenate([a_lo, scd[1]], axis=1)], axis=0)
        o = jnp.dot(sc.astype(BF16), v, preferred_element_type=F32)
        if s_in is not None:
            qd = jnp.concatenate([q * _exp2_neg_abs(cumf), q * _exp2_neg_abs(tot[:, dk:] - cumbx)], axis=1)
            o = o + jnp.dot(qd.astype(BF16), s_in.astype(BF16), preferred_element_type=F32)
        o = o * lax.rsqrt(jnp.mean(o * o, axis=-1, keepdims=True) + EPS) * ng_ref[...]
        ga = ga_ref[rows, hd * dv:(hd + 1) * dv].astype(F32)
        y_ref[rows, hd * dv:(hd + 1) * dv] = (o * (ga * jax.nn.sigmoid(ga))).astype(y_ref.dtype)

    if not has_s0:
        rows = slice(0, c)
        for hd in range(heads):
            las, cums, tot, kv = pass1(hd, rows)
            st_ref[0, hd] = kv[0:dk]
            st_ref[1, hd] = kv[dk:2 * dk]
            pass2(hd, rows, las, cums, tot, None)
        return

    la_sc, cum_sc, kv_sc, tot_sc, sin_sc = rest
    assert heads == 1

    for ci in range(n_chunks):
        rows = slice(ci * c, (ci + 1) * c)
        las, cums, tot, kv = pass1(0, rows)
        for d in range(2):
            la_sc[d, rows, :] = las[d]
            cum_sc[d, rows, :] = cums[d]
        tot_sc[ci] = tot
        kv_sc[ci] = kv

    eye = lax.broadcasted_iota(jnp.int32, (2 * dk, 2 * dk), 0) == lax.broadcasted_iota(
        jnp.int32, (2 * dk, 2 * dk), 1)

    def decay_col(ci):
        tot = jnp.broadcast_to(tot_sc[ci], (2 * dk, 2 * dk))
        return jnp.exp2(jnp.sum(jnp.where(eye, tot, 0.0), axis=1, keepdims=True))

    s = s0_ref[0]
    for ci in range(n_chunks):
        sin_sc[ci, 0:dk, :] = s
        if ci < n_chunks - 1:
            s = decay_col(ci)[0:dk] * s + kv_sc[ci, 0:dk, :]
    s = s0_ref[1]
    for ci in reversed(range(n_chunks)):
        sin_sc[ci, dk:2 * dk, :] = s
        if ci > 0:
            s = decay_col(ci)[dk:2 * dk] * s + kv_sc[ci, dk:2 * dk, :]

    for ci in range(n_chunks):
        rows = slice(ci * c, (ci + 1) * c)
        pass2(0, rows, (la_sc[0, rows, :], la_sc[1, rows, :]), (cum_sc[0, rows, :], cum_sc[1, rows, :]),
              tot_sc[ci], sin_sc[ci])


def gla(z, up_pad, bias, norm_g, tril, msb, l, *, row0, nb, seq, s0):
    rb0 = row0 // seq
    n_chunks = seq // GLA_CHUNK
    has_s0 = s0 is not None
    assert has_s0 or n_chunks == 1
    heads = 1 if has_s0 else H_A
    hb = GLA_CHUNK // 2
    if has_s0:
        grid = (nb, H_A)
        im = lambda f: (lambda b, h: f(b, h))
    else:
        grid = (nb,)
        im = lambda f: (lambda b: f(b, 0))
    nk, nv = heads * DK_A, heads * DV_A
    in_specs = [
        pl.BlockSpec((seq, nk), im(lambda b, h: (rb0 + b, Z_QA // nk + h))),
        pl.BlockSpec((seq, nk), im(lambda b, h: (rb0 + b, Z_KA // nk + h))),
        pl.BlockSpec((seq, nv), im(lambda b, h: (rb0 + b, Z_VA // nv + h))),
        pl.BlockSpec((seq, nv), im(lambda b, h: (rb0 + b, Z_GA // nv + h))),
        pl.BlockSpec((seq, 128), im(lambda b, h: (rb0 + b, Z_GK // 128))),
        pl.BlockSpec((None, 2, 128, nk), im(lambda b, h: (l, 0, 0, h))),
        pl.BlockSpec((None, 2, 1, nk), im(lambda b, h: (l, 0, 0, h))),
        pl.BlockSpec((None, 1, DV_A), im(lambda b, h: (l, 0, 0))),
        pl.BlockSpec((GLA_CHUNK, GLA_CHUNK), im(lambda b, h: (0, 0))),
        pl.BlockSpec((hb, hb), im(lambda b, h: (0, 0))),
    ]
    args = [z, z, z, z, z, up_pad, bias.reshape(DEPTH, 2, 1, H_A * DK_A),
            norm_g.reshape(DEPTH, 1, DV_A), tril, msb]
    out_shape = [jax.ShapeDtypeStruct((nb * seq, H_A * DV_A), BF16)]
    out_specs = [pl.BlockSpec((seq, nv), im(lambda b, h: (b, h)))]
    scratch = []
    if has_s0:
        in_specs.append(pl.BlockSpec((None, None, 2, None, DK_A, DV_A), im(lambda b, h: (b, l, 0, h, 0, 0))))
        args.append(s0)
        scratch = [pltpu.VMEM((2, seq, DK_A), F32), pltpu.VMEM((2, seq, DK_A), F32),
                   pltpu.VMEM((n_chunks, 2 * DK_A, DV_A), F32),
                   pltpu.VMEM((n_chunks, 1, 2 * DK_A), F32),
                   pltpu.VMEM((n_chunks, 2 * DK_A, DV_A), F32)]
    else:
        out_shape.append(jax.ShapeDtypeStruct((nb, 2, H_A, DK_A, DV_A), F32))
        out_specs.append(pl.BlockSpec((None, 2, H_A, DK_A, DV_A), im(lambda b, h: (b, 0, 0, 0, 0))))
    return pl.pallas_call(
        functools.partial(_gla_kernel, n_chunks=n_chunks, has_s0=has_s0, heads=heads),
        out_shape=out_shape,
        grid=grid,
        in_specs=in_specs,
        out_specs=out_specs,
        scratch_shapes=scratch,
        compiler_params=_cparams(len(grid)),
        name="gla",
    )(*args)


def kernel(x_prompt, x_sample, c, state_gla, cache_mla_ckv, cache_mla_krope, cache_gqa_k, cache_gqa_v,
           c_ctx, norm1_g, norm2_g, w_ada, b_ada, w_in, gla_gk_up, gla_gk_bias, gla_norm_g,
           mla_qa_norm_g, mla_w_qb, mla_kva_norm_g, mla_w_kvb, mla_q_norm_g, mla_k_norm_g,
           gqa_q_norm_g, gqa_k_norm_g, w_out_a, w_out_b, w_out_c, w_o,
           ffn_w_up, ffn_conv_w, ffn_conv_b, ffn_w_down):
    d = D_MODEL
    x = jnp.concatenate([x_prompt.reshape(T_P, d), x_sample.reshape(T_S, d)], axis=0)
    cvec = jnp.concatenate([c_ctx[None, :], c, jnp.zeros((MOD_ROWS - N_GROUPS, d), F32)], axis=0)
    mods = adaln_all(cvec, w_ada, b_ada)

    w_qb_pad = jnp.pad(mla_w_qb.reshape(DEPTH, Q_LORA_B, H_B, QK_B),
                       ((0, 0), (0, 0), (0, 0), (0, PAD_QK_B - QK_B))).reshape(DEPTH, Q_LORA_B, H_B * PAD_QK_B)
    q_g_pad = jnp.pad(mla_q_norm_g, ((0, 0), (0, PAD_QK_B - QK_B))).reshape(DEPTH, 1, PAD_QK_B)
    k_g_pad = jnp.pad(mla_k_norm_g, ((0, 0), (0, PAD_QK_B - QK_B))).reshape(DEPTH, 1, PAD_QK_B)
    up_pad = jnp.zeros((DEPTH, 2, 128, H_A * DK_A), F32)
    for dd in range(2):
        r0 = dd * GLA_GATE_RANK
        up_pad = up_pad.at[:, dd, r0:r0 + GLA_GATE_RANK, :].set(gla_gk_up[:, dd])
    tab_b = _rope_tables(DEC_SEQ, ROPE_B, 128)
    tab_c = _rope_tables(DEC_SEQ, HD_C, 128)
    tril, msb = _gla_constants()
    w_in_t = jnp.swapaxes(w_in, 1, 2)
    ck = cache_gqa_k.reshape(DEC_BATCH, DEPTH, PAST_LEN, HKV_C * HD_C)
    cv = cache_gqa_v.reshape(DEC_BATCH, DEPTH, PAST_LEN, HKV_C * HD_C)

    st_out, ckv_out, krope_out, kc_out, vc_out = [], [], [], [], []
    for l in range(DEPTH):
        h = norm_mod(x, norm1_g, mods, l, 0, 1)
        z = w_in_proj(h, w_in_t, l)

        ya_p, st = gla(z, up_pad, gla_gk_bias, gla_norm_g, tril, msb, l,
                       row0=0, nb=BATCH, seq=SEQ, s0=None)
        ya_s, = gla(z, up_pad, gla_gk_bias, gla_norm_g, tril, msb, l,
                    row0=T_P, nb=DEC_BATCH, seq=DEC_SEQ, s0=state_gla)

        bmk = 1024
        qb_p = mla_q_proj(z, mla_qa_norm_g, w_qb_pad, q_g_pad, l, row0=0, rows=T_P, tab=None)
        qb_s = mla_q_proj(z, mla_qa_norm_g, w_qb_pad, q_g_pad, l, row0=T_P, rows=T_S, tab=tab_b)

        def tail_specs(row0):
            rb = row0 // bmk
            return (pl.BlockSpec((bmk, KV_LORA_B), lambda i: (rb + i, Z_CKV // KV_LORA_B)),
                    pl.BlockSpec((bmk, 128), lambda i: (rb + i, Z_KROPE // 128)))

        c_spec, kr_spec = tail_specs(0)
        kb_p, vb_p, ckv_p = mla_kv_proj(z, c_spec, z, kr_spec, mla_kva_norm_g, mla_w_kvb, k_g_pad, l,
                                        rows=T_P, bm=bmk, norm_in=True, tab=None)
        c_spec, kr_spec = tail_specs(T_P)
        kb_s, vb_s, _ = mla_kv_proj(z, c_spec, z, kr_spec, mla_kva_norm_g, mla_w_kvb, k_g_pad, l,
                                    rows=T_S, bm=bmk, norm_in=True, tab=tab_b)
        kb_c, vb_c = mla_kv_proj(
            cache_mla_ckv, pl.BlockSpec((None, None, PAST_LEN, KV_LORA_B), lambda i: (i, l, 0, 0)),
            cache_mla_krope, pl.BlockSpec((None, None, PAST_LEN, ROPE_B), lambda i: (i, l, 0, 0)),
            mla_kva_norm_g, mla_w_kvb, k_g_pad, l, rows=DEC_BATCH * PAST_LEN, bm=PAST_LEN, norm_in=False, tab=None)
        nkb, nvb = H_B * PAD_QK_B, H_B * V_B
        yb_p = attention(qb_p, [(kb_p, _rows_spec(SEQ, nkb), vb_p, _rows_spec(SEQ, nvb), SEQ)],
                         nb=BATCH, q_rows=SEQ, hq=H_B, hkv=H_B, dqk=PAD_QK_B, dv=V_B, name="mla_attn_p")
        yb_s = attention(qb_s, [(kb_s, _rows_spec(DEC_SEQ, nkb), vb_s, _rows_spec(DEC_SEQ, nvb), DEC_SEQ),
                                (kb_c, _rows_spec(PAST_LEN, nkb), vb_c, _rows_spec(PAST_LEN, nvb), PAST_LEN)],
                         nb=DEC_BATCH, q_rows=DEC_SEQ, hq=H_B, hkv=H_B, dqk=PAD_QK_B, dv=V_B, name="mla_attn_s")

        qc_p, kc_p, vc_p, kn_p = gqa_prep(z, gqa_q_norm_g, gqa_k_norm_g, l, row0=0, rows=T_P,
                                          tab=None, emit_kn=True)
        qc_s, kc_s, vc_s = gqa_prep(z, gqa_q_norm_g, gqa_k_norm_g, l, row0=T_P, rows=T_S,
                                    tab=tab_c, emit_kn=False)
        nkc = HKV_C * HD_C
        yc_p = attention(qc_p, [(kc_p, _rows_spec(SEQ, nkc), vc_p, _rows_spec(SEQ, nkc), SEQ)],
                         nb=BATCH, q_rows=SEQ, hq=HQ_C, hkv=HKV_C, dqk=HD_C, dv=HD_C, name="gqa_attn_p")
        cache_spec = pl.BlockSpec((None, None, PAST_LEN, nkc), lambda b: (b, l, 0, 0))
        yc_s = attention(qc_s, [(kc_s, _rows_spec(DEC_SEQ, nkc), vc_s, _rows_spec(DEC_SEQ, nkc), DEC_SEQ),
                                (ck, cache_spec, cv, cache_spec, PAST_LEN)],
                         nb=DEC_BATCH, q_rows=DEC_SEQ, hq=HQ_C, hkv=HKV_C, dqk=HD_C, dv=HD_C, name="gqa_attn_s")

        m = gated_merge((ya_p, ya_s, yb_p, yb_s, yc_p, yc_s), z, w_out_a, w_out_b, w_out_c, l)
        x, h2 = w_o_norm(m, w_o, l, x, mods, norm2_g)
        act = up_conv(h2, ffn_w_up, ffn_conv_w, ffn_conv_b, l)
        x = matmul_residual(act, ffn_w_down, l, x, mods, 5, bm=512, bn=512, name="ffn_down")

        st_out.append(st)
        ckv_out.append(ckv_p.reshape(BATCH, SEQ, KV_LORA_B))
        krope_out.append(z[:T_P, Z_KROPE:Z_KROPE + ROPE_B].astype(F32).reshape(BATCH, SEQ, ROPE_B))
        kc_out.append(kn_p.reshape(BATCH, SEQ, HKV_C, HD_C))
        vc_out.append(z[:T_P, Z_VC:Z_VC + nkc].astype(F32).reshape(BATCH, SEQ, HKV_C, HD_C))

    y_p = x[:T_P].reshape(BATCH, SEQ, d)
    y_s = x[T_P:].reshape(DEC_BATCH, DEC_SEQ, d)
    return (y_p, y_s, jnp.stack(st_out, axis=1), jnp.stack(ckv_out, axis=1), jnp.stack(krope_out, axis=1),
            jnp.stack(kc_out, axis=1), jnp.stack(vc_out, axis=1))
```
